```python
import jax, jax.numpy as jnp
from jax import lax
import numpy as np

D_MODEL = 1024
BATCH = 4
SEQ = 4096
DEPTH = 2
DEC_BATCH = 32
DEC_SEQ = 4
PAST_LEN = 16384
PAGE_SIZE = 128

N_HEADS = 8
HEAD_DIM = D_MODEL // N_HEADS
ATTN_DIM = N_HEADS * HEAD_DIM
MOBA_BLOCK = 256
MOBA_TOP_K = 3
Q_BLOCK = 64
ROPE_DIM = HEAD_DIM // 4
ROPE_THETA = 500000.0
CONV_WIDTH = 3
D_FF = 4 * D_MODEL
N_MIXERS = 2
N_ATTN_LAYERS = (DEPTH + 1) // 2
N_CONV_LAYERS = DEPTH // 2
RMS_EPS = 1e-6

kernel_name = 'hybrid_moba_shortconv_adaln_decoder_step'


def rms_norm(x, g):
    xf = x.astype(jnp.float32)
    y = xf * lax.rsqrt(jnp.mean(xf * xf, axis=-1, keepdims=True) + RMS_EPS)
    return (y * g.astype(jnp.float32)).astype(x.dtype)


def ada_modulation(c, w, b):
    m = (c @ w + b)[:, None, :]
    return jnp.split(m, 6, axis=-1)


def apply_partial_rope(x, pos):
    half = ROPE_DIM // 2
    inv_freq = ROPE_THETA ** (-jnp.arange(0, ROPE_DIM, 2, dtype=jnp.float32) / ROPE_DIM)
    ang = pos.astype(jnp.float32)[:, None] * inv_freq[None, :]
    cos = jnp.cos(ang)[:, None, :]
    sin = jnp.sin(ang)[:, None, :]
    xr = x[..., :ROPE_DIM].astype(jnp.float32)
    x1, x2 = xr[..., :half], xr[..., half:]
    rot = jnp.concatenate([x1 * cos - x2 * sin, x2 * cos + x1 * sin], axis=-1).astype(x.dtype)
    return jnp.concatenate([rot, x[..., ROPE_DIM:]], axis=-1)


def to_blocks(x):
    L = x.shape[-3]
    nb = -(-L // MOBA_BLOCK)
    pad = [(0, 0)] * (x.ndim - 3) + [(0, nb * MOBA_BLOCK - L), (0, 0), (0, 0)]
    xb = jnp.pad(x, pad).reshape(x.shape[:-3] + (nb, MOBA_BLOCK, N_HEADS, HEAD_DIM))
    return jnp.moveaxis(xb, -2, -4)


def moba_attend(q, q_pos, kb, vb, k_mean):
    nq = q.shape[0]
    nb = kb.shape[1]
    q_blk = q_pos // MOBA_BLOCK
    gate = jnp.einsum('qhd,hnd->qhn', q.astype(jnp.float32), k_mean)
    fully_past = jnp.arange(nb)[None, None, :] < q_blk[:, None, None]
    gate = jnp.where(fully_past, gate, -jnp.inf)
    top_s, top_i = lax.top_k(gate, min(MOBA_TOP_K, nb))
    own = jnp.broadcast_to(q_blk[:, None, None], (nq, N_HEADS, 1)).astype(jnp.int32)
    blk = jnp.concatenate([top_i.astype(jnp.int32), own], axis=-1)
    blk_ok = jnp.concatenate([jnp.isfinite(top_s), jnp.ones(own.shape, dtype=bool)], axis=-1)
    h_idx = jnp.arange(N_HEADS)[None, :, None]
    k_sel = kb[h_idx, blk]
    v_sel = vb[h_idx, blk]
    logits = jnp.einsum('qhd,qhsbd->qhsb', q, k_sel, preferred_element_type=jnp.float32) * (HEAD_DIM ** -0.5)
    k_pos = blk[..., None] * MOBA_BLOCK + jnp.arange(MOBA_BLOCK, dtype=jnp.int32)
    mask = blk_ok[..., None] & (k_pos <= q_pos[:, None, None, None])
    logits = jnp.where(mask, logits, -jnp.inf)
    shp = logits.shape
    p = jax.nn.softmax(logits.reshape(shp[0], shp[1], shp[2] * shp[3]), axis=-1).reshape(shp)
    return jnp.einsum('qhsb,qhsbd->qhd', p.astype(v_sel.dtype), v_sel)


def split_qkv(h, w_qkv, pos):
    B, S, _ = h.shape
    qkv = (h @ w_qkv).reshape(B, S, 3, N_HEADS, HEAD_DIM)
    q = apply_partial_rope(qkv[:, :, 0], pos)
    k = apply_partial_rope(qkv[:, :, 1], pos)
    return q, k, qkv[:, :, 2]


def moba_prompt(h, w_qkv, w_o):
    B, S, _ = h.shape
    pos = jnp.arange(S, dtype=jnp.int32)
    q, k, v = split_qkv(h, w_qkv, pos)
    kb = to_blocks(k)
    vb = to_blocks(v)
    k_mean = jnp.mean(kb.astype(jnp.float32), axis=-2)
    n_chunks = S // Q_BLOCK
    ids = jnp.arange(B * n_chunks, dtype=jnp.int32)

    def one(idx):
        b = idx // n_chunks
        start = (idx % n_chunks) * Q_BLOCK
        q_c = lax.dynamic_slice_in_dim(q[b], start, Q_BLOCK, axis=0)
        pos_c = start + jnp.arange(Q_BLOCK, dtype=jnp.int32)
        return moba_attend(q_c, pos_c, kb[b], vb[b], k_mean[b])

    o = lax.map(one, ids).reshape(B, S, ATTN_DIM)
    return o @ w_o, k, v


def moba_sample(h, cache_k, cache_v, layer, page_table, w_qkv, w_o):
    Bd, T, _ = h.shape
    past = page_table.shape[1] * PAGE_SIZE
    pos = past + jnp.arange(T, dtype=jnp.int32)
    q, k, v = split_qkv(h, w_qkv, pos)

    def one(args):
        q_b, k_b, v_b, pt = args
        k_all = jnp.concatenate([cache_k[layer, pt].reshape(past, N_HEADS, HEAD_DIM), k_b], axis=0)
        v_all = jnp.concatenate([cache_v[layer, pt].reshape(past, N_HEADS, HEAD_DIM), v_b], axis=0)
        kb = to_blocks(k_all)
        vb = to_blocks(v_all)
        k_mean = jnp.mean(kb.astype(jnp.float32), axis=-2)
        return moba_attend(q_b, pos, kb, vb, k_mean)

    o = lax.map(one, (q, k, v, page_table)).reshape(Bd, T, ATTN_DIM)
    return o @ w_o, k, v


def short_conv_mixer(h, conv_state, w_bcx, w_conv, w_out):
    bg, cg, xin = jnp.split(h @ w_bcx, 3, axis=-1)
    u = cg * xin
    u_ext = jnp.concatenate([conv_state, u], axis=1)
    conv = lax.conv_general_dilated(u_ext, w_conv[:, None, :], window_strides=(1,), padding='VALID',
                                    dimension_numbers=('NWC', 'WIO', 'NWC'), feature_group_count=D_MODEL)
    y = (bg * conv) @ w_out
    return y, u_ext[:, -(CONV_WIDTH - 1):, :]


def sq_relu_mlp(h, w_up, w_down):
    return jnp.square(jax.nn.relu(h @ w_up)) @ w_down


def setup_inputs(seed: int = 0) -> dict:
    key = jax.random.key(seed)
    ks = jax.random.split(key, 19)
    f32 = jnp.float32
    n_pages = PAST_LEN // PAGE_SIZE
    n_used = DEC_BATCH * n_pages
    n_pool = n_used + (n_used + 3) // 4

    def nrm(k, shape, s):
        return jax.random.normal(k, shape, f32) * s

    page_table = jax.random.permutation(ks[5], n_pool)[:n_used].reshape(DEC_BATCH, n_pages).astype(jnp.int32)
    return {
        'x_prompt': nrm(ks[0], (BATCH, SEQ, D_MODEL), 1.0),
        'x_sample': nrm(ks[1], (DEC_BATCH, DEC_SEQ, D_MODEL), 1.0),
        'cache_k': nrm(ks[2], (N_ATTN_LAYERS, n_pool, PAGE_SIZE, N_HEADS, HEAD_DIM), 1.0),
        'cache_v': nrm(ks[3], (N_ATTN_LAYERS, n_pool, PAGE_SIZE, N_HEADS, HEAD_DIM), 1.0),
        'state_conv': nrm(ks[4], (N_CONV_LAYERS, DEC_BATCH, CONV_WIDTH - 1, D_MODEL), 1.0),
        'page_table': page_table,
        'c_prompt': nrm(ks[6], (BATCH, D_MODEL), 1.0),
        'c_sample': nrm(ks[7], (DEC_BATCH, D_MODEL), 1.0),
        'w_ada': nrm(ks[8], (DEPTH, D_MODEL, 6 * D_MODEL), 0.5 * D_MODEL ** -0.5),
        'b_ada': nrm(ks[9], (DEPTH, 6 * D_MODEL), 0.02),
        'g_norm': 1.0 + nrm(ks[10], (DEPTH, 2, D_MODEL), 0.05),
        'w_qkv': nrm(ks[11], (N_ATTN_LAYERS, D_MODEL, 3 * ATTN_DIM), D_MODEL ** -0.5),
        'w_o': nrm(ks[12], (N_ATTN_LAYERS, ATTN_DIM, D_MODEL), ATTN_DIM ** -0.5),
        'w_bcx': nrm(ks[13], (N_CONV_LAYERS, D_MODEL, 3 * D_MODEL), D_MODEL ** -0.5),
        'w_conv': nrm(ks[14], (N_CONV_LAYERS, CONV_WIDTH, D_MODEL), CONV_WIDTH ** -0.5),
        'w_out': nrm(ks[15], (N_CONV_LAYERS, D_MODEL, D_MODEL), D_MODEL ** -0.5),
        'w_up': nrm(ks[16], (DEPTH, D_MODEL, D_FF), D_MODEL ** -0.5),
        'w_down': nrm(ks[17], (DEPTH, D_FF, D_MODEL), D_FF ** -0.5),
        'g_final': 1.0 + nrm(ks[18], (D_MODEL,), 0.05),
    }


def reference(x_prompt, x_sample, cache_k, cache_v, state_conv, page_table, c_prompt, c_sample,
              w_ada, b_ada, g_norm, w_qkv, w_o, w_bcx, w_conv, w_out, w_up, w_down, g_final):
    xp, xs = x_prompt, x_sample
    k_p, v_p, k_s, v_s, cv_p, cv_s = [], [], [], [], [], []
    for i in range(DEPTH):
        sh1p, sc1p, g1p, sh2p, sc2p, g2p = ada_modulation(c_prompt, w_ada[i], b_ada[i])
        sh1s, sc1s, g1s, sh2s, sc2s, g2s = ada_modulation(c_sample, w_ada[i], b_ada[i])
        hp = rms_norm(xp, g_norm[i, 0]) * (1 + sc1p) + sh1p
        hs = rms_norm(xs, g_norm[i, 0]) * (1 + sc1s) + sh1s
        j = i // N_MIXERS
        if i % N_MIXERS == 0:
            op, kp_new, vp_new = moba_prompt(hp, w_qkv[j], w_o[j])
            os_, ks_new, vs_new = moba_sample(hs, cache_k, cache_v, j, page_table, w_qkv[j], w_o[j])
            k_p.append(kp_new)
            v_p.append(vp_new)
            k_s.append(ks_new)
            v_s.append(vs_new)
        else:
            zero_state = jnp.zeros((hp.shape[0], CONV_WIDTH - 1, D_MODEL), hp.dtype)
            op, stp = short_conv_mixer(hp, zero_state, w_bcx[j], w_conv[j], w_out[j])
            os_, sts = short_conv_mixer(hs, state_conv[j], w_bcx[j], w_conv[j], w_out[j])
            cv_p.append(stp)
            cv_s.append(sts)
        xp = xp + g1p * op
        xs = xs + g1s * os_
        hp = rms_norm(xp, g_norm[i, 1]) * (1 + sc2p) + sh2p
        hs = rms_norm(xs, g_norm[i, 1]) * (1 + sc2s) + sh2s
        xp = xp + g2p * sq_relu_mlp(hp, w_up[i], w_down[i])
        xs = xs + g2s * sq_relu_mlp(hs, w_up[i], w_down[i])
    y_prompt = rms_norm(xp, g_final)
    y_sample = rms_norm(xs, g_final)
    return (y_prompt, y_sample, jnp.stack(k_p), jnp.stack(v_p), jnp.stack(k_s), jnp.stack(v_s), jnp.stack(cv_p), jnp.stack(cv_s))
```

```python
import functools

import numpy as np
import jax
import jax.numpy as jnp
from jax import lax
from jax.experimental import pallas as pl
from jax.experimental.pallas import tpu as pltpu

F32 = jnp.float32
BF16 = jnp.bfloat16

HEAD_DIM = 128
MOBA_BLOCK = 256
MOBA_TOP_K = 3
ROPE_DIM = HEAD_DIM // 4
ROPE_THETA = 500000.0
RMS_EPS = 1e-6
SOFTMAX_SCALE = HEAD_DIM ** -0.5
MASKED = -1e30

LANES = 128
VMEM_LIMIT = 56 * 1024 * 1024
ROW_TILE = 256
PAGES_PER_STEP = 16
ADA_COL_TILE = 1536

NT_DIMS = (((1,), (1,)), ((), ()))


def _nt_dot(a, b):
    return lax.dot_general(a, b, NT_DIMS, preferred_element_type=F32)


def _split_bf16(x):
    hi = x.astype(BF16)
    lo = (x - hi.astype(F32)).astype(BF16)
    return hi, lo


def _const_spec(shape):
    n = len(shape)
    return pl.BlockSpec(shape, lambda *_: (0,) * n, pipeline_mode=pl.Buffered(1))


def _params(*sem):
    return pltpu.CompilerParams(dimension_semantics=sem, vmem_limit_bytes=VMEM_LIMIT)


def _ada_kernel(c_ref, w_ref, b_ref, o_ref):
    c_hi, c_lo = _split_bf16(c_ref[...])
    w_hi, w_lo = _split_bf16(w_ref[0])
    acc = jnp.dot(c_hi, w_hi, preferred_element_type=F32)
    acc += jnp.dot(c_lo, w_hi, preferred_element_type=F32)
    acc += jnp.dot(c_hi, w_lo, preferred_element_type=F32)
    o_ref[0] = acc + b_ref[0]


def _ada(c_all, w_ada, b_ada):
    depth, d, n6 = w_ada.shape
    rows = c_all.shape[0]
    tn = ADA_COL_TILE
    return pl.pallas_call(
        _ada_kernel,
        grid=(depth, n6 // tn),
        in_specs=[
            pl.BlockSpec((rows, d), lambda l, n: (0, 0)),
            pl.BlockSpec((1, d, tn), lambda l, n: (l, 0, n)),
            pl.BlockSpec((1, 1, tn), lambda l, n: (l, 0, n)),
        ],
        out_specs=pl.BlockSpec((1, rows, tn), lambda l, n: (l, 0, n)),
        out_shape=jax.ShapeDtypeStruct((depth, rows, n6), F32),
        compiler_params=_params("arbitrary", "arbitrary"),
        name="ada",
    )(c_all, w_ada, b_ada.reshape(depth, 1, n6))


def _norm_mod(x, g, shift, scale):
    ms = jnp.mean(x * x, axis=-1, keepdims=True)
    y = x * lax.rsqrt(ms + RMS_EPS) * g
    return y * (1.0 + scale) + shift


def _mlp(h, wup_ref, wdown_ref):
    d, dff = wup_ref.shape
    hb = h.astype(BF16)
    acc = jnp.zeros((h.shape[0], d), F32)
    for c in range(0, dff, d):
        u = jnp.dot(hb, wup_ref[:, c:c + d], preferred_element_type=F32)
        u = jnp.square(jnp.maximum(u, 0.0)).astype(BF16)
        acc += jnp.dot(u, wdown_ref[c:c + d, :], preferred_element_type=F32)
    return acc


def _rope_tables(pos):
    half = ROPE_DIM // 2
    inv = np.float32(ROPE_THETA) ** (-np.arange(0, ROPE_DIM, 2, dtype=np.float32) / np.float32(ROPE_DIM))
    ang = (pos.astype(np.float32)[:, None] * inv.astype(np.float32)[None, :]).astype(np.float32)
    cos = np.cos(ang.astype(np.float64))
    sin = np.sin(ang.astype(np.float64))
    n = pos.shape[0]
    c = np.ones((n, HEAD_DIM), np.float64)
    s1 = np.zeros((n, HEAD_DIM), np.float64)
    s2 = np.zeros((n, HEAD_DIM), np.float64)
    c[:, :half] = cos
    c[:, half:ROPE_DIM] = cos
    s1[:, :half] = -sin
    s2[:, half:ROPE_DIM] = sin
    return tuple(jnp.asarray(t.astype(np.float32)) for t in (c, s1, s2))


def _qkv_kernel(x_ref, mod_ref, g_ref, w_ref, cos_ref, s1_ref, s2_ref, q_ref, k5_ref, v5_ref, kb_ref, vb_ref,
                *ks_ref, attn_dim):
    d = x_ref.shape[1]
    m = mod_ref[0]
    h = _norm_mod(x_ref[...], g_ref[...], m[:, 0:d], m[:, d:2 * d]).astype(BF16)
    cos, s1, s2 = cos_ref[...], s1_ref[...], s2_ref[...]
    tm = h.shape[0]
    heads = attn_dim // HEAD_DIM
    cw = 2 * HEAD_DIM
    for c in range(0, 3 * attn_dim, cw):
        r = jnp.dot(h, w_ref[:, c:c + cw], preferred_element_type=F32)
        if c < 2 * attn_dim:
            rot = []
            for u in range(0, cw, HEAD_DIM):
                z = r[:, u:u + HEAD_DIM]
                z = (z * cos + pltpu.roll(z, HEAD_DIM - ROPE_DIM // 2, 1) * s1
                     + pltpu.roll(z, ROPE_DIM // 2, 1) * s2)
                rot.append(z)
            r = jnp.concatenate(rot, axis=1)
        if c < attn_dim:
            q_ref[:, c:c + cw] = r
            continue
        is_k = c < 2 * attn_dim
        c0 = c - (attn_dim if is_k else 2 * attn_dim)
        (kb_ref if is_k else vb_ref)[:, c0:c0 + cw] = r.astype(BF16)
        for u in range(0, cw, HEAD_DIM):
            head = (c0 + u) // HEAD_DIM
            (k5_ref if is_k else v5_ref)[pl.ds(head, tm, stride=heads), :] = r[:, u:u + HEAD_DIM]
        if is_k and ks_ref:
            nblk = tm // MOBA_BLOCK
            ks_ref[0][0, :, c0:c0 + cw] = jnp.sum(r.reshape(nblk, MOBA_BLOCK, cw), axis=1)


def _qkv(x, mod, g, w, tables, *, tm, tiles_per_seq, with_ksum):
    rows, d = x.shape
    attn_dim = w.shape[1] // 3
    heads = attn_dim // HEAD_DIM
    n_tiles = rows // tm
    mod_rows = mod.shape[1]
    row_spec = pl.BlockSpec((tm, attn_dim), lambda i: (i, 0))
    tok_head_spec = pl.BlockSpec((tm * heads, HEAD_DIM), lambda i: (i, 0))
    tab_spec = pl.BlockSpec((tm, HEAD_DIM), lambda i: (i % tiles_per_seq, 0))
    out_specs = [row_spec, tok_head_spec, tok_head_spec, row_spec, row_spec]
    out_shape = ([jax.ShapeDtypeStruct((rows, attn_dim), F32)]
                 + [jax.ShapeDtypeStruct((rows * heads, HEAD_DIM), F32)] * 2
                 + [jax.ShapeDtypeStruct((rows, attn_dim), BF16)] * 2)
    if with_ksum:
        nblk = tm // MOBA_BLOCK
        out_specs.append(pl.BlockSpec((1, nblk, attn_dim), lambda i: (i, 0, 0)))
        out_shape.append(jax.ShapeDtypeStruct((n_tiles, nblk, attn_dim), F32))
    return pl.pallas_call(
        functools.partial(_qkv_kernel, attn_dim=attn_dim),
        grid=(n_tiles,),
        in_specs=[
            pl.BlockSpec((tm, d), lambda i: (i, 0)),
            pl.BlockSpec((1, mod_rows, mod.shape[2]), lambda i: (i // tiles_per_seq, 0, 0)),
            _const_spec((1, d)),
            _const_spec(w.shape),
            tab_spec, tab_spec, tab_spec,
        ],
        out_specs=out_specs,
        out_shape=out_shape,
        compiler_params=_params("arbitrary"),
        name="qkv",
    )(x, mod, g, w, *tables)


def _attn_prompt_kernel(q_ref, k_ref, vb, ks_ref, o_ref, kaug, qaug, m_sc, l_sc, acc_sc):
    i = pl.program_id(2)
    tq = q_ref.shape[0]
    seq = k_ref.shape[0]
    nb = seq // MOBA_BLOCK

    @pl.when(i == 0)
    def _():
        def fill(j, carry):
            rows = pl.ds(pl.multiple_of(j * MOBA_BLOCK, MOBA_BLOCK), MOBA_BLOCK)
            kaug[rows, 0:HEAD_DIM] = k_ref[rows, :]
            lane = lax.broadcasted_iota(jnp.int32, (MOBA_BLOCK, LANES), 1)
            kaug[rows, HEAD_DIM:] = jnp.where(lane == j, 1.0, 0.0).astype(BF16)
            return carry
        lax.fori_loop(0, nb, fill, 0)

    q = q_ref[...]
    qs = (q * SOFTMAX_SCALE).astype(BF16)

    kmean = ks_ref[0] * (1.0 / MOBA_BLOCK)
    kmean = jnp.concatenate([kmean, jnp.zeros((LANES - nb, HEAD_DIM), F32)], axis=0)
    gate = _nt_dot(q.astype(BF16), kmean.astype(BF16))
    lane = lax.broadcasted_iota(jnp.int32, (tq, LANES), 1)
    lane_f = lane.astype(F32)
    gate = jnp.where(lane < i, gate, -jnp.inf)
    sel = jnp.zeros((tq, LANES), jnp.bool_)
    for _ in range(MOBA_TOP_K):
        mx = jnp.max(gate, axis=1, keepdims=True)
        first = jnp.min(jnp.where(gate == mx, lane_f, float(LANES)), axis=1, keepdims=True)
        hit = lane_f == first
        sel = sel | (hit & (mx > -jnp.inf))
        gate = jnp.where(hit, -jnp.inf, gate)
    qaug[:, 0:HEAD_DIM] = qs
    qaug[:, HEAD_DIM:] = jnp.where(sel, 0.0, MASKED).astype(BF16)

    own = pl.ds(pl.multiple_of(i * MOBA_BLOCK, MOBA_BLOCK), MOBA_BLOCK)
    s = _nt_dot(qs, kaug[own, 0:HEAD_DIM])
    r_id = lax.broadcasted_iota(jnp.int32, s.shape, 0)
    c_id = lax.broadcasted_iota(jnp.int32, s.shape, 1)
    s = jnp.where(c_id <= r_id, s, MASKED)
    m0 = jnp.max(s, axis=1, keepdims=True)
    p = jnp.exp(s - m0)
    m_sc[...] = jnp.broadcast_to(m0, m_sc.shape)
    l_sc[...] = jnp.broadcast_to(jnp.sum(p, axis=1, keepdims=True), l_sc.shape)
    acc_sc[...] = jnp.dot(p.astype(BF16), vb[own, :], preferred_element_type=F32)

    def step(j, carry):
        rows = pl.ds(pl.multiple_of(j * MOBA_BLOCK, MOBA_BLOCK), MOBA_BLOCK)
        sj = _nt_dot(qaug[...], kaug[rows, :])
        m_prev = m_sc[...]
        m_new = jnp.maximum(m_prev, jnp.max(sj, axis=1, keepdims=True))
        alpha = jnp.exp(m_prev - m_new)
        pj = jnp.exp(sj - m_new[:, 0:1])
        l_sc[...] = alpha * l_sc[...] + jnp.sum(pj, axis=1, keepdims=True)
        acc_sc[...] = alpha * acc_sc[...] + jnp.dot(pj.astype(BF16), vb[rows, :], preferred_element_type=F32)
        m_sc[...] = m_new
        return carry
    lax.fori_loop(0, i, step, 0)

    o_ref[...] = (acc_sc[...] / l_sc[...]).astype(o_ref.dtype)


def _attn_prompt(q, k, v, ksum, *, batch, seq):
    rows, attn_dim = q.shape
    heads = attn_dim // HEAD_DIM
    tq = MOBA_BLOCK
    nq = seq // tq
    nb = seq // MOBA_BLOCK
    assert nb <= LANES
    q_spec = pl.BlockSpec((tq, HEAD_DIM), lambda b, h, i: (b * nq + i, h))
    kv_spec = pl.BlockSpec((seq, HEAD_DIM), lambda b, h, i: (b, h))
    return pl.pallas_call(
        _attn_prompt_kernel,
        grid=(batch, heads, nq),
        in_specs=[q_spec, kv_spec, kv_spec,
                  pl.BlockSpec((1, nb, HEAD_DIM), lambda b, h, i: (b, 0, h))],
        out_specs=q_spec,
        out_shape=jax.ShapeDtypeStruct((rows, attn_dim), BF16),
        scratch_shapes=[
            pltpu.VMEM((seq, 2 * HEAD_DIM), BF16),
            pltpu.VMEM((tq, 2 * HEAD_DIM), BF16),
            pltpu.VMEM((tq, HEAD_DIM), F32),
            pltpu.VMEM((tq, HEAD_DIM), F32),
            pltpu.VMEM((tq, HEAD_DIM), F32),
        ],
        compiler_params=_params("arbitrary", "arbitrary", "arbitrary"),
        name="attn_prompt",
    )(q, k, v, ksum.reshape(batch, nb, attn_dim))


def _cache_scan_kernel(pt_ref, q_ref, *rest, pps, heads, page):
    page_refs = rest[:pps]
    l_ref, ti_ref, ksum_sc = rest[pps:]
    p = pl.program_id(1)
    t = q_ref.shape[1]
    ppb = MOBA_BLOCK // page
    bps = pps // ppb

    qs = q_ref[0] * SOFTMAX_SCALE
    q_hi = qs.astype(BF16).astype(F32)
    q_lo = (qs - q_hi).astype(BF16).astype(F32)
    q16 = jnp.concatenate([q_hi, q_lo, jnp.zeros((16 - 2 * t, qs.shape[1]), F32)], axis=0).astype(BF16)

    sums = []
    for u in range(pps):
        sums.append(jnp.sum(page_refs[u][...].reshape(page, heads, HEAD_DIM), axis=0))
        for h in range(heads):
            k_h = page_refs[u][pl.ds(h, page, stride=heads), :].astype(BF16)
            lt = _nt_dot(q16[:, h * HEAD_DIM:(h + 1) * HEAD_DIM], k_h)
            l_ref[0, h, :, u * page:(u + 1) * page] = lt[0:t] + lt[t:2 * t]
    for b in range(bps):
        blk_sum = functools.reduce(lambda x, y: x + y, sums[b * ppb:(b + 1) * ppb])
        ksum_sc[pl.ds(pl.multiple_of((p * bps + b) * heads, heads), heads), :] = blk_sum

    @pl.when(p == pl.num_programs(1) - 1)
    def _():
        nb = ksum_sc.shape[0] // heads
        lane = lax.broadcasted_iota(jnp.int32, (16, LANES), 1)
        lane_f = lane.astype(F32)
        for h in range(heads):
            kmean = ksum_sc[pl.ds(h, nb, stride=heads), :] * (1.0 / MOBA_BLOCK)
            km = jnp.concatenate([kmean, jnp.zeros((LANES - nb, HEAD_DIM), F32)], axis=0)
            km_hi, km_lo = _split_bf16(km)
            q_h = q16[:, h * HEAD_DIM:(h + 1) * HEAD_DIM]
            r1 = _nt_dot(q_h, km_hi)
            r2 = _nt_dot(q_h, km_lo)
            gate = r1 + pltpu.roll(r1, 16 - t, 0) + r2
            gate = jnp.where(lane < nb, gate, -jnp.inf)
            out = jnp.zeros((16, LANES), jnp.int32)
            for kk in range(MOBA_TOP_K):
                mx = jnp.max(gate, axis=1, keepdims=True)
                first = jnp.min(jnp.where(gate == mx, lane_f, float(LANES)), axis=1, keepdims=True)
                idx = jnp.where(mx > -jnp.inf, first, -1.0).astype(jnp.int32)
                out = jnp.where(lane == kk, idx, out)
                gate = jnp.where(lane_f == first, -jnp.inf, gate)
            ti_ref[0, h] = out[0:8]


def _cache_scan(page_table, q_s, cache, *, pps, page):
    nseq, n_pages = page_table.shape
    _, t, attn_dim = q_s.shape
    heads = attn_dim // HEAD_DIM
    steps = n_pages // pps
    past = n_pages * page
    nb = past // MOBA_BLOCK
    assert n_pages % pps == 0 and (pps * page) % MOBA_BLOCK == 0 and MOBA_BLOCK % page == 0
    assert nb <= LANES and 2 * t <= 8 and heads % 8 == 0
    page_specs = [
        pl.BlockSpec((page * heads, HEAD_DIM), functools.partial(lambda s, p, pt, u: (pt[s, p * pps + u], 0), u=u))
        for u in range(pps)
    ]
    grid_spec = pltpu.PrefetchScalarGridSpec(
        num_scalar_prefetch=1,
        grid=(nseq, steps),
        in_specs=[pl.BlockSpec((1, t, attn_dim), lambda s, p, pt: (s, 0, 0))] + page_specs,
        out_specs=[
            pl.BlockSpec((1, heads, t, pps * page), lambda s, p, pt: (s, 0, 0, p)),
            pl.BlockSpec((1, heads, 8, LANES), lambda s, p, pt: (s, 0, 0, 0)),
        ],
        scratch_shapes=[pltpu.VMEM((nb * heads, HEAD_DIM), F32)],
    )
    return pl.pallas_call(
        functools.partial(_cache_scan_kernel, pps=pps, heads=heads, page=page),
        grid_spec=grid_spec,
        out_shape=[
            jax.ShapeDtypeStruct((nseq, heads, t, past), F32),
            jax.ShapeDtypeStruct((nseq, heads, 8, LANES), jnp.int32),
        ],
        compiler_params=_params("arbitrary", "arbitrary"),
        name="cache_scan",
    )(page_table, q_s, *([cache] * pps))


def _attn_sample_kernel(pt_ref, ti_ref, q_ref, kn_ref, vn_ref, l_hbm, cv_hbm, o_ref, vbuf, lbuf, sem, *, heads):
    s = pl.program_id(0)
    h = pl.program_id(1)
    n = s * heads + h
    total = pl.num_programs(0) * heads
    t = q_ref.shape[1]
    nsel = t * MOBA_TOP_K
    page = cv_hbm.shape[1]
    ppb = MOBA_BLOCK // page

    def copies(s_, h_, slot_):
        out = []
        for c in range(nsel):
            blk = jnp.maximum(ti_ref[s_, h_ * nsel + c], 0)
            out.append(pltpu.make_async_copy(
                l_hbm.at[s_, h_, :, pl.ds(pl.multiple_of(blk * MOBA_BLOCK, MOBA_BLOCK), MOBA_BLOCK)],
                lbuf.at[slot_, c], sem.at[slot_]))
            for e in range(ppb):
                pg = pt_ref[s_, blk * ppb + e]
                out.append(pltpu.make_async_copy(
                    cv_hbm.at[pg, :, h_, :], vbuf.at[slot_, c * ppb + e], sem.at[slot_]))
        return out

    slot = n % 2

    @pl.when(n == 0)
    def _():
        for cp in copies(s, h, slot):
            cp.start()

    @pl.when(n + 1 < total)
    def _():
        h_next = (h + 1) % heads
        s_next = s + (h + 1) // heads
        for cp in copies(s_next, h_next, 1 - slot):
            cp.start()

    for cp in copies(s, h, slot):
        cp.wait()

    qs = q_ref[0] * SOFTMAX_SCALE
    kn = kn_ref[0]
    vn = vn_ref[0]
    width = nsel * MOBA_BLOCK
    segs = []
    for c in range(nsel):
        ok = ti_ref[s, h * nsel + c] >= 0
        segs.append(jnp.where(ok, lbuf[slot, c], MASKED))
    lm = jnp.concatenate(segs, axis=1)
    row = lax.broadcasted_iota(jnp.int32, (t, width), 0)
    col = lax.broadcasted_iota(jnp.int32, (t, width), 1)
    seg_w = MOBA_TOP_K * MOBA_BLOCK
    lm = jnp.where((col >= row * seg_w) & (col < (row + 1) * seg_w), lm, MASKED)
    row_o = lax.broadcasted_iota(jnp.int32, (t, LANES), 0)
    col_o = lax.broadcasted_iota(jnp.int32, (t, LANES), 1)
    own = jnp.full((t, LANES), MASKED, F32)
    for j in range(t):
        dj = jnp.sum(qs * kn[j:j + 1, :], axis=1, keepdims=True)
        own = jnp.where((col_o == j) & (j <= row_o), dj, own)
    full = jnp.concatenate([lm, own], axis=1)
    m = jnp.max(full, axis=1, keepdims=True)
    p = jnp.exp(full - m)
    l = jnp.sum(p, axis=1, keepdims=True)
    p_pad = jnp.concatenate([p[:, :width], jnp.zeros((8 - t, width), F32)], axis=0).astype(BF16)
    vsel = vbuf[slot].reshape(width, HEAD_DIM).astype(BF16)
    pv = jnp.dot(p_pad, vsel, preferred_element_type=F32)[0:t]
    for j in range(t):
        pv += p[:, width + j:width + j + 1] * vn[j:j + 1, :]
    o_ref[0] = pv / l


def _attn_sample(page_table, top_idx, q_s, k_s, v_s, logits, cache_v):
    nseq, t, attn_dim = q_s.shape
    heads = attn_dim // HEAD_DIM
    page = cache_v.shape[1]
    nsel = t * MOBA_TOP_K
    ppb = MOBA_BLOCK // page
    new_spec = pl.BlockSpec((1, t, HEAD_DIM), lambda s, h, pt, ti: (s, 0, h))
    grid_spec = pltpu.PrefetchScalarGridSpec(
        num_scalar_prefetch=2,
        grid=(nseq, heads),
        in_specs=[new_spec, new_spec, new_spec,
                  pl.BlockSpec(memory_space=pl.ANY), pl.BlockSpec(memory_space=pl.ANY)],
        out_specs=new_spec,
        scratch_shapes=[
            pltpu.VMEM((2, nsel * ppb, page, HEAD_DIM), F32),
            pltpu.VMEM((2, nsel, t, MOBA_BLOCK), F32),
            pltpu.SemaphoreType.DMA((2,)),
        ],
    )
    return pl.pallas_call(
        functools.partial(_attn_sample_kernel, heads=heads),
        grid_spec=grid_spec,
        out_shape=jax.ShapeDtypeStruct((nseq, t, attn_dim), F32),
        compiler_params=_params("arbitrary", "arbitrary"),
        name="attn_sample",
    )(page_table, top_idx, q_s, k_s, v_s, logits, cache_v)


def _post_attn_kernel(x_ref, o_ref, mod_ref, g_ref, wo_ref, wup_ref, wdown_ref, y_ref):
    d = x_ref.shape[1]
    m = mod_ref[0]
    a = jnp.dot(o_ref[...].astype(BF16), wo_ref[...], preferred_element_type=F32)
    x1 = x_ref[...] + m[:, 2 * d:3 * d] * a
    h = _norm_mod(x1, g_ref[...], m[:, 3 * d:4 * d], m[:, 4 * d:5 * d])
    y_ref[...] = x1 + m[:, 5 * d:6 * d] * _mlp(h, wup_ref, wdown_ref)


def _post_attn(x, o, mod, g, wo, wup, wdown, *, tm, tiles_per_seq):
    rows, d = x.shape
    row_spec = pl.BlockSpec((tm, d), lambda i: (i, 0))
    return pl.pallas_call(
        _post_attn_kernel,
        grid=(rows // tm,),
        in_specs=[
            row_spec,
            pl.BlockSpec((tm, o.shape[1]), lambda i: (i, 0)),
            pl.BlockSpec((1, mod.shape[1], mod.shape[2]), lambda i: (i // tiles_per_seq, 0, 0)),
            _const_spec((1, d)),
            _const_spec(wo.shape), _const_spec(wup.shape), _const_spec(wdown.shape),
        ],
        out_specs=row_spec,
        out_shape=jax.ShapeDtypeStruct((rows, d), F32),
        compiler_params=_params("arbitrary"),
        name="post_attn",
    )(x, o, mod, g, wo, wup, wdown)


def _conv_layer_kernel(x_ref, mod_ref, g1_ref, g2_ref, gf_ref, wbcx_ref, wconv_ref, wout_ref, wup_ref, wdown_ref,
                       *rest, tiles_per_seq, period, tail):
    if period is None:
        y_ref, ut_ref, ext = rest
    else:
        e1_ref, e2_ref, y_ref, ut_ref, ext = rest
    i = pl.program_id(0)
    tm, d = x_ref.shape
    halo = 8
    m = mod_ref[0]
    x = x_ref[...]
    h = _norm_mod(x, g1_ref[...], m[:, 0:d], m[:, d:2 * d]).astype(BF16)
    bg = jnp.dot(h, wbcx_ref[:, 0:d], preferred_element_type=F32)
    cg = jnp.dot(h, wbcx_ref[:, d:2 * d], preferred_element_type=F32)
    xin = jnp.dot(h, wbcx_ref[:, 2 * d:3 * d], preferred_element_type=F32)
    u = cg * xin

    @pl.when(i % tiles_per_seq == 0)
    def _():
        ext[0:halo, :] = jnp.zeros((halo, d), F32)

    ext[halo:halo + tm, :] = u
    prev1 = ext[halo - 1:halo - 1 + tm, :]
    prev2 = ext[halo - 2:halo - 2 + tm, :]
    if period is not None:
        tpos = lax.broadcasted_iota(jnp.int32, (tm, d), 0) % period
        prev1 = jnp.where(tpos < 1, e1_ref[...], prev1)
        prev2 = jnp.where(tpos < 2, e2_ref[...], prev2)
    w = wconv_ref[...]
    conv = w[0:1, :] * prev2 + w[1:2, :] * prev1 + w[2:3, :] * u
    ext[0:halo, :] = ext[tm:tm + halo, :]
    ut_ref[0] = u[tm - tail:, :]

    y = jnp.dot((bg * conv).astype(BF16), wout_ref[...], preferred_element_type=F32)
    x1 = x + m[:, 2 * d:3 * d] * y
    h2 = _norm_mod(x1, g2_ref[...], m[:, 3 * d:4 * d], m[:, 4 * d:5 * d])
    x2 = x1 + m[:, 5 * d:6 * d] * _mlp(h2, wup_ref, wdown_ref)
    ms = jnp.mean(x2 * x2, axis=-1, keepdims=True)
    y_ref[...] = x2 * lax.rsqrt(ms + RMS_EPS) * gf_ref[...]


def _conv_layer(x, mod, g1, g2, gf, wbcx, wconv, wout, wup, wdown, state_rows, *, tm, tiles_per_seq, period, tail):
    rows, d = x.shape
    n_tiles = rows // tm
    row_spec = pl.BlockSpec((tm, d), lambda i: (i, 0))
    in_specs = [
        row_spec,
        pl.BlockSpec((1, mod.shape[1], mod.shape[2]), lambda i: (i // tiles_per_seq, 0, 0)),
        _const_spec((1, d)), _const_spec((1, d)), _const_spec((1, d)),
        _const_spec(wbcx.shape), _const_spec(wconv.shape), _const_spec(wout.shape),
        _const_spec(wup.shape), _const_spec(wdown.shape),
    ]
    args = [x, mod, g1, g2, gf, wbcx, wconv, wout, wup, wdown]
    if period is not None:
        in_specs += [row_spec, row_spec]
        args += list(state_rows)
    return pl.pallas_call(
        functools.partial(_conv_layer_kernel, tiles_per_seq=tiles_per_seq, period=period, tail=tail),
        grid=(n_tiles,),
        in_specs=in_specs,
        out_specs=[row_spec, pl.BlockSpec((1, tail, d), lambda i: (i, 0, 0))],
        out_shape=[jax.ShapeDtypeStruct((rows, d), F32), jax.ShapeDtypeStruct((n_tiles, tail, d), F32)],
        scratch_shapes=[pltpu.VMEM((tm + 8, d), F32)],
        compiler_params=_params("arbitrary"),
        name="conv_layer",
    )(*args)


def kernel(x_prompt, x_sample, cache_k, cache_v, state_conv, page_table, c_prompt, c_sample, w_ada, b_ada, g_norm,
           w_qkv, w_o, w_bcx, w_conv, w_out, w_up, w_down, g_final):
    batch, seq, d = x_prompt.shape
    nseq, t, _ = x_sample.shape
    depth = w_ada.shape[0]
    n_layers_attn, pool, page, heads, head_dim = cache_k.shape
    attn_dim = heads * head_dim
    n_pages = page_table.shape[1]
    past = n_pages * page
    kw = state_conv.shape[2] + 1
    assert depth == 2 and n_layers_attn == 1 and state_conv.shape[0] == 1, "layer 0 attention, layer 1 convolution"
    assert head_dim == HEAD_DIM and kw == 3 and t >= kw - 1
    assert seq % ROW_TILE == 0 and ROW_TILE % MOBA_BLOCK == 0 and past % MOBA_BLOCK == 0 and t <= MOBA_BLOCK
    rows_p, rows_s = batch * seq, nseq * t
    assert rows_s % 8 == 0

    n_c = batch + nseq
    pad = -n_c % 8
    c_all = jnp.concatenate([c_prompt, c_sample, jnp.zeros((pad, d), F32)], axis=0)
    mod = _ada(c_all, w_ada, b_ada)
    mod_p = mod[:, :batch, None, :]
    mod_s = jnp.repeat(mod[:, batch:n_c], t, axis=1)[:, None]

    bf = lambda a: a.astype(BF16)
    wqkv, wo, wbcx, wout = bf(w_qkv[0]), bf(w_o[0]), bf(w_bcx[0]), bf(w_out[0])
    wup, wdown = bf(w_up), bf(w_down)
    g = g_norm.reshape(depth, 2, 1, d)
    gf = g_final.reshape(1, d)

    xp = x_prompt.reshape(rows_p, d)
    xs = x_sample.reshape(rows_s, d)
    tps = seq // ROW_TILE
    tab_p = _rope_tables(np.arange(seq))
    tab_s = _rope_tables(past + (np.arange(rows_s) % t))

    q_p, k_p, v_p, kb_p, vb_p, ksum = _qkv(xp, mod_p[0], g[0, 0], wqkv, tab_p, tm=ROW_TILE, tiles_per_seq=tps,
                                           with_ksum=True)
    q_s, k_s, v_s, kb_s, vb_s = _qkv(xs, mod_s[0], g[0, 0], wqkv, tab_s, tm=rows_s, tiles_per_seq=1,
                                     with_ksum=False)
    o_p = _attn_prompt(q_p, kb_p, vb_p, ksum, batch=batch, seq=seq)

    q_s3, kn_s3, vn_s3 = (a.astype(F32).reshape(nseq, t, attn_dim) for a in (q_s, kb_s, vb_s))
    logits, top = _cache_scan(page_table, q_s3, cache_k.reshape(pool * page * heads, head_dim),
                              pps=PAGES_PER_STEP, page=page)
    top_idx = top[:, :, :t, :MOBA_TOP_K].reshape(nseq, heads * t * MOBA_TOP_K)
    o_s = _attn_sample(page_table, top_idx, q_s3, kn_s3, vn_s3, logits,
                       cache_v.reshape(pool, page, heads, head_dim))

    x1_p = _post_attn(xp, o_p, mod_p[0], g[0, 1], wo, wup[0], wdown[0], tm=ROW_TILE, tiles_per_seq=tps)
    x1_s = _post_attn(xs, o_s.reshape(rows_s, attn_dim), mod_s[0], g[0, 1], wo, wup[0], wdown[0],
                      tm=rows_s, tiles_per_seq=1)

    y_p, ut_p = _conv_layer(x1_p, mod_p[1], g[1, 0], g[1, 1], gf, wbcx, w_conv[0], wout, wup[1], wdown[1], None,
                            tm=ROW_TILE, tiles_per_seq=tps, period=None, tail=8)
    st = state_conv[0]
    zeros = jnp.zeros((nseq, t - 1, d), F32)
    e1 = jnp.concatenate([st[:, 1:2], zeros], axis=1).reshape(rows_s, d)
    e2 = jnp.concatenate([st[:, 0:2], zeros[:, 1:]], axis=1).reshape(rows_s, d)
    y_s, ut_s = _conv_layer(x1_s, mod_s[1], g[1, 0], g[1, 1], gf, wbcx, w_conv[0], wout, wup[1], wdown[1], (e1, e2),
                            tm=rows_s, tiles_per_seq=1, period=t, tail=rows_s)

    conv_p = ut_p.reshape(batch, tps, 8, d)[:, -1, 8 - (kw - 1):, :]
    conv_s = ut_s.reshape(nseq, t, d)[:, t - (kw - 1):, :]
    return (
        y_p.reshape(batch, seq, d),
        y_s.reshape(nseq, t, d),
        k_p.reshape(1, batch, seq, heads, head_dim),
        v_p.reshape(1, batch, seq, heads, head_dim),
        k_s.reshape(1, nseq, t, heads, head_dim),
        v_s.reshape(1, nseq, t, heads, head_dim),
        conv_p[None],
        conv_s[None],
    )
```

```python
import functools

import numpy as np
import jax
import jax.numpy as jnp
from jax import lax
from jax.experimental import pallas as pl
from jax.experimental.pallas import tpu as pltpu

F32 = jnp.float32
BF16 = jnp.bfloat16

HEAD_DIM = 128
MOBA_BLOCK = 256
MOBA_TOP_K = 3
ROPE_DIM = HEAD_DIM // 4
ROPE_THETA = 500000.0
RMS_EPS = 1e-6
SOFTMAX_SCALE = HEAD_DIM ** -0.5
MASKED = -1e30

LANES = 128
VMEM_LIMIT = 56 * 1024 * 1024
ROW_TILE = 256
PAGES_PER_STEP = 16
ADA_COL_TILE = 1536
GATHER_SLOTS = 4
ATTN_HEADS_PER_STEP = 2
ATTN_CHUNK_BLOCKS = 4

NT_DIMS = (((1,), (1,)), ((), ()))


def _nt_dot(a, b):
    return lax.dot_general(a, b, NT_DIMS, preferred_element_type=F32)


def _split_bf16(x):
    hi = x.astype(BF16)
    lo = (x - hi.astype(F32)).astype(BF16)
    return hi, lo


def _const_spec(shape):
    n = len(shape)
    return pl.BlockSpec(shape, lambda *_: (0,) * n, pipeline_mode=pl.Buffered(1))


def _params(*sem):
    return pltpu.CompilerParams(dimension_semantics=sem, vmem_limit_bytes=VMEM_LIMIT)


def _ada_kernel(c_ref, w_ref, b_ref, o_ref):
    c_hi, c_lo = _split_bf16(c_ref[...])
    w_hi, w_lo = _split_bf16(w_ref[0])
    acc = jnp.dot(c_hi, w_hi, preferred_element_type=F32)
    acc += jnp.dot(c_lo, w_hi, preferred_element_type=F32)
    acc += jnp.dot(c_hi, w_lo, preferred_element_type=F32)
    o_ref[0] = acc + b_ref[0]


def _ada(c_all, w_ada, b_ada):
    depth, d, n6 = w_ada.shape
    rows = c_all.shape[0]
    tn = ADA_COL_TILE
    return pl.pallas_call(
        _ada_kernel,
        grid=(depth, n6 // tn),
        in_specs=[
            pl.BlockSpec((rows, d), lambda l, n: (0, 0)),
            pl.BlockSpec((1, d, tn), lambda l, n: (l, 0, n)),
            pl.BlockSpec((1, 1, tn), lambda l, n: (l, 0, n)),
        ],
        out_specs=pl.BlockSpec((1, rows, tn), lambda l, n: (l, 0, n)),
        out_shape=jax.ShapeDtypeStruct((depth, rows, n6), F32),
        compiler_params=_params("arbitrary", "arbitrary"),
        name="ada",
    )(c_all, w_ada, b_ada.reshape(depth, 1, n6))


def _norm_mod(x, g, shift, scale):
    ms = jnp.mean(x * x, axis=-1, keepdims=True)
    y = x * lax.rsqrt(ms + RMS_EPS) * g
    return y * (1.0 + scale) + shift


def _mlp(h, wup_ref, wdown_ref):
    d, dff = wup_ref.shape
    hb = h.astype(BF16)
    acc = jnp.zeros((h.shape[0], d), F32)
    for c in range(0, dff, d):
        u = jnp.dot(hb, wup_ref[:, c:c + d], preferred_element_type=F32)
        u = jnp.square(jnp.maximum(u, 0.0)).astype(BF16)
        acc += jnp.dot(u, wdown_ref[c:c + d, :], preferred_element_type=F32)
    return acc


def _rope_tables(pos):
    half = ROPE_DIM // 2
    inv = np.float32(ROPE_THETA) ** (-np.arange(0, ROPE_DIM, 2, dtype=np.float32) / np.float32(ROPE_DIM))
    ang = (pos.astype(np.float32)[:, None] * inv.astype(np.float32)[None, :]).astype(np.float32)
    cos = np.cos(ang.astype(np.float64))
    sin = np.sin(ang.astype(np.float64))
    n = pos.shape[0]
    c = np.ones((n, HEAD_DIM), np.float64)
    s1 = np.zeros((n, HEAD_DIM), np.float64)
    s2 = np.zeros((n, HEAD_DIM), np.float64)
    c[:, :half] = cos
    c[:, half:ROPE_DIM] = cos
    s1[:, :half] = -sin
    s2[:, half:ROPE_DIM] = sin
    return tuple(jnp.asarray(t.astype(np.float32)) for t in (c, s1, s2))


def _qkv_kernel(x_ref, mod_ref, g_ref, w_ref, cos_ref, s1_ref, s2_ref, q_ref, k5_ref, v5_ref, kb_ref, vb_ref,
                *ks_ref, attn_dim):
    d = x_ref.shape[1]
    m = mod_ref[0]
    h = _norm_mod(x_ref[...], g_ref[...], m[:, 0:d], m[:, d:2 * d]).astype(BF16)
    cos, s1, s2 = cos_ref[...], s1_ref[...], s2_ref[...]
    tm = h.shape[0]
    heads = attn_dim // HEAD_DIM
    cw = 2 * HEAD_DIM
    for c in range(0, 3 * attn_dim, cw):
        r = jnp.dot(h, w_ref[:, c:c + cw], preferred_element_type=F32)
        if c < 2 * attn_dim:
            rot = []
            for u in range(0, cw, HEAD_DIM):
                z = r[:, u:u + HEAD_DIM]
                z = (z * cos + pltpu.roll(z, HEAD_DIM - ROPE_DIM // 2, 1) * s1
                     + pltpu.roll(z, ROPE_DIM // 2, 1) * s2)
                rot.append(z)
            r = jnp.concatenate(rot, axis=1)
        if c < attn_dim:
            q_ref[:, c:c + cw] = r
            continue
        is_k = c < 2 * attn_dim
        c0 = c - (attn_dim if is_k else 2 * attn_dim)
        (kb_ref if is_k else vb_ref)[:, c0:c0 + cw] = r.astype(BF16)
        for u in range(0, cw, HEAD_DIM):
            head = (c0 + u) // HEAD_DIM
            (k5_ref if is_k else v5_ref)[pl.ds(head, tm, stride=heads), :] = r[:, u:u + HEAD_DIM]
        if is_k and ks_ref:
            nblk = tm // MOBA_BLOCK
            ks_ref[0][0, :, c0:c0 + cw] = jnp.sum(r.reshape(nblk, MOBA_BLOCK, cw), axis=1)


def _qkv(x, mod, g, w, tables, *, tm, tiles_per_seq, with_ksum):
    rows, d = x.shape
    attn_dim = w.shape[1] // 3
    heads = attn_dim // HEAD_DIM
    n_tiles = rows // tm
    mod_rows = mod.shape[1]
    row_spec = pl.BlockSpec((tm, attn_dim), lambda i: (i, 0))
    tok_head_spec = pl.BlockSpec((tm * heads, HEAD_DIM), lambda i: (i, 0))
    tab_spec = pl.BlockSpec((tm, HEAD_DIM), lambda i: (i % tiles_per_seq, 0))
    out_specs = [row_spec, tok_head_spec, tok_head_spec, row_spec, row_spec]
    out_shape = ([jax.ShapeDtypeStruct((rows, attn_dim), F32)]
                 + [jax.ShapeDtypeStruct((rows * heads, HEAD_DIM), F32)] * 2
                 + [jax.ShapeDtypeStruct((rows, attn_dim), BF16)] * 2)
    if with_ksum:
        nblk = tm // MOBA_BLOCK
        out_specs.append(pl.BlockSpec((1, nblk, attn_dim), lambda i: (i, 0, 0)))
        out_shape.append(jax.ShapeDtypeStruct((n_tiles, nblk, attn_dim), F32))
    return pl.pallas_call(
        functools.partial(_qkv_kernel, attn_dim=attn_dim),
        grid=(n_tiles,),
        in_specs=[
            pl.BlockSpec((tm, d), lambda i: (i, 0)),
            pl.BlockSpec((1, mod_rows, mod.shape[2]), lambda i: (i // tiles_per_seq, 0, 0)),
            _const_spec((1, d)),
            _const_spec(w.shape),
            tab_spec, tab_spec, tab_spec,
        ],
        out_specs=out_specs,
        out_shape=out_shape,
        compiler_params=_params("arbitrary"),
        name="qkv",
    )(x, mod, g, w, *tables)


def _attn_prompt_kernel(q_ref, k_ref, v_ref, ks_ref, o_ref, kaug, vaug, eye, s_sc, *, hps, chunk_blocks):
    i = pl.program_id(2)
    tq = q_ref.shape[0]
    seq = k_ref.shape[0]
    nb = seq // MOBA_BLOCK
    ck = chunk_blocks * MOBA_BLOCK
    n_chunk_max = nb // chunk_blocks

    @pl.when(i == 0)
    def _():
        r_id = lax.broadcasted_iota(jnp.int32, (tq, tq), 0)
        c_id = lax.broadcasted_iota(jnp.int32, (tq, tq), 1)
        eye[...] = jnp.where(r_id == c_id, 1.0, 0.0).astype(BF16)
        lane = lax.broadcasted_iota(jnp.int32, (MOBA_BLOCK, LANES), 1)
        ones_col = jnp.where(lane == 0, 1.0, 0.0).astype(BF16)
        for hh in range(hps):
            cols = slice(hh * HEAD_DIM, (hh + 1) * HEAD_DIM)

            def fill(j, carry):
                rows = pl.ds(pl.multiple_of(j * MOBA_BLOCK, MOBA_BLOCK), MOBA_BLOCK)
                kaug[hh, rows, 0:HEAD_DIM] = k_ref[rows, cols]
                kaug[hh, rows, HEAD_DIM:] = jnp.where(lane == j, 1.0, 0.0).astype(BF16)
                vaug[hh, rows, 0:HEAD_DIM] = v_ref[rows, cols]
                vaug[hh, rows, HEAD_DIM:] = ones_col
                return carry
            lax.fori_loop(0, nb, fill, 0)

    def head(hh, n_chunks):
        cols = slice(hh * HEAD_DIM, (hh + 1) * HEAD_DIM)
        q = q_ref[:, cols]
        qs = (q * (SOFTMAX_SCALE * np.log2(np.e))).astype(BF16)

        kmean = (ks_ref[0][:, cols] * (1.0 / MOBA_BLOCK)).astype(BF16)
        gate = _nt_dot(kmean, q.astype(BF16))
        blk = lax.broadcasted_iota(jnp.int32, (nb, tq), 0)
        blk_f = blk.astype(F32)
        gate = jnp.where(blk < i, gate, -jnp.inf)
        sel = jnp.zeros((nb, tq), F32)
        for _ in range(MOBA_TOP_K):
            mx = jnp.max(gate, axis=0, keepdims=True)
            first = jnp.min(jnp.where(gate == mx, blk_f, float(nb)), axis=0, keepdims=True)
            hit = blk_f == first
            sel = jnp.where(hit & (mx > -jnp.inf), 1.0, sel)
            gate = jnp.where(hit, -jnp.inf, gate)
        sel_pad = jnp.concatenate([sel, jnp.zeros((LANES - nb, tq), F32)], axis=0).astype(BF16)
        sel_q = _nt_dot(eye[...], sel_pad)
        bias = jnp.where(sel_q > 0.5, 0.0, MASKED).astype(BF16)
        qa = jnp.concatenate([qs, bias], axis=1)

        own = pl.ds(pl.multiple_of(i * MOBA_BLOCK, MOBA_BLOCK), MOBA_BLOCK)
        s_own = _nt_dot(qs, kaug[hh, own, 0:HEAD_DIM])
        r_id = lax.broadcasted_iota(jnp.int32, s_own.shape, 0)
        c_id = lax.broadcasted_iota(jnp.int32, s_own.shape, 1)
        s_own = jnp.where(c_id <= r_id, s_own, MASKED)
        run = jnp.maximum(s_own[:, 0:LANES], s_own[:, LANES:])
        for c in range(n_chunks):
            s_c = _nt_dot(qa, kaug[hh, c * ck:(c + 1) * ck, :])
            s_sc[hh, :, c * ck:(c + 1) * ck] = s_c
            for u in range(0, ck, LANES):
                run = jnp.maximum(run, s_c[:, u:u + LANES])
        m = jnp.max(run, axis=1, keepdims=True)
        acc = jnp.dot(jnp.exp2(s_own - m).astype(BF16), vaug[hh, own, :], preferred_element_type=F32)
        for c in range(n_chunks):
            p_c = jnp.exp2(s_sc[hh, :, c * ck:(c + 1) * ck] - m).astype(BF16)
            acc += jnp.dot(p_c, vaug[hh, c * ck:(c + 1) * ck, :], preferred_element_type=F32)
        o_ref[:, cols] = (acc[:, 0:HEAD_DIM] / acc[:, HEAD_DIM:HEAD_DIM + 1]).astype(o_ref.dtype)

    n_needed = (i + chunk_blocks - 1) // chunk_blocks
    for n in range(n_chunk_max + 1):
        @pl.when(n_needed == n)
        def _(n=n):
            for hh in range(hps):
                head(hh, n)


def _attn_prompt(q, k, v, ksum, *, batch, seq):
    rows, attn_dim = q.shape
    heads = attn_dim // HEAD_DIM
    tq = MOBA_BLOCK
    nq = seq // tq
    nb = seq // MOBA_BLOCK
    hps = ATTN_HEADS_PER_STEP
    assert nb <= LANES and nb % ATTN_CHUNK_BLOCKS == 0 and heads % hps == 0
    q_spec = pl.BlockSpec((tq, hps * HEAD_DIM), lambda b, h, i: (b * nq + i, h))
    kv_spec = pl.BlockSpec((seq, hps * HEAD_DIM), lambda b, h, i: (b, h))
    return pl.pallas_call(
        functools.partial(_attn_prompt_kernel, hps=hps, chunk_blocks=ATTN_CHUNK_BLOCKS),
        grid=(batch, heads // hps, nq),
        in_specs=[q_spec, kv_spec, kv_spec,
                  pl.BlockSpec((1, nb, hps * HEAD_DIM), lambda b, h, i: (b, 0, h))],
        out_specs=q_spec,
        out_shape=jax.ShapeDtypeStruct((rows, attn_dim), BF16),
        scratch_shapes=[
            pltpu.VMEM((hps, seq, 2 * HEAD_DIM), BF16),
            pltpu.VMEM((hps, seq, 2 * HEAD_DIM), BF16),
            pltpu.VMEM((tq, tq), BF16),
            pltpu.VMEM((hps, tq, seq), F32),
        ],
        compiler_params=_params("arbitrary", "arbitrary", "arbitrary"),
        name="attn_prompt",
    )(q, k, v, ksum.reshape(batch, nb, attn_dim))


def _cache_scan_kernel(pt_ref, q_ref, *rest, pps, heads, page):
    page_refs = rest[:pps]
    l_ref, ti_ref, ksum_sc = rest[pps:]
    p = pl.program_id(1)
    t = q_ref.shape[1]
    ppb = MOBA_BLOCK // page
    bps = pps // ppb

    qs = q_ref[0] * SOFTMAX_SCALE
    q_hi = qs.astype(BF16).astype(F32)
    q_lo = (qs - q_hi).astype(BF16).astype(F32)
    q16 = jnp.concatenate([q_hi, q_lo, jnp.zeros((16 - 2 * t, qs.shape[1]), F32)], axis=0).astype(BF16)

    for h in range(heads):
        keys, sums = [], []
        for u in range(pps):
            k_h = page_refs[u][pl.ds(h, page, stride=heads), :]
            sums.append(jnp.sum(k_h, axis=0, keepdims=True))
            keys.append(k_h.astype(BF16))
        lt = _nt_dot(q16[:, h * HEAD_DIM:(h + 1) * HEAD_DIM], jnp.concatenate(keys, axis=0))
        l_ref[0, h] = lt[0:t] + lt[t:2 * t]
        blk_sums = [functools.reduce(lambda x, y: x + y, sums[b * ppb:(b + 1) * ppb]) for b in range(bps)]
        ksum_sc[h, pl.ds(pl.multiple_of(p * bps, bps), bps), :] = jnp.concatenate(blk_sums, axis=0)

    @pl.when(p == pl.num_programs(1) - 1)
    def _():
        nb = ksum_sc.shape[1]
        lane = lax.broadcasted_iota(jnp.int32, (16, LANES), 1)
        lane_f = lane.astype(F32)
        for h in range(heads):
            kmean = ksum_sc[h] * (1.0 / MOBA_BLOCK)
            km = jnp.concatenate([kmean, jnp.zeros((LANES - nb, HEAD_DIM), F32)], axis=0)
            km_hi, km_lo = _split_bf16(km)
            q_h = q16[:, h * HEAD_DIM:(h + 1) * HEAD_DIM]
            r1 = _nt_dot(q_h, km_hi)
            r2 = _nt_dot(q_h, km_lo)
            gate = r1 + pltpu.roll(r1, 16 - t, 0) + r2
            gate = jnp.where(lane < nb, gate, -jnp.inf)
            out = jnp.zeros((16, LANES), jnp.int32)
            for kk in range(MOBA_TOP_K):
                mx = jnp.max(gate, axis=1, keepdims=True)
                first = jnp.min(jnp.where(gate == mx, lane_f, float(LANES)), axis=1, keepdims=True)
                idx = jnp.where(mx > -jnp.inf, first, -1.0).astype(jnp.int32)
                out = jnp.where(lane == kk, idx, out)
                gate = jnp.where(lane_f == first, -jnp.inf, gate)
            ti_ref[0, h] = out[0:8]


def _cache_scan(page_table, q_s, cache, *, pps, page):
    nseq, n_pages = page_table.shape
    _, t, attn_dim = q_s.shape
    heads = attn_dim // HEAD_DIM
    steps = n_pages // pps
    past = n_pages * page
    nb = past // MOBA_BLOCK
    assert n_pages % pps == 0 and (pps * page) % MOBA_BLOCK == 0 and MOBA_BLOCK % page == 0
    assert nb <= LANES and 2 * t <= 8 and (pps * page // MOBA_BLOCK) % 8 == 0
    page_specs = [
        pl.BlockSpec((page * heads, HEAD_DIM), functools.partial(lambda s, p, pt, u: (pt[s, p * pps + u], 0), u=u))
        for u in range(pps)
    ]
    grid_spec = pltpu.PrefetchScalarGridSpec(
        num_scalar_prefetch=1,
        grid=(nseq, steps),
        in_specs=[pl.BlockSpec((1, t, attn_dim), lambda s, p, pt: (s, 0, 0))] + page_specs,
        out_specs=[
            pl.BlockSpec((1, heads, t, pps * page), lambda s, p, pt: (s, 0, 0, p)),
            pl.BlockSpec((1, heads, 8, LANES), lambda s, p, pt: (s, 0, 0, 0)),
        ],
        scratch_shapes=[pltpu.VMEM((heads, nb, HEAD_DIM), F32)],
    )
    return pl.pallas_call(
        functools.partial(_cache_scan_kernel, pps=pps, heads=heads, page=page),
        grid_spec=grid_spec,
        out_shape=[
            jax.ShapeDtypeStruct((nseq, heads, t, past), F32),
            jax.ShapeDtypeStruct((nseq, heads, 8, LANES), jnp.int32),
        ],
        compiler_params=_params("arbitrary", "arbitrary"),
        name="cache_scan",
    )(page_table, q_s, *([cache] * pps))


def _attn_sample_kernel(pt_ref, ti_ref, q_ref, kn_ref, vn_ref, l_hbm, cv_hbm, o_ref, vbuf, lbuf, sem, *, heads):
    s = pl.program_id(0)
    h = pl.program_id(1)
    n = s * heads + h
    total = pl.num_programs(0) * heads
    t = q_ref.shape[1]
    nsel = t * MOBA_TOP_K
    page = cv_hbm.shape[1]
    ppb = MOBA_BLOCK // page

    def copies(s_, h_, slot_):
        out = []
        for c in range(nsel):
            blk = jnp.maximum(ti_ref[s_, h_ * nsel + c], 0)
            out.append(pltpu.make_async_copy(
                l_hbm.at[s_, h_, :, pl.ds(pl.multiple_of(blk * MOBA_BLOCK, MOBA_BLOCK), MOBA_BLOCK)],
                lbuf.at[slot_, c], sem.at[slot_]))
            for e in range(ppb):
                pg = pt_ref[s_, blk * ppb + e]
                out.append(pltpu.make_async_copy(
                    cv_hbm.at[pg, :, h_, :], vbuf.at[slot_, c * ppb + e], sem.at[slot_]))
        return out

    n_slots = vbuf.shape[0]
    slot = n % n_slots

    def start_step(m):
        for cp in copies(m // heads, m % heads, m % n_slots):
            cp.start()

    @pl.when(n == 0)
    def _():
        for ahead in range(n_slots - 1):
            start_step(ahead)

    @pl.when(n + n_slots - 1 < total)
    def _():
        start_step(n + n_slots - 1)

    for cp in copies(s, h, slot):
        cp.wait()

    qs = q_ref[0] * SOFTMAX_SCALE
    kn = kn_ref[0]
    vn = vn_ref[0]
    width = nsel * MOBA_BLOCK
    segs = []
    for c in range(nsel):
        ok = ti_ref[s, h * nsel + c] >= 0
        segs.append(jnp.where(ok, lbuf[slot, c], MASKED))
    lm = jnp.concatenate(segs, axis=1)
    row = lax.broadcasted_iota(jnp.int32, (t, width), 0)
    col = lax.broadcasted_iota(jnp.int32, (t, width), 1)
    seg_w = MOBA_TOP_K * MOBA_BLOCK
    lm = jnp.where((col >= row * seg_w) & (col < (row + 1) * seg_w), lm, MASKED)
    row_o = lax.broadcasted_iota(jnp.int32, (t, LANES), 0)
    col_o = lax.broadcasted_iota(jnp.int32, (t, LANES), 1)
    own = jnp.full((t, LANES), MASKED, F32)
    for j in range(t):
        dj = jnp.sum(qs * kn[j:j + 1, :], axis=1, keepdims=True)
        own = jnp.where((col_o == j) & (j <= row_o), dj, own)
    full = jnp.concatenate([lm, own], axis=1)
    m = jnp.max(full, axis=1, keepdims=True)
    p = jnp.exp(full - m)
    l = jnp.sum(p, axis=1, keepdims=True)
    p_pad = jnp.concatenate([p[:, :width], jnp.zeros((8 - t, width), F32)], axis=0).astype(BF16)
    vsel = vbuf[slot].reshape(width, HEAD_DIM).astype(BF16)
    pv = jnp.dot(p_pad, vsel, preferred_element_type=F32)[0:t]
    for j in range(t):
        pv += p[:, width + j:width + j + 1] * vn[j:j + 1, :]
    o_ref[0] = pv / l


def _attn_sample(page_table, top_idx, q_s, k_s, v_s, logits, cache_v):
    nseq, t, attn_dim = q_s.shape
    heads = attn_dim // HEAD_DIM
    page = cache_v.shape[1]
    nsel = t * MOBA_TOP_K
    ppb = MOBA_BLOCK // page
    new_spec = pl.BlockSpec((1, t, HEAD_DIM), lambda s, h, pt, ti: (s, 0, h))
    grid_spec = pltpu.PrefetchScalarGridSpec(
        num_scalar_prefetch=2,
        grid=(nseq, heads),
        in_specs=[new_spec, new_spec, new_spec,
                  pl.BlockSpec(memory_space=pl.ANY), pl.BlockSpec(memory_space=pl.ANY)],
        out_specs=new_spec,
        scratch_shapes=[
            pltpu.VMEM((GATHER_SLOTS, nsel * ppb, page, HEAD_DIM), F32),
            pltpu.VMEM((GATHER_SLOTS, nsel, t, MOBA_BLOCK), F32),
            pltpu.SemaphoreType.DMA((GATHER_SLOTS,)),
        ],
    )
    return pl.pallas_call(
        functools.partial(_attn_sample_kernel, heads=heads),
        grid_spec=grid_spec,
        out_shape=jax.ShapeDtypeStruct((nseq, t, attn_dim), F32),
        compiler_params=_params("arbitrary", "arbitrary"),
        name="attn_sample",
    )(page_table, top_idx, q_s, k_s, v_s, logits, cache_v)


def _post_attn_kernel(x_ref, o_ref, mod_ref, g_ref, wo_ref, wup_ref, wdown_ref, y_ref):
    d = x_ref.shape[1]
    m = mod_ref[0]
    a = jnp.dot(o_ref[...].astype(BF16), wo_ref[...], preferred_element_type=F32)
    x1 = x_ref[...] + m[:, 2 * d:3 * d] * a
    h = _norm_mod(x1, g_ref[...], m[:, 3 * d:4 * d], m[:, 4 * d:5 * d])
    y_ref[...] = x1 + m[:, 5 * d:6 * d] * _mlp(h, wup_ref, wdown_ref)


def _post_attn(x, o, mod, g, wo, wup, wdown, *, tm, tiles_per_seq):
    rows, d = x.shape
    row_spec = pl.BlockSpec((tm, d), lambda i: (i, 0))
    return pl.pallas_call(
        _post_attn_kernel,
        grid=(rows // tm,),
        in_specs=[
            row_spec,
            pl.BlockSpec((tm, o.shape[1]), lambda i: (i, 0)),
            pl.BlockSpec((1, mod.shape[1], mod.shape[2]), lambda i: (i // tiles_per_seq, 0, 0)),
            _const_spec((1, d)),
            _const_spec(wo.shape), _const_spec(wup.shape), _const_spec(wdown.shape),
        ],
        out_specs=row_spec,
        out_shape=jax.ShapeDtypeStruct((rows, d), F32),
        compiler_params=_params("arbitrary"),
        name="post_attn",
    )(x, o, mod, g, wo, wup, wdown)


def _conv_layer_kernel(x_ref, mod_ref, g1_ref, g2_ref, gf_ref, wbcx_ref, wconv_ref, wout_ref, wup_ref, wdown_ref,
                       *rest, tiles_per_seq, period, tail):
    if period is None:
        y_ref, ut_ref, ext = rest
    else:
        e1_ref, e2_ref, y_ref, ut_ref, ext = rest
    i = pl.program_id(0)
    tm, d = x_ref.shape
    halo = 8
    m = mod_ref[0]
    x = x_ref[...]
    h = _norm_mod(x, g1_ref[...], m[:, 0:d], m[:, d:2 * d]).astype(BF16)
    bg = jnp.dot(h, wbcx_ref[:, 0:d], preferred_element_type=F32)
    cg = jnp.dot(h, wbcx_ref[:, d:2 * d], preferred_element_type=F32)
    xin = jnp.dot(h, wbcx_ref[:, 2 * d:3 * d], preferred_element_type=F32)
    u = cg * xin

    @pl.when(i % tiles_per_seq == 0)
    def _():
        ext[0:halo, :] = jnp.zeros((halo, d), F32)

    ext[halo:halo + tm, :] = u
    prev1 = ext[halo - 1:halo - 1 + tm, :]
    prev2 = ext[halo - 2:halo - 2 + tm, :]
    if period is not None:
        tpos = lax.broadcasted_iota(jnp.int32, (tm, d), 0) % period
        prev1 = jnp.where(tpos < 1, e1_ref[...], prev1)
        prev2 = jnp.where(tpos < 2, e2_ref[...], prev2)
    w = wconv_ref[...]
    conv = w[0:1, :] * prev2 + w[1:2, :] * prev1 + w[2:3, :] * u
    ext[0:halo, :] = ext[tm:tm + halo, :]
    ut_ref[0] = u[tm - tail:, :]

    y = jnp.dot((bg * conv).astype(BF16), wout_ref[...], preferred_element_type=F32)
    x1 = x + m[:, 2 * d:3 * d] * y
    h2 = _norm_mod(x1, g2_ref[...], m[:, 3 * d:4 * d], m[:, 4 * d:5 * d])
    x2 = x1 + m[:, 5 * d:6 * d] * _mlp(h2, wup_ref, wdown_ref)
    ms = jnp.mean(x2 * x2, axis=-1, keepdims=True)
    y_ref[...] = x2 * lax.rsqrt(ms + RMS_EPS) * gf_ref[...]


def _conv_layer(x, mod, g1, g2, gf, wbcx, wconv, wout, wup, wdown, state_rows, *, tm, tiles_per_seq, period, tail):
    rows, d = x.shape
    n_tiles = rows // tm
    row_spec = pl.BlockSpec((tm, d), lambda i: (i, 0))
    in_specs = [
        row_spec,
        pl.BlockSpec((1, mod.shape[1], mod.shape[2]), lambda i: (i // tiles_per_seq, 0, 0)),
        _const_spec((1, d)), _const_spec((1, d)), _const_spec((1, d)),
        _const_spec(wbcx.shape), _const_spec(wconv.shape), _const_spec(wout.shape),
        _const_spec(wup.shape), _const_spec(wdown.shape),
    ]
    args = [x, mod, g1, g2, gf, wbcx, wconv, wout, wup, wdown]
    if period is not None:
        in_specs += [row_spec, row_spec]
        args += list(state_rows)
    return pl.pallas_call(
        functools.partial(_conv_layer_kernel, tiles_per_seq=tiles_per_seq, period=period, tail=tail),
        grid=(n_tiles,),
        in_specs=in_specs,
        out_specs=[row_spec, pl.BlockSpec((1, tail, d), lambda i: (i, 0, 0))],
        out_shape=[jax.ShapeDtypeStruct((rows, d), F32), jax.ShapeDtypeStruct((n_tiles, tail, d), F32)],
        scratch_shapes=[pltpu.VMEM((tm + 8, d), F32)],
        compiler_params=_params("arbitrary"),
        name="conv_layer",
    )(*args)


def kernel(x_prompt, x_sample, cache_k, cache_v, state_conv, page_table, c_prompt, c_sample, w_ada, b_ada, g_norm,
           w_qkv, w_o, w_bcx, w_conv, w_out, w_up, w_down, g_final):
    batch, seq, d = x_prompt.shape
    nseq, t, _ = x_sample.shape
    depth = w_ada.shape[0]
    n_layers_attn, pool, page, heads, head_dim = cache_k.shape
    attn_dim = heads * head_dim
    n_pages = page_table.shape[1]
    past = n_pages * page
    kw = state_conv.shape[2] + 1
    assert depth == 2 and n_layers_attn == 1 and state_conv.shape[0] == 1, "layer 0 attention, layer 1 convolution"
    assert head_dim == HEAD_DIM and kw == 3 and t >= kw - 1
    assert seq % ROW_TILE == 0 and ROW_TILE % MOBA_BLOCK == 0 and past % MOBA_BLOCK == 0 and t <= MOBA_BLOCK
    rows_p, rows_s = batch * seq, nseq * t
    assert rows_s % 8 == 0

    n_c = batch + nseq
    pad = -n_c % 8
    c_all = jnp.concatenate([c_prompt, c_sample, jnp.zeros((pad, d), F32)], axis=0)
    mod = _ada(c_all, w_ada, b_ada)
    mod_p = mod[:, :batch, None, :]
    mod_s = jnp.repeat(mod[:, batch:n_c], t, axis=1)[:, None]

    bf = lambda a: a.astype(BF16)
    wqkv, wo, wbcx, wout = bf(w_qkv[0]), bf(w_o[0]), bf(w_bcx[0]), bf(w_out[0])
    wup, wdown = bf(w_up), bf(w_down)
    g = g_norm.reshape(depth, 2, 1, d)
    gf = g_final.reshape(1, d)

    xp = x_prompt.reshape(rows_p, d)
    xs = x_sample.reshape(rows_s, d)
    tps = seq // ROW_TILE
    tab_p = _rope_tables(np.arange(seq))
    tab_s = _rope_tables(past + (np.arange(rows_s) % t))

    q_p, k_p, v_p, kb_p, vb_p, ksum = _qkv(xp, mod_p[0], g[0, 0], wqkv, tab_p, tm=ROW_TILE, tiles_per_seq=tps,
                                           with_ksum=True)
    q_s, k_s, v_s, kb_s, vb_s = _qkv(xs, mod_s[0], g[0, 0], wqkv, tab_s, tm=rows_s, tiles_per_seq=1,
                                     with_ksum=False)
    o_p = _attn_prompt(q_p, kb_p, vb_p, ksum, batch=batch, seq=seq)

    q_s3, kn_s3, vn_s3 = (a.astype(F32).reshape(nseq, t, attn_dim) for a in (q_s, kb_s, vb_s))
    logits, top = _cache_scan(page_table, q_s3, cache_k.reshape(pool * page * heads, head_dim),
                              pps=PAGES_PER_STEP, page=page)
    top_idx = top[:, :, :t, :MOBA_TOP_K].reshape(nseq, heads * t * MOBA_TOP_K)
    o_s = _attn_sample(page_table, top_idx, q_s3, kn_s3, vn_s3, logits,
                       cache_v.reshape(pool, page, heads, head_dim))

    x1_p = _post_attn(xp, o_p, mod_p[0], g[0, 1], wo, wup[0], wdown[0], tm=ROW_TILE, tiles_per_seq=tps)
    x1_s = _post_attn(xs, o_s.reshape(rows_s, attn_dim), mod_s[0], g[0, 1], wo, wup[0], wdown[0],
                      tm=rows_s, tiles_per_seq=1)

    y_p, ut_p = _conv_layer(x1_p, mod_p[1], g[1, 0], g[1, 1], gf, wbcx, w_conv[0], wout, wup[1], wdown[1], None,
                            tm=ROW_TILE, tiles_per_seq=tps, period=None, tail=8)
    st = state_conv[0]
    zeros = jnp.zeros((nseq, t - 1, d), F32)
    e1 = jnp.concatenate([st[:, 1:2], zeros], axis=1).reshape(rows_s, d)
    e2 = jnp.concatenate([st[:, 0:2], zeros[:, 1:]], axis=1).reshape(rows_s, d)
    y_s, ut_s = _conv_layer(x1_s, mod_s[1], g[1, 0], g[1, 1], gf, wbcx, w_conv[0], wout, wup[1], wdown[1], (e1, e2),
                            tm=rows_s, tiles_per_seq=1, period=t, tail=rows_s)

    conv_p = ut_p.reshape(batch, tps, 8, d)[:, -1, 8 - (kw - 1):, :]
    conv_s = ut_s.reshape(nseq, t, d)[:, t - (kw - 1):, :]
    return (
        y_p.reshape(batch, seq, d),
        y_s.reshape(nseq, t, d),
        k_p.reshape(1, batch, seq, heads, head_dim),
        v_p.reshape(1, batch, seq, heads, head_dim),
        k_s.reshape(1, nseq, t, heads, head_dim),
        v_s.reshape(1, nseq, t, heads, head_dim),
        conv_p[None],
        conv_s[None],
    )
```

```python
import functools

import numpy as np
import jax
import jax.numpy as jnp
from jax import lax
from jax.experimental import pallas as pl
from jax.experimental.pallas import tpu as pltpu

F32 = jnp.float32
BF16 = jnp.bfloat16

HEAD_DIM = 128
MOBA_BLOCK = 256
MOBA_TOP_K = 3
ROPE_DIM = HEAD_DIM // 4
ROPE_THETA = 500000.0
RMS_EPS = 1e-6
SOFTMAX_SCALE = HEAD_DIM ** -0.5
MASKED = -1e30

LANES = 128
VMEM_LIMIT = 56 * 1024 * 1024
ROW_TILE = 256
ADA_COL_TILE = 1536
GATHER_SLOTS = 4
ATTN_HEADS_PER_STEP = 2
ATTN_CHUNK_BLOCKS = 4

NT_DIMS = (((1,), (1,)), ((), ()))


def _nt_dot(a, b):
    return lax.dot_general(a, b, NT_DIMS, preferred_element_type=F32)


def _split_bf16(x):
    hi = x.astype(BF16)
    lo = (x - hi.astype(F32)).astype(BF16)
    return hi, lo


def _const_spec(shape):
    n = len(shape)
    return pl.BlockSpec(shape, lambda *_: (0,) * n, pipeline_mode=pl.Buffered(1))


def _params(*sem):
    return pltpu.CompilerParams(dimension_semantics=sem, vmem_limit_bytes=VMEM_LIMIT)


def _ada_kernel(c_ref, w_ref, b_ref, o_ref):
    c_hi, c_lo = _split_bf16(c_ref[...])
    w_hi, w_lo = _split_bf16(w_ref[0])
    acc = jnp.dot(c_hi, w_hi, preferred_element_type=F32)
    acc += jnp.dot(c_lo, w_hi, preferred_element_type=F32)
    acc += jnp.dot(c_hi, w_lo, preferred_element_type=F32)
    o_ref[0] = acc + b_ref[0]


def _ada(c_all, w_ada, b_ada):
    depth, d, n6 = w_ada.shape
    rows = c_all.shape[0]
    tn = ADA_COL_TILE
    return pl.pallas_call(
        _ada_kernel,
        grid=(depth, n6 // tn),
        in_specs=[
            pl.BlockSpec((rows, d), lambda l, n: (0, 0)),
            pl.BlockSpec((1, d, tn), lambda l, n: (l, 0, n)),
            pl.BlockSpec((1, 1, tn), lambda l, n: (l, 0, n)),
        ],
        out_specs=pl.BlockSpec((1, rows, tn), lambda l, n: (l, 0, n)),
        out_shape=jax.ShapeDtypeStruct((depth, rows, n6), F32),
        compiler_params=_params("arbitrary", "arbitrary"),
        name="ada",
    )(c_all, w_ada, b_ada.reshape(depth, 1, n6))


def _norm_mod(x, g, shift, scale):
    ms = jnp.mean(x * x, axis=-1, keepdims=True)
    y = x * lax.rsqrt(ms + RMS_EPS) * g
    return y * (1.0 + scale) + shift


def _mlp(h, wup_ref, wdown_ref):
    d, dff = wup_ref.shape
    hb = h.astype(BF16)
    acc = jnp.zeros((h.shape[0], d), F32)
    for c in range(0, dff, d):
        u = jnp.dot(hb, wup_ref[:, c:c + d], preferred_element_type=F32)
        u = jnp.square(jnp.maximum(u, 0.0)).astype(BF16)
        acc += jnp.dot(u, wdown_ref[c:c + d, :], preferred_element_type=F32)
    return acc


def _rope_tables(pos):
    half = ROPE_DIM // 2
    inv = np.float32(ROPE_THETA) ** (-np.arange(0, ROPE_DIM, 2, dtype=np.float32) / np.float32(ROPE_DIM))
    ang = (pos.astype(np.float32)[:, None] * inv.astype(np.float32)[None, :]).astype(np.float32)
    cos = np.cos(ang.astype(np.float64))
    sin = np.sin(ang.astype(np.float64))
    n = pos.shape[0]
    c = np.ones((n, HEAD_DIM), np.float64)
    s1 = np.zeros((n, HEAD_DIM), np.float64)
    s2 = np.zeros((n, HEAD_DIM), np.float64)
    c[:, :half] = cos
    c[:, half:ROPE_DIM] = cos
    s1[:, :half] = -sin
    s2[:, half:ROPE_DIM] = sin
    return tuple(jnp.asarray(t.astype(np.float32)) for t in (c, s1, s2))


def _qkv_kernel(x_ref, mod_ref, g_ref, w_ref, cos_ref, s1_ref, s2_ref, q_ref, k5_ref, v5_ref, kb_ref, vb_ref,
                *ks_ref, attn_dim):
    d = x_ref.shape[1]
    m = mod_ref[0]
    h = _norm_mod(x_ref[...], g_ref[...], m[:, 0:d], m[:, d:2 * d]).astype(BF16)
    cos, s1, s2 = cos_ref[...], s1_ref[...], s2_ref[...]
    tm = h.shape[0]
    heads = attn_dim // HEAD_DIM
    cw = 2 * HEAD_DIM
    for c in range(0, 3 * attn_dim, cw):
        r = jnp.dot(h, w_ref[:, c:c + cw], preferred_element_type=F32)
        if c < 2 * attn_dim:
            rot = []
            for u in range(0, cw, HEAD_DIM):
                z = r[:, u:u + HEAD_DIM]
                z = (z * cos + pltpu.roll(z, HEAD_DIM - ROPE_DIM // 2, 1) * s1
                     + pltpu.roll(z, ROPE_DIM // 2, 1) * s2)
                rot.append(z)
            r = jnp.concatenate(rot, axis=1)
        if c < attn_dim:
            q_ref[:, c:c + cw] = r
            continue
        is_k = c < 2 * attn_dim
        c0 = c - (attn_dim if is_k else 2 * attn_dim)
        (kb_ref if is_k else vb_ref)[:, c0:c0 + cw] = r.astype(BF16)
        for u in range(0, cw, HEAD_DIM):
            head = (c0 + u) // HEAD_DIM
            (k5_ref if is_k else v5_ref)[pl.ds(head, tm, stride=heads), :] = r[:, u:u + HEAD_DIM]
        if is_k and ks_ref:
            nblk = tm // MOBA_BLOCK
            ks_ref[0][0, :, c0:c0 + cw] = jnp.sum(r.reshape(nblk, MOBA_BLOCK, cw), axis=1)


def _qkv(x, mod, g, w, tables, *, tm, tiles_per_seq, with_ksum):
    rows, d = x.shape
    attn_dim = w.shape[1] // 3
    heads = attn_dim // HEAD_DIM
    n_tiles = rows // tm
    mod_rows = mod.shape[1]
    row_spec = pl.BlockSpec((tm, attn_dim), lambda i: (i, 0))
    tok_head_spec = pl.BlockSpec((tm * heads, HEAD_DIM), lambda i: (i, 0))
    tab_spec = pl.BlockSpec((tm, HEAD_DIM), lambda i: (i % tiles_per_seq, 0))
    out_specs = [row_spec, tok_head_spec, tok_head_spec, row_spec, row_spec]
    out_shape = ([jax.ShapeDtypeStruct((rows, attn_dim), F32)]
                 + [jax.ShapeDtypeStruct((rows * heads, HEAD_DIM), F32)] * 2
                 + [jax.ShapeDtypeStruct((rows, attn_dim), BF16)] * 2)
    if with_ksum:
        nblk = tm // MOBA_BLOCK
        out_specs.append(pl.BlockSpec((1, nblk, attn_dim), lambda i: (i, 0, 0)))
        out_shape.append(jax.ShapeDtypeStruct((n_tiles, nblk, attn_dim), F32))
    return pl.pallas_call(
        functools.partial(_qkv_kernel, attn_dim=attn_dim),
        grid=(n_tiles,),
        in_specs=[
            pl.BlockSpec((tm, d), lambda i: (i, 0)),
            pl.BlockSpec((1, mod_rows, mod.shape[2]), lambda i: (i // tiles_per_seq, 0, 0)),
            _const_spec((1, d)),
            _const_spec(w.shape),
            tab_spec, tab_spec, tab_spec,
        ],
        out_specs=out_specs,
        out_shape=out_shape,
        compiler_params=_params("arbitrary"),
        name="qkv",
    )(x, mod, g, w, *tables)


def _attn_prompt_step(i, q_ref, k_ref, v_ref, ks_ref, o_ref, kaug, vaug, eye, s_sc, *, hps, chunk_blocks, side_work):
    tq = q_ref.shape[0]
    seq = k_ref.shape[0]
    nb = seq // MOBA_BLOCK
    ck = chunk_blocks * MOBA_BLOCK
    n_chunk_max = nb // chunk_blocks

    @pl.when(i == 0)
    def _():
        r_id = lax.broadcasted_iota(jnp.int32, (tq, tq), 0)
        c_id = lax.broadcasted_iota(jnp.int32, (tq, tq), 1)
        eye[...] = jnp.where(r_id == c_id, 1.0, 0.0).astype(BF16)
        lane = lax.broadcasted_iota(jnp.int32, (MOBA_BLOCK, LANES), 1)
        ones_col = jnp.where(lane == 0, 1.0, 0.0).astype(BF16)
        for hh in range(hps):
            cols = slice(hh * HEAD_DIM, (hh + 1) * HEAD_DIM)

            def fill(j, carry):
                rows = pl.ds(pl.multiple_of(j * MOBA_BLOCK, MOBA_BLOCK), MOBA_BLOCK)
                kaug[hh, rows, 0:HEAD_DIM] = k_ref[rows, cols]
                kaug[hh, rows, HEAD_DIM:] = jnp.where(lane == j, 1.0, 0.0).astype(BF16)
                vaug[hh, rows, 0:HEAD_DIM] = v_ref[rows, cols]
                vaug[hh, rows, HEAD_DIM:] = ones_col
                return carry
            lax.fori_loop(0, nb, fill, 0)

    def head(hh, n_chunks):
        cols = slice(hh * HEAD_DIM, (hh + 1) * HEAD_DIM)
        q = q_ref[:, cols]
        qs = (q * (SOFTMAX_SCALE * np.log2(np.e))).astype(BF16)

        kmean = (ks_ref[0][:, cols] * (1.0 / MOBA_BLOCK)).astype(BF16)
        gate = _nt_dot(kmean, q.astype(BF16))
        blk = lax.broadcasted_iota(jnp.int32, (nb, tq), 0)
        blk_f = blk.astype(F32)
        gate = jnp.where(blk < i, gate, -jnp.inf)
        sel = jnp.zeros((nb, tq), F32)
        for _ in range(MOBA_TOP_K):
            mx = jnp.max(gate, axis=0, keepdims=True)
            first = jnp.min(jnp.where(gate == mx, blk_f, float(nb)), axis=0, keepdims=True)
            hit = blk_f == first
            sel = jnp.where(hit & (mx > -jnp.inf), 1.0, sel)
            gate = jnp.where(hit, -jnp.inf, gate)
        sel_pad = jnp.concatenate([sel, jnp.zeros((LANES - nb, tq), F32)], axis=0).astype(BF16)
        sel_q = _nt_dot(eye[...], sel_pad)
        bias = jnp.where(sel_q > 0.5, 0.0, MASKED).astype(BF16)
        qa = jnp.concatenate([qs, bias], axis=1)

        own = pl.ds(pl.multiple_of(i * MOBA_BLOCK, MOBA_BLOCK), MOBA_BLOCK)
        s_own = _nt_dot(qs, kaug[hh, own, 0:HEAD_DIM])
        r_id = lax.broadcasted_iota(jnp.int32, s_own.shape, 0)
        c_id = lax.broadcasted_iota(jnp.int32, s_own.shape, 1)
        s_own = jnp.where(c_id <= r_id, s_own, MASKED)
        run = jnp.maximum(s_own[:, 0:LANES], s_own[:, LANES:])
        for c in range(n_chunks):
            s_c = _nt_dot(qa, kaug[hh, c * ck:(c + 1) * ck, :])
            s_sc[hh, :, c * ck:(c + 1) * ck] = s_c
            for u in range(0, ck, LANES):
                run = jnp.maximum(run, s_c[:, u:u + LANES])
        m = jnp.max(run, axis=1, keepdims=True)
        acc = jnp.dot(jnp.exp2(s_own - m).astype(BF16), vaug[hh, own, :], preferred_element_type=F32)
        for c in range(n_chunks):
            p_c = jnp.exp2(s_sc[hh, :, c * ck:(c + 1) * ck] - m).astype(BF16)
            acc += jnp.dot(p_c, vaug[hh, c * ck:(c + 1) * ck, :], preferred_element_type=F32)
        o_ref[:, cols] = (acc[:, 0:HEAD_DIM] / acc[:, HEAD_DIM:HEAD_DIM + 1]).astype(o_ref.dtype)

    n_needed = (i + chunk_blocks - 1) // chunk_blocks
    for n in range(n_chunk_max + 1):
        @pl.when(n_needed == n)
        def _(n=n):
            side_work()
            for hh in range(hps):
                head(hh, n)


def _scan_queries(q_ref):
    t = q_ref.shape[1]
    qs = q_ref[0] * SOFTMAX_SCALE
    q_hi = qs.astype(BF16).astype(F32)
    q_lo = (qs - q_hi).astype(BF16).astype(F32)
    return jnp.concatenate([q_hi, q_lo, jnp.zeros((16 - 2 * t, qs.shape[1]), F32)], axis=0).astype(BF16)


def _cache_scan_pages(p, q_ref, page_refs, l_ref, ksum_sc, *, heads, page):
    pps = len(page_refs)
    t = q_ref.shape[1]
    ppb = MOBA_BLOCK // page
    bps = pps // ppb
    q16 = _scan_queries(q_ref)
    for h in range(heads):
        keys, sums = [], []
        for u in range(pps):
            k_h = page_refs[u][pl.ds(h, page, stride=heads), :]
            sums.append(jnp.sum(k_h, axis=0, keepdims=True))
            keys.append(k_h.astype(BF16))
        lt = _nt_dot(q16[:, h * HEAD_DIM:(h + 1) * HEAD_DIM], jnp.concatenate(keys, axis=0))
        l_ref[0, h] = lt[0:t] + lt[t:2 * t]
        blk_sums = [functools.reduce(lambda x, y: x + y, sums[b * ppb:(b + 1) * ppb]) for b in range(bps)]
        ksum_sc[h, pl.ds(pl.multiple_of(p * bps, bps), bps), :] = jnp.concatenate(blk_sums, axis=0)


def _cache_scan_gate(q_ref, ti_ref, ksum_sc, *, heads):
    t = q_ref.shape[1]
    q16 = _scan_queries(q_ref)
    nb = ksum_sc.shape[1]
    lane = lax.broadcasted_iota(jnp.int32, (16, LANES), 1)
    lane_f = lane.astype(F32)
    for h in range(heads):
        kmean = ksum_sc[h] * (1.0 / MOBA_BLOCK)
        km = jnp.concatenate([kmean, jnp.zeros((LANES - nb, HEAD_DIM), F32)], axis=0)
        km_hi, km_lo = _split_bf16(km)
        q_h = q16[:, h * HEAD_DIM:(h + 1) * HEAD_DIM]
        r1 = _nt_dot(q_h, km_hi)
        r2 = _nt_dot(q_h, km_lo)
        gate = r1 + pltpu.roll(r1, 16 - t, 0) + r2
        gate = jnp.where(lane < nb, gate, -jnp.inf)
        out = jnp.zeros((16, LANES), jnp.int32)
        for kk in range(MOBA_TOP_K):
            mx = jnp.max(gate, axis=1, keepdims=True)
            first = jnp.min(jnp.where(gate == mx, lane_f, float(LANES)), axis=1, keepdims=True)
            idx = jnp.where(mx > -jnp.inf, first, -1.0).astype(jnp.int32)
            out = jnp.where(lane == kk, idx, out)
            gate = jnp.where(lane_f == first, -jnp.inf, gate)
        ti_ref[0, h] = out[0:8]


def _attn_scan_kernel(pt_ref, q_ref, k_ref, v_ref, ks_ref, qs_ref, *rest, pps, hps, heads, page, scan_steps):
    page_refs = rest[:pps]
    o_ref, l_ref, ti_ref, kaug, vaug, eye, s_sc, ksum_sc = rest[pps:]
    i = pl.program_id(2)
    flat = (pl.program_id(0) * pl.num_programs(1) + pl.program_id(1)) * pl.num_programs(2) + i
    p = flat % scan_steps

    def scan_pages():
        _cache_scan_pages(p, qs_ref, page_refs, l_ref, ksum_sc, heads=heads, page=page)

    _attn_prompt_step(i, q_ref, k_ref, v_ref, ks_ref, o_ref, kaug, vaug, eye, s_sc,
                      hps=hps, chunk_blocks=ATTN_CHUNK_BLOCKS, side_work=scan_pages)

    @pl.when(p == scan_steps - 1)
    def _():
        _cache_scan_gate(qs_ref, ti_ref, ksum_sc, heads=heads)


def _attn_scan(page_table, q, k, v, ksum, q_s, cache, *, batch, seq, page):
    rows, attn_dim = q.shape
    heads = attn_dim // HEAD_DIM
    tq = MOBA_BLOCK
    nq = seq // tq
    nb = seq // MOBA_BLOCK
    hps = ATTN_HEADS_PER_STEP
    n_hp = heads // hps
    assert nb <= LANES and nb % ATTN_CHUNK_BLOCKS == 0 and heads % hps == 0

    nseq, n_pages = page_table.shape
    t = q_s.shape[1]
    past = n_pages * page
    nb_past = past // MOBA_BLOCK
    n_steps = batch * n_hp * nq
    assert n_steps % nseq == 0
    scan_steps = n_steps // nseq
    assert n_pages % scan_steps == 0
    pps = n_pages // scan_steps
    assert (pps * page) % MOBA_BLOCK == 0 and MOBA_BLOCK % page == 0
    assert nb_past <= LANES and 2 * t <= 8 and (pps * page // MOBA_BLOCK) % 8 == 0

    def flat(b, h, i):
        return (b * n_hp + h) * nq + i

    q_spec = pl.BlockSpec((tq, hps * HEAD_DIM), lambda b, h, i, pt: (b * nq + i, h))
    kv_spec = pl.BlockSpec((seq, hps * HEAD_DIM), lambda b, h, i, pt: (b, h))
    page_specs = [
        pl.BlockSpec((page * heads, HEAD_DIM),
                     functools.partial(lambda b, h, i, pt, u: (pt[flat(b, h, i) // scan_steps,
                                                                  (flat(b, h, i) % scan_steps) * pps + u], 0), u=u))
        for u in range(pps)
    ]
    grid_spec = pltpu.PrefetchScalarGridSpec(
        num_scalar_prefetch=1,
        grid=(batch, n_hp, nq),
        in_specs=[q_spec, kv_spec, kv_spec,
                  pl.BlockSpec((1, nb, hps * HEAD_DIM), lambda b, h, i, pt: (b, 0, h)),
                  pl.BlockSpec((1, t, attn_dim), lambda b, h, i, pt: (flat(b, h, i) // scan_steps, 0, 0))]
        + page_specs,
        out_specs=[
            q_spec,
            pl.BlockSpec((1, heads, t, pps * page),
                         lambda b, h, i, pt: (flat(b, h, i) // scan_steps, 0, 0, flat(b, h, i) % scan_steps)),
            pl.BlockSpec((1, heads, 8, LANES), lambda b, h, i, pt: (flat(b, h, i) // scan_steps, 0, 0, 0)),
        ],
        scratch_shapes=[
            pltpu.VMEM((hps, seq, 2 * HEAD_DIM), BF16),
            pltpu.VMEM((hps, seq, 2 * HEAD_DIM), BF16),
            pltpu.VMEM((tq, tq), BF16),
            pltpu.VMEM((hps, tq, seq), F32),
            pltpu.VMEM((heads, nb_past, HEAD_DIM), F32),
        ],
    )
    return pl.pallas_call(
        functools.partial(_attn_scan_kernel, pps=pps, hps=hps, heads=heads, page=page, scan_steps=scan_steps),
        grid_spec=grid_spec,
        out_shape=[
            jax.ShapeDtypeStruct((rows, attn_dim), BF16),
            jax.ShapeDtypeStruct((nseq, heads, t, past), F32),
            jax.ShapeDtypeStruct((nseq, heads, 8, LANES), jnp.int32),
        ],
        compiler_params=_params("arbitrary", "arbitrary", "arbitrary"),
        name="attn_scan",
    )(page_table, q, k, v, ksum.reshape(batch, nb, attn_dim), q_s, *([cache] * pps))


def _attn_sample_kernel(pt_ref, ti_ref, q_ref, kn_ref, vn_ref, l_hbm, cv_hbm, o_ref, vbuf, lbuf, sem, *, heads):
    s = pl.program_id(0)
    h = pl.program_id(1)
    n = s * heads + h
    total = pl.num_programs(0) * heads
    t = q_ref.shape[1]
    nsel = t * MOBA_TOP_K
    page = cv_hbm.shape[1]
    ppb = MOBA_BLOCK // page

    def copies(s_, h_, slot_):
        out = []
        for c in range(nsel):
            blk = jnp.maximum(ti_ref[s_, h_ * nsel + c], 0)
            out.append(pltpu.make_async_copy(
                l_hbm.at[s_, h_, :, pl.ds(pl.multiple_of(blk * MOBA_BLOCK, MOBA_BLOCK), MOBA_BLOCK)],
                lbuf.at[slot_, c], sem.at[slot_]))
            for e in range(ppb):
                pg = pt_ref[s_, blk * ppb + e]
                out.append(pltpu.make_async_copy(
                    cv_hbm.at[pg, :, h_, :], vbuf.at[slot_, c * ppb + e], sem.at[slot_]))
        return out

    n_slots = vbuf.shape[0]
    slot = n % n_slots

    def start_step(m):
        for cp in copies(m // heads, m % heads, m % n_slots):
            cp.start()

    @pl.when(n == 0)
    def _():
        for ahead in range(n_slots - 1):
            start_step(ahead)

    @pl.when(n + n_slots - 1 < total)
    def _():
        start_step(n + n_slots - 1)

    for cp in copies(s, h, slot):
        cp.wait()

    qs = q_ref[0] * SOFTMAX_SCALE
    kn = kn_ref[0]
    vn = vn_ref[0]
    width = nsel * MOBA_BLOCK
    segs = []
    for c in range(nsel):
        ok = ti_ref[s, h * nsel + c] >= 0
        segs.append(jnp.where(ok, lbuf[slot, c], MASKED))
    lm = jnp.concatenate(segs, axis=1)
    row = lax.broadcasted_iota(jnp.int32, (t, width), 0)
    col = lax.broadcasted_iota(jnp.int32, (t, width), 1)
    seg_w = MOBA_TOP_K * MOBA_BLOCK
    lm = jnp.where((col >= row * seg_w) & (col < (row + 1) * seg_w), lm, MASKED)
    row_o = lax.broadcasted_iota(jnp.int32, (t, LANES), 0)
    col_o = lax.broadcasted_iota(jnp.int32, (t, LANES), 1)
    own = jnp.full((t, LANES), MASKED, F32)
    for j in range(t):
        dj = jnp.sum(qs * kn[j:j + 1, :], axis=1, keepdims=True)
        own = jnp.where((col_o == j) & (j <= row_o), dj, own)
    full = jnp.concatenate([lm, own], axis=1)
    m = jnp.max(full, axis=1, keepdims=True)
    p = jnp.exp(full - m)
    l = jnp.sum(p, axis=1, keepdims=True)
    p_pad = jnp.concatenate([p[:, :width], jnp.zeros((8 - t, width), F32)], axis=0).astype(BF16)
    vsel = vbuf[slot].reshape(width, HEAD_DIM).astype(BF16)
    pv = jnp.dot(p_pad, vsel, preferred_element_type=F32)[0:t]
    for j in range(t):
        pv += p[:, width + j:width + j + 1] * vn[j:j + 1, :]
    o_ref[0] = pv / l


def _attn_sample(page_table, top_idx, q_s, k_s, v_s, logits, cache_v):
    nseq, t, attn_dim = q_s.shape
    heads = attn_dim // HEAD_DIM
    page = cache_v.shape[1]
    nsel = t * MOBA_TOP_K
    ppb = MOBA_BLOCK // page
    new_spec = pl.BlockSpec((1, t, HEAD_DIM), lambda s, h, pt, ti: (s, 0, h))
    grid_spec = pltpu.PrefetchScalarGridSpec(
        num_scalar_prefetch=2,
        grid=(nseq, heads),
        in_specs=[new_spec, new_spec, new_spec,
                  pl.BlockSpec(memory_space=pl.ANY), pl.BlockSpec(memory_space=pl.ANY)],
        out_specs=new_spec,
        scratch_shapes=[
            pltpu.VMEM((GATHER_SLOTS, nsel * ppb, page, HEAD_DIM), F32),
            pltpu.VMEM((GATHER_SLOTS, nsel, t, MOBA_BLOCK), F32),
            pltpu.SemaphoreType.DMA((GATHER_SLOTS,)),
        ],
    )
    return pl.pallas_call(
        functools.partial(_attn_sample_kernel, heads=heads),
        grid_spec=grid_spec,
        out_shape=jax.ShapeDtypeStruct((nseq, t, attn_dim), F32),
        compiler_params=_params("arbitrary", "arbitrary"),
        name="attn_sample",
    )(page_table, top_idx, q_s, k_s, v_s, logits, cache_v)


def _post_attn_kernel(x_ref, o_ref, mod_ref, g_ref, wo_ref, wup_ref, wdown_ref, y_ref):
    d = x_ref.shape[1]
    m = mod_ref[0]
    a = jnp.dot(o_ref[...].astype(BF16), wo_ref[...], preferred_element_type=F32)
    x1 = x_ref[...] + m[:, 2 * d:3 * d] * a
    h = _norm_mod(x1, g_ref[...], m[:, 3 * d:4 * d], m[:, 4 * d:5 * d])
    y_ref[...] = x1 + m[:, 5 * d:6 * d] * _mlp(h, wup_ref, wdown_ref)


def _post_attn(x, o, mod, g, wo, wup, wdown, *, tm, tiles_per_seq):
    rows, d = x.shape
    row_spec = pl.BlockSpec((tm, d), lambda i: (i, 0))
    return pl.pallas_call(
        _post_attn_kernel,
        grid=(rows // tm,),
        in_specs=[
            row_spec,
            pl.BlockSpec((tm, o.shape[1]), lambda i: (i, 0)),
            pl.BlockSpec((1, mod.shape[1], mod.shape[2]), lambda i: (i // tiles_per_seq, 0, 0)),
            _const_spec((1, d)),
            _const_spec(wo.shape), _const_spec(wup.shape), _const_spec(wdown.shape),
        ],
        out_specs=row_spec,
        out_shape=jax.ShapeDtypeStruct((rows, d), F32),
        compiler_params=_params("arbitrary"),
        name="post_attn",
    )(x, o, mod, g, wo, wup, wdown)


def _conv_layer_kernel(x_ref, mod_ref, g1_ref, g2_ref, gf_ref, wbcx_ref, wconv_ref, wout_ref, wup_ref, wdown_ref,
                       *rest, tiles_per_seq, period, tail):
    if period is None:
        y_ref, ut_ref, ext = rest
    else:
        e1_ref, e2_ref, y_ref, ut_ref, ext = rest
    i = pl.program_id(0)
    tm, d = x_ref.shape
    halo = 8
    m = mod_ref[0]
    x = x_ref[...]
    h = _norm_mod(x, g1_ref[...], m[:, 0:d], m[:, d:2 * d]).astype(BF16)
    bg = jnp.dot(h, wbcx_ref[:, 0:d], preferred_element_type=F32)
    cg = jnp.dot(h, wbcx_ref[:, d:2 * d], preferred_element_type=F32)
    xin = jnp.dot(h, wbcx_ref[:, 2 * d:3 * d], preferred_element_type=F32)
    u = cg * xin

    @pl.when(i % tiles_per_seq == 0)
    def _():
        ext[0:halo, :] = jnp.zeros((halo, d), F32)

    ext[halo:halo + tm, :] = u
    prev1 = ext[halo - 1:halo - 1 + tm, :]
    prev2 = ext[halo - 2:halo - 2 + tm, :]
    if period is not None:
        tpos = lax.broadcasted_iota(jnp.int32, (tm, d), 0) % period
        prev1 = jnp.where(tpos < 1, e1_ref[...], prev1)
        prev2 = jnp.where(tpos < 2, e2_ref[...], prev2)
    w = wconv_ref[...]
    conv = w[0:1, :] * prev2 + w[1:2, :] * prev1 + w[2:3, :] * u
    ext[0:halo, :] = ext[tm:tm + halo, :]
    ut_ref[0] = u[tm - tail:, :]

    y = jnp.dot((bg * conv).astype(BF16), wout_ref[...], preferred_element_type=F32)
    x1 = x + m[:, 2 * d:3 * d] * y
    h2 = _norm_mod(x1, g2_ref[...], m[:, 3 * d:4 * d], m[:, 4 * d:5 * d])
    x2 = x1 + m[:, 5 * d:6 * d] * _mlp(h2, wup_ref, wdown_ref)
    ms = jnp.mean(x2 * x2, axis=-1, keepdims=True)
    y_ref[...] = x2 * lax.rsqrt(ms + RMS_EPS) * gf_ref[...]


def _conv_layer(x, mod, g1, g2, gf, wbcx, wconv, wout, wup, wdown, state_rows, *, tm, tiles_per_seq, period, tail):
    rows, d = x.shape
    n_tiles = rows // tm
    row_spec = pl.BlockSpec((tm, d), lambda i: (i, 0))
    in_specs = [
        row_spec,
        pl.BlockSpec((1, mod.shape[1], mod.shape[2]), lambda i: (i // tiles_per_seq, 0, 0)),
        _const_spec((1, d)), _const_spec((1, d)), _const_spec((1, d)),
        _const_spec(wbcx.shape), _const_spec(wconv.shape), _const_spec(wout.shape),
        _const_spec(wup.shape), _const_spec(wdown.shape),
    ]
    args = [x, mod, g1, g2, gf, wbcx, wconv, wout, wup, wdown]
    if period is not None:
        in_specs += [row_spec, row_spec]
        args += list(state_rows)
    return pl.pallas_call(
        functools.partial(_conv_layer_kernel, tiles_per_seq=tiles_per_seq, period=period, tail=tail),
        grid=(n_tiles,),
        in_specs=in_specs,
        out_specs=[row_spec, pl.BlockSpec((1, tail, d), lambda i: (i, 0, 0))],
        out_shape=[jax.ShapeDtypeStruct((rows, d), F32), jax.ShapeDtypeStruct((n_tiles, tail, d), F32)],
        scratch_shapes=[pltpu.VMEM((tm + 8, d), F32)],
        compiler_params=_params("arbitrary"),
        name="conv_layer",
    )(*args)


def kernel(x_prompt, x_sample, cache_k, cache_v, state_conv, page_table, c_prompt, c_sample, w_ada, b_ada, g_norm,
           w_qkv, w_o, w_bcx, w_conv, w_out, w_up, w_down, g_final):
    batch, seq, d = x_prompt.shape
    nseq, t, _ = x_sample.shape
    depth = w_ada.shape[0]
    n_layers_attn, pool, page, heads, head_dim = cache_k.shape
    attn_dim = heads * head_dim
    n_pages = page_table.shape[1]
    past = n_pages * page
    kw = state_conv.shape[2] + 1
    assert depth == 2 and n_layers_attn == 1 and state_conv.shape[0] == 1, "layer 0 attention, layer 1 convolution"
    assert head_dim == HEAD_DIM and kw == 3 and t >= kw - 1
    assert seq % ROW_TILE == 0 and ROW_TILE % MOBA_BLOCK == 0 and past % MOBA_BLOCK == 0 and t <= MOBA_BLOCK
    rows_p, rows_s = batch * seq, nseq * t
    assert rows_s % 8 == 0

    n_c = batch + nseq
    pad = -n_c % 8
    c_all = jnp.concatenate([c_prompt, c_sample, jnp.zeros((pad, d), F32)], axis=0)
    mod = _ada(c_all, w_ada, b_ada)
    mod_p = mod[:, :batch, None, :]
    mod_s = jnp.repeat(mod[:, batch:n_c], t, axis=1)[:, None]

    bf = lambda a: a.astype(BF16)
    wqkv, wo, wbcx, wout = bf(w_qkv[0]), bf(w_o[0]), bf(w_bcx[0]), bf(w_out[0])
    wup, wdown = bf(w_up), bf(w_down)
    g = g_norm.reshape(depth, 2, 1, d)
    gf = g_final.reshape(1, d)

    xp = x_prompt.reshape(rows_p, d)
    xs = x_sample.reshape(rows_s, d)
    tps = seq // ROW_TILE
    tab_p = _rope_tables(np.arange(seq))
    tab_s = _rope_tables(past + (np.arange(rows_s) % t))

    q_p, k_p, v_p, kb_p, vb_p, ksum = _qkv(xp, mod_p[0], g[0, 0], wqkv, tab_p, tm=ROW_TILE, tiles_per_seq=tps,
                                           with_ksum=True)
    q_s, k_s, v_s, kb_s, vb_s = _qkv(xs, mod_s[0], g[0, 0], wqkv, tab_s, tm=rows_s, tiles_per_seq=1,
                                     with_ksum=False)
    q_s3, kn_s3, vn_s3 = (a.astype(F32).reshape(nseq, t, attn_dim) for a in (q_s, kb_s, vb_s))
    o_p, logits, top = _attn_scan(page_table, q_p, kb_p, vb_p, ksum, q_s3,
                                  cache_k.reshape(pool * page * heads, head_dim), batch=batch, seq=seq, page=page)
    top_idx = top[:, :, :t, :MOBA_TOP_K].reshape(nseq, heads * t * MOBA_TOP_K)
    o_s = _attn_sample(page_table, top_idx, q_s3, kn_s3, vn_s3, logits,
                       cache_v.reshape(pool, page, heads, head_dim))

    x1_p = _post_attn(xp, o_p, mod_p[0], g[0, 1], wo, wup[0], wdown[0], tm=ROW_TILE, tiles_per_seq=tps)
    x1_s = _post_attn(xs, o_s.reshape(rows_s, attn_dim), mod_s[0], g[0, 1], wo, wup[0], wdown[0],
                      tm=rows_s, tiles_per_seq=1)

    y_p, ut_p = _conv_layer(x1_p, mod_p[1], g[1, 0], g[1, 1], gf, wbcx, w_conv[0], wout, wup[1], wdown[1], None,
                            tm=ROW_TILE, tiles_per_seq=tps, period=None, tail=8)
    st = state_conv[0]
    zeros = jnp.zeros((nseq, t - 1, d), F32)
    e1 = jnp.concatenate([st[:, 1:2], zeros], axis=1).reshape(rows_s, d)
    e2 = jnp.concatenate([st[:, 0:2], zeros[:, 1:]], axis=1).reshape(rows_s, d)
    y_s, ut_s = _conv_layer(x1_s, mod_s[1], g[1, 0], g[1, 1], gf, wbcx, w_conv[0], wout, wup[1], wdown[1], (e1, e2),
                            tm=rows_s, tiles_per_seq=1, period=t, tail=rows_s)

    conv_p = ut_p.reshape(batch, tps, 8, d)[:, -1, 8 - (kw - 1):, :]
    conv_s = ut_s.reshape(nseq, t, d)[:, t - (kw - 1):, :]
    return (
        y_p.reshape(batch, seq, d),
        y_s.reshape(nseq, t, d),
        k_p.reshape(1, batch, seq, heads, head_dim),
        v_p.reshape(1, batch, seq, heads, head_dim),
        k_s.reshape(1, nseq, t, heads, head_dim),
        v_s.reshape(1, nseq, t, heads, head_dim),
        conv_p[None],
        conv_s[None],
    )
```

```python
import functools

import numpy as np
import jax
import jax.numpy as jnp
from jax import lax
from jax.experimental import pallas as pl
from jax.experimental.pallas import tpu as pltpu

F32 = jnp.float32
BF16 = jnp.bfloat16

HEAD_DIM = 128
MOBA_BLOCK = 256
MOBA_TOP_K = 3
ROPE_DIM = HEAD_DIM // 4
ROPE_THETA = 500000.0
RMS_EPS = 1e-6
SOFTMAX_SCALE = HEAD_DIM ** -0.5
MASKED = -1e30

LANES = 128
VMEM_LIMIT = 56 * 1024 * 1024
ROW_TILE = 256
ADA_COL_TILE = 1536
GATHER_SLOTS = 4
ATTN_HEADS_PER_STEP = 2
ATTN_CHUNK_BLOCKS = 4

NT_DIMS = (((1,), (1,)), ((), ()))


def _nt_dot(a, b):
    return lax.dot_general(a, b, NT_DIMS, preferred_element_type=F32)


def _split_bf16(x):
    hi = x.astype(BF16)
    lo = (x - hi.astype(F32)).astype(BF16)
    return hi, lo


def _const_spec(shape):
    n = len(shape)
    return pl.BlockSpec(shape, lambda *_: (0,) * n, pipeline_mode=pl.Buffered(1))


def _params(*sem):
    return pltpu.CompilerParams(dimension_semantics=sem, vmem_limit_bytes=VMEM_LIMIT)


def _ada_kernel(c_ref, w_ref, b_ref, o_ref):
    c_hi, c_lo = _split_bf16(c_ref[...])
    w_hi, w_lo = _split_bf16(w_ref[0])
    acc = jnp.dot(c_hi, w_hi, preferred_element_type=F32)
    acc += jnp.dot(c_lo, w_hi, preferred_element_type=F32)
    acc += jnp.dot(c_hi, w_lo, preferred_element_type=F32)
    o_ref[0] = acc + b_ref[0]


def _ada(c_all, w_ada, b_ada):
    depth, d, n6 = w_ada.shape
    rows = c_all.shape[0]
    tn = ADA_COL_TILE
    return pl.pallas_call(
        _ada_kernel,
        grid=(depth, n6 // tn),
        in_specs=[
            pl.BlockSpec((rows, d), lambda l, n: (0, 0)),
            pl.BlockSpec((1, d, tn), lambda l, n: (l, 0, n)),
            pl.BlockSpec((1, 1, tn), lambda l, n: (l, 0, n)),
        ],
        out_specs=pl.BlockSpec((1, rows, tn), lambda l, n: (l, 0, n)),
        out_shape=jax.ShapeDtypeStruct((depth, rows, n6), F32),
        compiler_params=_params("arbitrary", "arbitrary"),
        name="ada",
    )(c_all, w_ada, b_ada.reshape(depth, 1, n6))


def _norm_mod(x, g, shift, scale):
    ms = jnp.mean(x * x, axis=-1, keepdims=True)
    y = x * lax.rsqrt(ms + RMS_EPS) * g
    return y * (1.0 + scale) + shift


def _mlp(h, wup_ref, wdown_ref):
    d, dff = wup_ref.shape
    hb = h.astype(BF16)
    acc = jnp.zeros((h.shape[0], d), F32)
    for c in range(0, dff, d):
        u = jnp.dot(hb, wup_ref[:, c:c + d], preferred_element_type=F32)
        u = jnp.square(jnp.maximum(u, 0.0)).astype(BF16)
        acc += jnp.dot(u, wdown_ref[c:c + d, :], preferred_element_type=F32)
    return acc


def _rope_tables(pos):
    half = ROPE_DIM // 2
    inv = np.float32(ROPE_THETA) ** (-np.arange(0, ROPE_DIM, 2, dtype=np.float32) / np.float32(ROPE_DIM))
    ang = (pos.astype(np.float32)[:, None] * inv.astype(np.float32)[None, :]).astype(np.float32)
    cos = np.cos(ang.astype(np.float64))
    sin = np.sin(ang.astype(np.float64))
    n = pos.shape[0]
    c = np.ones((n, HEAD_DIM), np.float64)
    s1 = np.zeros((n, HEAD_DIM), np.float64)
    s2 = np.zeros((n, HEAD_DIM), np.float64)
    c[:, :half] = cos
    c[:, half:ROPE_DIM] = cos
    s1[:, :half] = -sin
    s2[:, half:ROPE_DIM] = sin
    return tuple(jnp.asarray(t.astype(np.float32)) for t in (c, s1, s2))


def _qkv_kernel(x_ref, mod_ref, g_ref, w_ref, cos_ref, s1_ref, s2_ref, q_ref, k5_ref, v5_ref, kb_ref, vb_ref,
                *ks_ref, attn_dim):
    d = x_ref.shape[1]
    m = mod_ref[0]
    h = _norm_mod(x_ref[...], g_ref[...], m[:, 0:d], m[:, d:2 * d]).astype(BF16)
    cos, s1, s2 = cos_ref[...], s1_ref[...], s2_ref[...]
    tm = h.shape[0]
    heads = attn_dim // HEAD_DIM
    cw = 2 * HEAD_DIM
    for c in range(0, 3 * attn_dim, cw):
        r = jnp.dot(h, w_ref[:, c:c + cw], preferred_element_type=F32)
        if c < 2 * attn_dim:
            rot = []
            for u in range(0, cw, HEAD_DIM):
                z = r[:, u:u + HEAD_DIM]
                z = (z * cos + pltpu.roll(z, HEAD_DIM - ROPE_DIM // 2, 1) * s1
                     + pltpu.roll(z, ROPE_DIM // 2, 1) * s2)
                rot.append(z)
            r = jnp.concatenate(rot, axis=1)
        if c < attn_dim:
            q_ref[:, c:c + cw] = r
            continue
        is_k = c < 2 * attn_dim
        c0 = c - (attn_dim if is_k else 2 * attn_dim)
        (kb_ref if is_k else vb_ref)[:, c0:c0 + cw] = r.astype(BF16)
        for u in range(0, cw, HEAD_DIM):
            head = (c0 + u) // HEAD_DIM
            (k5_ref if is_k else v5_ref)[pl.ds(head, tm, stride=heads), :] = r[:, u:u + HEAD_DIM]
        if is_k and ks_ref:
            nblk = tm // MOBA_BLOCK
            ks_ref[0][0, :, c0:c0 + cw] = jnp.sum(r.reshape(nblk, MOBA_BLOCK, cw), axis=1)


def _qkv(x, mod, g, w, tables, *, tm, tiles_per_seq, with_ksum):
    rows, d = x.shape
    attn_dim = w.shape[1] // 3
    heads = attn_dim // HEAD_DIM
    n_tiles = rows // tm
    mod_rows = mod.shape[1]
    row_spec = pl.BlockSpec((tm, attn_dim), lambda i: (i, 0))
    tok_head_spec = pl.BlockSpec((tm * heads, HEAD_DIM), lambda i: (i, 0))
    tab_spec = pl.BlockSpec((tm, HEAD_DIM), lambda i: (i % tiles_per_seq, 0))
    out_specs = [row_spec, tok_head_spec, tok_head_spec, row_spec, row_spec]
    out_shape = ([jax.ShapeDtypeStruct((rows, attn_dim), F32)]
                 + [jax.ShapeDtypeStruct((rows * heads, HEAD_DIM), F32)] * 2
                 + [jax.ShapeDtypeStruct((rows, attn_dim), BF16)] * 2)
    if with_ksum:
        nblk = tm // MOBA_BLOCK
        out_specs.append(pl.BlockSpec((1, nblk, attn_dim), lambda i: (i, 0, 0)))
        out_shape.append(jax.ShapeDtypeStruct((n_tiles, nblk, attn_dim), F32))
    return pl.pallas_call(
        functools.partial(_qkv_kernel, attn_dim=attn_dim),
        grid=(n_tiles,),
        in_specs=[
            pl.BlockSpec((tm, d), lambda i: (i, 0)),
            pl.BlockSpec((1, mod_rows, mod.shape[2]), lambda i: (i // tiles_per_seq, 0, 0)),
            _const_spec((1, d)),
            _const_spec(w.shape),
            tab_spec, tab_spec, tab_spec,
        ],
        out_specs=out_specs,
        out_shape=out_shape,
        compiler_params=_params("arbitrary"),
        name="qkv",
    )(x, mod, g, w, *tables)


def _attn_prompt_step(i, q_ref, k_ref, v_ref, ks_ref, o_ref, kaug, vaug, eye, s_sc, *, hps, chunk_blocks, side_work):
    tq = q_ref.shape[0]
    seq = k_ref.shape[0]
    nb = seq // MOBA_BLOCK
    ck = chunk_blocks * MOBA_BLOCK
    n_chunk_max = nb // chunk_blocks

    @pl.when(i == 0)
    def _():
        r_id = lax.broadcasted_iota(jnp.int32, (tq, tq), 0)
        c_id = lax.broadcasted_iota(jnp.int32, (tq, tq), 1)
        eye[...] = jnp.where(r_id == c_id, 1.0, 0.0).astype(BF16)
        lane = lax.broadcasted_iota(jnp.int32, (MOBA_BLOCK, LANES), 1)
        ones_col = jnp.where(lane == 0, 1.0, 0.0).astype(BF16)
        for hh in range(hps):
            cols = slice(hh * HEAD_DIM, (hh + 1) * HEAD_DIM)

            def fill(j, carry):
                rows = pl.ds(pl.multiple_of(j * MOBA_BLOCK, MOBA_BLOCK), MOBA_BLOCK)
                kaug[hh, rows, 0:HEAD_DIM] = k_ref[rows, cols]
                kaug[hh, rows, HEAD_DIM:] = jnp.where(lane == j, 1.0, 0.0).astype(BF16)
                vaug[hh, rows, 0:HEAD_DIM] = v_ref[rows, cols]
                vaug[hh, rows, HEAD_DIM:] = ones_col
                return carry
            lax.fori_loop(0, nb, fill, 0)

    own = pl.ds(pl.multiple_of(i * MOBA_BLOCK, MOBA_BLOCK), MOBA_BLOCK)
    head_cols = [slice(hh * HEAD_DIM, (hh + 1) * HEAD_DIM) for hh in range(hps)]

    def select(hh):
        q = q_ref[:, head_cols[hh]]
        qs = (q * (SOFTMAX_SCALE * np.log2(np.e))).astype(BF16)
        kmean = (ks_ref[0][:, head_cols[hh]] * (1.0 / MOBA_BLOCK)).astype(BF16)
        gate = _nt_dot(kmean, q.astype(BF16))
        blk = lax.broadcasted_iota(jnp.int32, (nb, tq), 0)
        blk_f = blk.astype(F32)
        gate = jnp.where(blk < i, gate, -jnp.inf)
        sel = jnp.zeros((nb, tq), F32)
        for _ in range(MOBA_TOP_K):
            mx = jnp.max(gate, axis=0, keepdims=True)
            first = jnp.min(jnp.where(gate == mx, blk_f, float(nb)), axis=0, keepdims=True)
            hit = blk_f == first
            sel = jnp.where(hit & (mx > -jnp.inf), 1.0, sel)
            gate = jnp.where(hit, -jnp.inf, gate)
        sel_pad = jnp.concatenate([sel, jnp.zeros((LANES - nb, tq), F32)], axis=0).astype(BF16)
        sel_q = _nt_dot(eye[...], sel_pad)
        bias = jnp.where(sel_q > 0.5, 0.0, MASKED).astype(BF16)
        return jnp.concatenate([qs, bias], axis=1)

    def variant(n_chunks):
        heads_ = range(hps)
        qa = [select(hh) for hh in heads_]
        r_id = lax.broadcasted_iota(jnp.int32, (tq, MOBA_BLOCK), 0)
        c_id = lax.broadcasted_iota(jnp.int32, (tq, MOBA_BLOCK), 1)
        s_own = [jnp.where(c_id <= r_id, _nt_dot(qa[hh][:, 0:HEAD_DIM], kaug[hh, own, 0:HEAD_DIM]), MASKED)
                 for hh in heads_]
        run = [jnp.maximum(s[:, 0:LANES], s[:, LANES:]) for s in s_own]
        for c in range(n_chunks):
            for hh in heads_:
                s_c = _nt_dot(qa[hh], kaug[hh, c * ck:(c + 1) * ck, :])
                s_sc[hh, :, c * ck:(c + 1) * ck] = s_c
                for u in range(0, ck, LANES):
                    run[hh] = jnp.maximum(run[hh], s_c[:, u:u + LANES])
        m = [jnp.max(r, axis=1, keepdims=True) for r in run]
        acc = [jnp.dot(jnp.exp2(s_own[hh] - m[hh]).astype(BF16), vaug[hh, own, :], preferred_element_type=F32)
               for hh in heads_]
        for c in range(n_chunks):
            for hh in heads_:
                p_c = jnp.exp2(s_sc[hh, :, c * ck:(c + 1) * ck] - m[hh]).astype(BF16)
                acc[hh] += jnp.dot(p_c, vaug[hh, c * ck:(c + 1) * ck, :], preferred_element_type=F32)
        for hh in heads_:
            o_ref[:, head_cols[hh]] = (acc[hh][:, 0:HEAD_DIM] / acc[hh][:, HEAD_DIM:HEAD_DIM + 1]).astype(o_ref.dtype)

    n_needed = (i + chunk_blocks - 1) // chunk_blocks
    for n in range(n_chunk_max + 1):
        @pl.when(n_needed == n)
        def _(n=n):
            side_work()
            variant(n)


def _scan_queries(q_ref):
    t = q_ref.shape[1]
    qs = q_ref[0] * SOFTMAX_SCALE
    q_hi = qs.astype(BF16).astype(F32)
    q_lo = (qs - q_hi).astype(BF16).astype(F32)
    return jnp.concatenate([q_hi, q_lo, jnp.zeros((16 - 2 * t, qs.shape[1]), F32)], axis=0).astype(BF16)


def _cache_scan_pages(p, q_ref, page_refs, l_ref, ksum_sc, *, heads, page):
    pps = len(page_refs)
    t = q_ref.shape[1]
    ppb = MOBA_BLOCK // page
    bps = pps // ppb
    keys = page * heads
    assert 2 * t == 8
    qs = q_ref[0] * SOFTMAX_SCALE
    q_rows = jnp.concatenate([qs[:, h * HEAD_DIM:(h + 1) * HEAD_DIM] for h in range(heads)], axis=0).astype(BF16)
    row_head = lax.broadcasted_iota(jnp.int32, (heads * t, keys), 0) // t
    key_head = lax.broadcasted_iota(jnp.int32, (heads * t, keys), 1) % heads
    same_head = row_head == key_head
    sums = []
    for u in range(pps):
        pg = page_refs[u][...]
        sums.append(jnp.sum(pg.reshape(page, heads, HEAD_DIM), axis=0))
        lt = jnp.where(same_head, _nt_dot(q_rows, pg.astype(BF16)), 0.0)
        fold = functools.reduce(lambda x, y: x + y, [lt[g:g + 8] for g in range(0, heads * t, 8)])
        l_ref[0, :, u * keys:(u + 1) * keys] = (fold + pltpu.roll(fold, t, 0))[0:t]
    for b in range(bps):
        blk_sum = functools.reduce(lambda x, y: x + y, sums[b * ppb:(b + 1) * ppb])
        ksum_sc[pl.ds(pl.multiple_of((p * bps + b) * heads, heads), heads), :] = blk_sum


def _cache_scan_gate(q_ref, ti_ref, ksum_sc, *, heads):
    t = q_ref.shape[1]
    q16 = _scan_queries(q_ref)
    nb = ksum_sc.shape[0] // heads
    lane = lax.broadcasted_iota(jnp.int32, (16, LANES), 1)
    lane_f = lane.astype(F32)
    for h in range(heads):
        kmean = ksum_sc[pl.ds(h, nb, stride=heads), :] * (1.0 / MOBA_BLOCK)
        km = jnp.concatenate([kmean, jnp.zeros((LANES - nb, HEAD_DIM), F32)], axis=0)
        km_hi, km_lo = _split_bf16(km)
        q_h = q16[:, h * HEAD_DIM:(h + 1) * HEAD_DIM]
        r1 = _nt_dot(q_h, km_hi)
        r2 = _nt_dot(q_h, km_lo)
        gate = r1 + pltpu.roll(r1, 16 - t, 0) + r2
        gate = jnp.where(lane < nb, gate, -jnp.inf)
        out = jnp.zeros((16, LANES), jnp.int32)
        for kk in range(MOBA_TOP_K):
            mx = jnp.max(gate, axis=1, keepdims=True)
            first = jnp.min(jnp.where(gate == mx, lane_f, float(LANES)), axis=1, keepdims=True)
            idx = jnp.where(mx > -jnp.inf, first, -1.0).astype(jnp.int32)
            out = jnp.where(lane == kk, idx, out)
            gate = jnp.where(lane_f == first, -jnp.inf, gate)
        ti_ref[0, h] = out[0:8]


def _attn_scan_kernel(pt_ref, q_ref, k_ref, v_ref, ks_ref, qs_ref, *rest, pps, hps, heads, page, scan_steps):
    page_refs = rest[:pps]
    o_ref, l_ref, ti_ref, kaug, vaug, eye, s_sc, ksum_sc = rest[pps:]
    i = pl.program_id(2)
    flat = (pl.program_id(0) * pl.num_programs(1) + pl.program_id(1)) * pl.num_programs(2) + i
    p = flat % scan_steps

    def scan_pages():
        _cache_scan_pages(p, qs_ref, page_refs, l_ref, ksum_sc, heads=heads, page=page)

    _attn_prompt_step(i, q_ref, k_ref, v_ref, ks_ref, o_ref, kaug, vaug, eye, s_sc,
                      hps=hps, chunk_blocks=ATTN_CHUNK_BLOCKS, side_work=scan_pages)

    @pl.when(p == scan_steps - 1)
    def _():
        _cache_scan_gate(qs_ref, ti_ref, ksum_sc, heads=heads)


def _attn_scan(page_table, q, k, v, ksum, q_s, cache, *, batch, seq, page):
    rows, attn_dim = q.shape
    heads = attn_dim // HEAD_DIM
    tq = MOBA_BLOCK
    nq = seq // tq
    nb = seq // MOBA_BLOCK
    hps = ATTN_HEADS_PER_STEP
    n_hp = heads // hps
    assert nb <= LANES and nb % ATTN_CHUNK_BLOCKS == 0 and heads % hps == 0

    nseq, n_pages = page_table.shape
    t = q_s.shape[1]
    past = n_pages * page
    nb_past = past // MOBA_BLOCK
    n_steps = batch * n_hp * nq
    assert n_steps % nseq == 0
    scan_steps = n_steps // nseq
    assert n_pages % scan_steps == 0
    pps = n_pages // scan_steps
    assert (pps * page) % MOBA_BLOCK == 0 and MOBA_BLOCK % page == 0
    assert nb_past <= LANES and 2 * t <= 8 and (pps * page // MOBA_BLOCK) % 8 == 0

    def flat(b, h, i):
        return (b * n_hp + h) * nq + i

    q_spec = pl.BlockSpec((tq, hps * HEAD_DIM), lambda b, h, i, pt: (b * nq + i, h))
    kv_spec = pl.BlockSpec((seq, hps * HEAD_DIM), lambda b, h, i, pt: (b, h))
    page_specs = [
        pl.BlockSpec((page * heads, HEAD_DIM),
                     functools.partial(lambda b, h, i, pt, u: (pt[flat(b, h, i) // scan_steps,
                                                                  (flat(b, h, i) % scan_steps) * pps + u], 0), u=u))
        for u in range(pps)
    ]
    grid_spec = pltpu.PrefetchScalarGridSpec(
        num_scalar_prefetch=1,
        grid=(batch, n_hp, nq),
        in_specs=[q_spec, kv_spec, kv_spec,
                  pl.BlockSpec((1, nb, hps * HEAD_DIM), lambda b, h, i, pt: (b, 0, h)),
                  pl.BlockSpec((1, t, attn_dim), lambda b, h, i, pt: (flat(b, h, i) // scan_steps, 0, 0))]
        + page_specs,
        out_specs=[
            q_spec,
            pl.BlockSpec((1, t, pps * page * heads),
                         lambda b, h, i, pt: (flat(b, h, i) // scan_steps, 0, flat(b, h, i) % scan_steps)),
            pl.BlockSpec((1, heads, 8, LANES), lambda b, h, i, pt: (flat(b, h, i) // scan_steps, 0, 0, 0)),
        ],
        scratch_shapes=[
            pltpu.VMEM((hps, seq, 2 * HEAD_DIM), BF16),
            pltpu.VMEM((hps, seq, 2 * HEAD_DIM), BF16),
            pltpu.VMEM((tq, tq), BF16),
            pltpu.VMEM((hps, tq, seq), F32),
            pltpu.VMEM((nb_past * heads, HEAD_DIM), F32),
        ],
    )
    return pl.pallas_call(
        functools.partial(_attn_scan_kernel, pps=pps, hps=hps, heads=heads, page=page, scan_steps=scan_steps),
        grid_spec=grid_spec,
        out_shape=[
            jax.ShapeDtypeStruct((rows, attn_dim), BF16),
            jax.ShapeDtypeStruct((nseq, t, past * heads), F32),
            jax.ShapeDtypeStruct((nseq, heads, 8, LANES), jnp.int32),
        ],
        compiler_params=_params("arbitrary", "arbitrary", "arbitrary"),
        name="attn_scan",
    )(page_table, q, k, v, ksum.reshape(batch, nb, attn_dim), q_s, *([cache] * pps))


def _attn_sample_kernel(pt_ref, ti_ref, q_ref, kn_ref, vn_ref, pick_ref, l_hbm, cv_hbm, o_ref, vbuf, lbuf, sem,
                        *, heads):
    s = pl.program_id(0)
    h = pl.program_id(1)
    n = s * heads + h
    total = pl.num_programs(0) * heads
    t = q_ref.shape[1]
    nsel = t * MOBA_TOP_K
    page = cv_hbm.shape[1]
    ppb = MOBA_BLOCK // page
    bw = MOBA_BLOCK * heads

    def copies(s_, h_, slot_):
        out = []
        for c in range(nsel):
            blk = jnp.maximum(ti_ref[s_, h_ * nsel + c], 0)
            out.append(pltpu.make_async_copy(
                l_hbm.at[s_, :, pl.ds(pl.multiple_of(blk * bw, bw), bw)], lbuf.at[slot_, c], sem.at[slot_]))
            for e in range(ppb):
                pg = pt_ref[s_, blk * ppb + e]
                out.append(pltpu.make_async_copy(
                    cv_hbm.at[pg, :, h_, :], vbuf.at[slot_, c * ppb + e], sem.at[slot_]))
        return out

    n_slots = vbuf.shape[0]
    slot = n % n_slots

    def start_step(m):
        for cp in copies(m // heads, m % heads, m % n_slots):
            cp.start()

    @pl.when(n == 0)
    def _():
        for ahead in range(n_slots - 1):
            start_step(ahead)

    @pl.when(n + n_slots - 1 < total)
    def _():
        start_step(n + n_slots - 1)

    for cp in copies(s, h, slot):
        cp.wait()

    qs = q_ref[0] * SOFTMAX_SCALE
    kn = kn_ref[0]
    vn = vn_ref[0]
    n_rows = 16
    assert nsel <= n_rows
    rows = []
    for c in range(nsel):
        r = c // MOBA_TOP_K
        ok = ti_ref[s, h * nsel + c] >= 0
        rows.append(jnp.where(ok, lbuf[slot, c, r:r + 1, :], MASKED))
    lm = jnp.concatenate(rows + [jnp.full((n_rows - nsel, bw), MASKED, F32)], axis=0)
    lane_head = lax.broadcasted_iota(jnp.int32, (n_rows, bw), 1) % heads
    lm = jnp.where(lane_head == h, lm, MASKED)
    row_o = lax.broadcasted_iota(jnp.int32, (n_rows, LANES), 0)
    col_o = lax.broadcasted_iota(jnp.int32, (n_rows, LANES), 1)
    own = jnp.full((n_rows, LANES), MASKED, F32)
    for r in range(t):
        for j in range(r + 1):
            d_rj = jnp.sum(qs[r:r + 1, :] * kn[j:j + 1, :], axis=1, keepdims=True)
            own = jnp.where((row_o == MOBA_TOP_K * r) & (col_o == j), d_rj, own)
    row_id = lax.broadcasted_iota(jnp.int32, (n_rows, 1), 0)
    groups = [(row_id >= MOBA_TOP_K * r) & (row_id < MOBA_TOP_K * (r + 1)) for r in range(t)]

    def per_query(x, reduce_fn, neutral):
        out = jnp.zeros((n_rows, 1), F32)
        for g in groups:
            out = jnp.where(g, reduce_fn(jnp.where(g, x, neutral), axis=0, keepdims=True), out)
        return out

    m = per_query(jnp.maximum(jnp.max(lm, axis=1, keepdims=True), jnp.max(own, axis=1, keepdims=True)),
                  jnp.max, -jnp.inf)
    p = jnp.exp(lm - m)
    p_own = jnp.exp(own - m)
    l = per_query(jnp.sum(p, axis=1, keepdims=True) + jnp.sum(p_own, axis=1, keepdims=True), jnp.sum, 0.0)
    pc = jnp.dot(p.astype(BF16), pick_ref[...], preferred_element_type=F32)
    row8 = lax.broadcasted_iota(jnp.int32, (8, MOBA_BLOCK), 0)
    p_big = jnp.concatenate(
        [jnp.where(row8 == c // MOBA_TOP_K, jnp.broadcast_to(pc[c:c + 1, :], (8, MOBA_BLOCK)), 0.0)
         for c in range(nsel)], axis=1).astype(BF16)
    vsel = vbuf[slot].reshape(nsel * MOBA_BLOCK, HEAD_DIM).astype(BF16)
    pv = jnp.dot(p_big, vsel, preferred_element_type=F32)[0:t]
    first = lambda x: jnp.concatenate([x[MOBA_TOP_K * r:MOBA_TOP_K * r + 1] for r in range(t)], axis=0)
    p_new = first(p_own)
    for j in range(t):
        pv += p_new[:, j:j + 1] * vn[j:j + 1, :]
    o_ref[0] = pv / first(l)


def _attn_sample(page_table, top_idx, q_s, k_s, v_s, logits, cache_v):
    nseq, t, attn_dim = q_s.shape
    heads = attn_dim // HEAD_DIM
    page = cache_v.shape[1]
    nsel = t * MOBA_TOP_K
    ppb = MOBA_BLOCK // page
    bw = MOBA_BLOCK * heads
    pick = jnp.asarray(np.arange(bw)[:, None] // heads == np.arange(MOBA_BLOCK)[None, :], BF16)
    new_spec = pl.BlockSpec((1, t, HEAD_DIM), lambda s, h, pt, ti: (s, 0, h))
    grid_spec = pltpu.PrefetchScalarGridSpec(
        num_scalar_prefetch=2,
        grid=(nseq, heads),
        in_specs=[new_spec, new_spec, new_spec, _const_spec(pick.shape),
                  pl.BlockSpec(memory_space=pl.ANY), pl.BlockSpec(memory_space=pl.ANY)],
        out_specs=new_spec,
        scratch_shapes=[
            pltpu.VMEM((GATHER_SLOTS, nsel * ppb, page, HEAD_DIM), F32),
            pltpu.VMEM((GATHER_SLOTS, nsel, t, bw), F32),
            pltpu.SemaphoreType.DMA((GATHER_SLOTS,)),
        ],
    )
    return pl.pallas_call(
        functools.partial(_attn_sample_kernel, heads=heads),
        grid_spec=grid_spec,
        out_shape=jax.ShapeDtypeStruct((nseq, t, attn_dim), F32),
        compiler_params=_params("arbitrary", "arbitrary"),
        name="attn_sample",
    )(page_table, top_idx, q_s, k_s, v_s, pick, logits, cache_v)


def _post_attn_kernel(x_ref, o_ref, mod_ref, g_ref, wo_ref, wup_ref, wdown_ref, y_ref):
    d = x_ref.shape[1]
    m = mod_ref[0]
    a = jnp.dot(o_ref[...].astype(BF16), wo_ref[...], preferred_element_type=F32)
    x1 = x_ref[...] + m[:, 2 * d:3 * d] * a
    h = _norm_mod(x1, g_ref[...], m[:, 3 * d:4 * d], m[:, 4 * d:5 * d])
    y_ref[...] = x1 + m[:, 5 * d:6 * d] * _mlp(h, wup_ref, wdown_ref)


def _post_attn(x, o, mod, g, wo, wup, wdown, *, tm, tiles_per_seq):
    rows, d = x.shape
    row_spec = pl.BlockSpec((tm, d), lambda i: (i, 0))
    return pl.pallas_call(
        _post_attn_kernel,
        grid=(rows // tm,),
        in_specs=[
            row_spec,
            pl.BlockSpec((tm, o.shape[1]), lambda i: (i, 0)),
            pl.BlockSpec((1, mod.shape[1], mod.shape[2]), lambda i: (i // tiles_per_seq, 0, 0)),
            _const_spec((1, d)),
            _const_spec(wo.shape), _const_spec(wup.shape), _const_spec(wdown.shape),
        ],
        out_specs=row_spec,
        out_shape=jax.ShapeDtypeStruct((rows, d), F32),
        compiler_params=_params("arbitrary"),
        name="post_attn",
    )(x, o, mod, g, wo, wup, wdown)


def _conv_layer_kernel(x_ref, mod_ref, g1_ref, g2_ref, gf_ref, wbcx_ref, wconv_ref, wout_ref, wup_ref, wdown_ref,
                       *rest, tiles_per_seq, period, tail):
    if period is None:
        y_ref, ut_ref, ext = rest
    else:
        e1_ref, e2_ref, y_ref, ut_ref, ext = rest
    i = pl.program_id(0)
    tm, d = x_ref.shape
    halo = 8
    m = mod_ref[0]
    x = x_ref[...]
    h = _norm_mod(x, g1_ref[...], m[:, 0:d], m[:, d:2 * d]).astype(BF16)
    bg = jnp.dot(h, wbcx_ref[:, 0:d], preferred_element_type=F32)
    cg = jnp.dot(h, wbcx_ref[:, d:2 * d], preferred_element_type=F32)
    xin = jnp.dot(h, wbcx_ref[:, 2 * d:3 * d], preferred_element_type=F32)
    u = cg * xin

    @pl.when(i % tiles_per_seq == 0)
    def _():
        ext[0:halo, :] = jnp.zeros((halo, d), F32)

    ext[halo:halo + tm, :] = u
    prev1 = ext[halo - 1:halo - 1 + tm, :]
    prev2 = ext[halo - 2:halo - 2 + tm, :]
    if period is not None:
        tpos = lax.broadcasted_iota(jnp.int32, (tm, d), 0) % period
        prev1 = jnp.where(tpos < 1, e1_ref[...], prev1)
        prev2 = jnp.where(tpos < 2, e2_ref[...], prev2)
    w = wconv_ref[...]
    conv = w[0:1, :] * prev2 + w[1:2, :] * prev1 + w[2:3, :] * u
    ext[0:halo, :] = ext[tm:tm + halo, :]
    ut_ref[0] = u[tm - tail:, :]

    y = jnp.dot((bg * conv).astype(BF16), wout_ref[...], preferred_element_type=F32)
    x1 = x + m[:, 2 * d:3 * d] * y
    h2 = _norm_mod(x1, g2_ref[...], m[:, 3 * d:4 * d], m[:, 4 * d:5 * d])
    x2 = x1 + m[:, 5 * d:6 * d] * _mlp(h2, wup_ref, wdown_ref)
    ms = jnp.mean(x2 * x2, axis=-1, keepdims=True)
    y_ref[...] = x2 * lax.rsqrt(ms + RMS_EPS) * gf_ref[...]


def _conv_layer(x, mod, g1, g2, gf, wbcx, wconv, wout, wup, wdown, state_rows, *, tm, tiles_per_seq, period, tail):
    rows, d = x.shape
    n_tiles = rows // tm
    row_spec = pl.BlockSpec((tm, d), lambda i: (i, 0))
    in_specs = [
        row_spec,
        pl.BlockSpec((1, mod.shape[1], mod.shape[2]), lambda i: (i // tiles_per_seq, 0, 0)),
        _const_spec((1, d)), _const_spec((1, d)), _const_spec((1, d)),
        _const_spec(wbcx.shape), _const_spec(wconv.shape), _const_spec(wout.shape),
        _const_spec(wup.shape), _const_spec(wdown.shape),
    ]
    args = [x, mod, g1, g2, gf, wbcx, wconv, wout, wup, wdown]
    if period is not None:
        in_specs += [row_spec, row_spec]
        args += list(state_rows)
    return pl.pallas_call(
        functools.partial(_conv_layer_kernel, tiles_per_seq=tiles_per_seq, period=period, tail=tail),
        grid=(n_tiles,),
        in_specs=in_specs,
        out_specs=[row_spec, pl.BlockSpec((1, tail, d), lambda i: (i, 0, 0))],
        out_shape=[jax.ShapeDtypeStruct((rows, d), F32), jax.ShapeDtypeStruct((n_tiles, tail, d), F32)],
        scratch_shapes=[pltpu.VMEM((tm + 8, d), F32)],
        compiler_params=_params("arbitrary"),
        name="conv_layer",
    )(*args)


def kernel(x_prompt, x_sample, cache_k, cache_v, state_conv, page_table, c_prompt, c_sample, w_ada, b_ada, g_norm,
           w_qkv, w_o, w_bcx, w_conv, w_out, w_up, w_down, g_final):
    batch, seq, d = x_prompt.shape
    nseq, t, _ = x_sample.shape
    depth = w_ada.shape[0]
    n_layers_attn, pool, page, heads, head_dim = cache_k.shape
    attn_dim = heads * head_dim
    n_pages = page_table.shape[1]
    past = n_pages * page
    kw = state_conv.shape[2] + 1
    assert depth == 2 and n_layers_attn == 1 and state_conv.shape[0] == 1, "layer 0 attention, layer 1 convolution"
    assert head_dim == HEAD_DIM and kw == 3 and t >= kw - 1
    assert seq % ROW_TILE == 0 and ROW_TILE % MOBA_BLOCK == 0 and past % MOBA_BLOCK == 0 and t <= MOBA_BLOCK
    rows_p, rows_s = batch * seq, nseq * t
    assert rows_s % 8 == 0

    n_c = batch + nseq
    pad = -n_c % 8
    c_all = jnp.concatenate([c_prompt, c_sample, jnp.zeros((pad, d), F32)], axis=0)
    mod = _ada(c_all, w_ada, b_ada)
    mod_p = mod[:, :batch, None, :]
    mod_s = jnp.repeat(mod[:, batch:n_c], t, axis=1)[:, None]

    bf = lambda a: a.astype(BF16)
    wqkv, wo, wbcx, wout = bf(w_qkv[0]), bf(w_o[0]), bf(w_bcx[0]), bf(w_out[0])
    wup, wdown = bf(w_up), bf(w_down)
    g = g_norm.reshape(depth, 2, 1, d)
    gf = g_final.reshape(1, d)

    xp = x_prompt.reshape(rows_p, d)
    xs = x_sample.reshape(rows_s, d)
    tps = seq // ROW_TILE
    tab_p = _rope_tables(np.arange(seq))
    tab_s = _rope_tables(past + (np.arange(rows_s) % t))

    q_p, k_p, v_p, kb_p, vb_p, ksum = _qkv(xp, mod_p[0], g[0, 0], wqkv, tab_p, tm=ROW_TILE, tiles_per_seq=tps,
                                           with_ksum=True)
    q_s, k_s, v_s, kb_s, vb_s = _qkv(xs, mod_s[0], g[0, 0], wqkv, tab_s, tm=rows_s, tiles_per_seq=1,
                                     with_ksum=False)
    q_s3, kn_s3, vn_s3 = (a.astype(F32).reshape(nseq, t, attn_dim) for a in (q_s, kb_s, vb_s))
    o_p, logits, top = _attn_scan(page_table, q_p, kb_p, vb_p, ksum, q_s3,
                                  cache_k.reshape(pool * page * heads, head_dim), batch=batch, seq=seq, page=page)
    top_idx = top[:, :, :t, :MOBA_TOP_K].reshape(nseq, heads * t * MOBA_TOP_K)
    o_s = _attn_sample(page_table, top_idx, q_s3, kn_s3, vn_s3, logits,
                       cache_v.reshape(pool, page, heads, head_dim))

    x1_p = _post_attn(xp, o_p, mod_p[0], g[0, 1], wo, wup[0], wdown[0], tm=ROW_TILE, tiles_per_seq=tps)
    x1_s = _post_attn(xs, o_s.reshape(rows_s, attn_dim), mod_s[0], g[0, 1], wo, wup[0], wdown[0],
                      tm=rows_s, tiles_per_seq=1)

    y_p, ut_p = _conv_layer(x1_p, mod_p[1], g[1, 0], g[1, 1], gf, wbcx, w_conv[0], wout, wup[1], wdown[1], None,
                            tm=ROW_TILE, tiles_per_seq=tps, period=None, tail=8)
    st = state_conv[0]
    zeros = jnp.zeros((nseq, t - 1, d), F32)
    e1 = jnp.concatenate([st[:, 1:2], zeros], axis=1).reshape(rows_s, d)
    e2 = jnp.concatenate([st[:, 0:2], zeros[:, 1:]], axis=1).reshape(rows_s, d)
    y_s, ut_s = _conv_layer(x1_s, mod_s[1], g[1, 0], g[1, 1], gf, wbcx, w_conv[0], wout, wup[1], wdown[1], (e1, e2),
                            tm=rows_s, tiles_per_seq=1, period=t, tail=rows_s)

    conv_p = ut_p.reshape(batch, tps, 8, d)[:, -1, 8 - (kw - 1):, :]
    conv_s = ut_s.reshape(nseq, t, d)[:, t - (kw - 1):, :]
    return (
        y_p.reshape(batch, seq, d),
        y_s.reshape(nseq, t, d),
        k_p.reshape(1, batch, seq, heads, head_dim),
        v_p.reshape(1, batch, seq, heads, head_dim),
        k_s.reshape(1, nseq, t, heads, head_dim),
        v_s.reshape(1, nseq, t, heads, head_dim),
        conv_p[None],
        conv_s[None],
    )
```

```python
import functools

import numpy as np
import jax
import jax.numpy as jnp
from jax import lax
from jax.experimental import pallas as pl
from jax.experimental.pallas import tpu as pltpu

F32 = jnp.float32
BF16 = jnp.bfloat16

HEAD_DIM = 128
MOBA_BLOCK = 256
MOBA_TOP_K = 3
ROPE_DIM = HEAD_DIM // 4
ROPE_THETA = 500000.0
RMS_EPS = 1e-6
SOFTMAX_SCALE = HEAD_DIM ** -0.5
MASKED = -1e30

LANES = 128
VMEM_LIMIT = 56 * 1024 * 1024
ROW_TILE = 256
ADA_COL_TILE = 1536
GATHER_SLOTS = 4
ATTN_HEADS_PER_STEP = 2
ATTN_CHUNK_BLOCKS = 4

NT_DIMS = (((1,), (1,)), ((), ()))


def _nt_dot(a, b):
    return lax.dot_general(a, b, NT_DIMS, preferred_element_type=F32)


def _split_bf16(x):
    hi = x.astype(BF16)
    lo = (x - hi.astype(F32)).astype(BF16)
    return hi, lo


def _const_spec(shape):
    n = len(shape)
    return pl.BlockSpec(shape, lambda *_: (0,) * n, pipeline_mode=pl.Buffered(1))


def _params(*sem):
    return pltpu.CompilerParams(dimension_semantics=sem, vmem_limit_bytes=VMEM_LIMIT)


def _ada_kernel(c_ref, w_ref, b_ref, o_ref):
    c_hi, c_lo = _split_bf16(c_ref[...])
    w_hi, w_lo = _split_bf16(w_ref[0])
    acc = jnp.dot(c_hi, w_hi, preferred_element_type=F32)
    acc += jnp.dot(c_lo, w_hi, preferred_element_type=F32)
    acc += jnp.dot(c_hi, w_lo, preferred_element_type=F32)
    o_ref[0] = acc + b_ref[0]


def _ada(c_all, w_ada, b_ada):
    depth, d, n6 = w_ada.shape
    rows = c_all.shape[0]
    tn = ADA_COL_TILE
    return pl.pallas_call(
        _ada_kernel,
        grid=(depth, n6 // tn),
        in_specs=[
            pl.BlockSpec((rows, d), lambda l, n: (0, 0)),
            pl.BlockSpec((1, d, tn), lambda l, n: (l, 0, n)),
            pl.BlockSpec((1, 1, tn), lambda l, n: (l, 0, n)),
        ],
        out_specs=pl.BlockSpec((1, rows, tn), lambda l, n: (l, 0, n)),
        out_shape=jax.ShapeDtypeStruct((depth, rows, n6), F32),
        compiler_params=_params("arbitrary", "arbitrary"),
        name="ada",
    )(c_all, w_ada, b_ada.reshape(depth, 1, n6))


def _norm_mod(x, g, shift, scale):
    ms = jnp.mean(x * x, axis=-1, keepdims=True)
    y = x * lax.rsqrt(ms + RMS_EPS) * g
    return y * (1.0 + scale) + shift


def _mlp(h, wup_ref, wdown_ref):
    d, dff = wup_ref.shape
    hb = h.astype(BF16)
    acc = jnp.zeros((h.shape[0], d), F32)
    for c in range(0, dff, d):
        u = jnp.dot(hb, wup_ref[:, c:c + d], preferred_element_type=F32)
        u = jnp.square(jnp.maximum(u, 0.0)).astype(BF16)
        acc += jnp.dot(u, wdown_ref[c:c + d, :], preferred_element_type=F32)
    return acc


def _rope_tables(pos):
    half = ROPE_DIM // 2
    inv = np.float32(ROPE_THETA) ** (-np.arange(0, ROPE_DIM, 2, dtype=np.float32) / np.float32(ROPE_DIM))
    ang = (pos.astype(np.float32)[:, None] * inv.astype(np.float32)[None, :]).astype(np.float32)
    cos = np.cos(ang.astype(np.float64))
    sin = np.sin(ang.astype(np.float64))
    n = pos.shape[0]
    c = np.ones((n, HEAD_DIM), np.float64)
    s1 = np.zeros((n, HEAD_DIM), np.float64)
    s2 = np.zeros((n, HEAD_DIM), np.float64)
    c[:, :half] = cos
    c[:, half:ROPE_DIM] = cos
    s1[:, :half] = -sin
    s2[:, half:ROPE_DIM] = sin
    return tuple(jnp.asarray(t.astype(np.float32)) for t in (c, s1, s2))


def _qkv_kernel(x_ref, mod_ref, g_ref, w_ref, cos_ref, s1_ref, s2_ref, q_ref, k5_ref, v5_ref, kb_ref, vb_ref,
                *ks_ref, attn_dim):
    d = x_ref.shape[1]
    m = mod_ref[0]
    h = _norm_mod(x_ref[...], g_ref[...], m[:, 0:d], m[:, d:2 * d]).astype(BF16)
    cos, s1, s2 = cos_ref[...], s1_ref[...], s2_ref[...]
    tm = h.shape[0]
    heads = attn_dim // HEAD_DIM
    cw = 2 * HEAD_DIM
    for c in range(0, 3 * attn_dim, cw):
        r = jnp.dot(h, w_ref[:, c:c + cw], preferred_element_type=F32)
        if c < 2 * attn_dim:
            rot = []
            for u in range(0, cw, HEAD_DIM):
                z = r[:, u:u + HEAD_DIM]
                z = (z * cos + pltpu.roll(z, HEAD_DIM - ROPE_DIM // 2, 1) * s1
                     + pltpu.roll(z, ROPE_DIM // 2, 1) * s2)
                rot.append(z)
            r = jnp.concatenate(rot, axis=1)
        if c < attn_dim:
            q_ref[:, c:c + cw] = r
            continue
        is_k = c < 2 * attn_dim
        c0 = c - (attn_dim if is_k else 2 * attn_dim)
        (kb_ref if is_k else vb_ref)[:, c0:c0 + cw] = r.astype(BF16)
        for u in range(0, cw, HEAD_DIM):
            head = (c0 + u) // HEAD_DIM
            (k5_ref if is_k else v5_ref)[pl.ds(head, tm, stride=heads), :] = r[:, u:u + HEAD_DIM]
        if is_k and ks_ref:
            nblk = tm // MOBA_BLOCK
            ks_ref[0][0, :, c0:c0 + cw] = jnp.sum(r.reshape(nblk, MOBA_BLOCK, cw), axis=1)


def _qkv(x, mod, g, w, tables, *, tm, tiles_per_seq, with_ksum):
    rows, d = x.shape
    attn_dim = w.shape[1] // 3
    heads = attn_dim // HEAD_DIM
    n_tiles = rows // tm
    mod_rows = mod.shape[1]
    row_spec = pl.BlockSpec((tm, attn_dim), lambda i: (i, 0))
    tok_head_spec = pl.BlockSpec((tm * heads, HEAD_DIM), lambda i: (i, 0))
    tab_spec = pl.BlockSpec((tm, HEAD_DIM), lambda i: (i % tiles_per_seq, 0))
    out_specs = [row_spec, tok_head_spec, tok_head_spec, row_spec, row_spec]
    out_shape = ([jax.ShapeDtypeStruct((rows, attn_dim), F32)]
                 + [jax.ShapeDtypeStruct((rows * heads, HEAD_DIM), F32)] * 2
                 + [jax.ShapeDtypeStruct((rows, attn_dim), BF16)] * 2)
    if with_ksum:
        nblk = tm // MOBA_BLOCK
        out_specs.append(pl.BlockSpec((1, nblk, attn_dim), lambda i: (i, 0, 0)))
        out_shape.append(jax.ShapeDtypeStruct((n_tiles, nblk, attn_dim), F32))
    return pl.pallas_call(
        functools.partial(_qkv_kernel, attn_dim=attn_dim),
        grid=(n_tiles,),
        in_specs=[
            pl.BlockSpec((tm, d), lambda i: (i, 0)),
            pl.BlockSpec((1, mod_rows, mod.shape[2]), lambda i: (i // tiles_per_seq, 0, 0)),
            _const_spec((1, d)),
            _const_spec(w.shape),
            tab_spec, tab_spec, tab_spec,
        ],
        out_specs=out_specs,
        out_shape=out_shape,
        compiler_params=_params("arbitrary"),
        name="qkv",
    )(x, mod, g, w, *tables)


def _attn_prompt_step(i, q_ref, k_ref, v_ref, ks_ref, o_ref, kaug, vaug, eye, s_sc, *, hps, chunk_blocks, side_work):
    tq = q_ref.shape[0]
    seq = k_ref.shape[0]
    nb = seq // MOBA_BLOCK
    ck = chunk_blocks * MOBA_BLOCK
    n_chunk_max = nb // chunk_blocks

    @pl.when(i == 0)
    def _():
        r_id = lax.broadcasted_iota(jnp.int32, (tq, tq), 0)
        c_id = lax.broadcasted_iota(jnp.int32, (tq, tq), 1)
        eye[...] = jnp.where(r_id == c_id, 1.0, 0.0).astype(BF16)
        lane = lax.broadcasted_iota(jnp.int32, (MOBA_BLOCK, LANES), 1)
        ones_col = jnp.where(lane == 0, 1.0, 0.0).astype(BF16)
        for hh in range(hps):
            cols = slice(hh * HEAD_DIM, (hh + 1) * HEAD_DIM)

            def fill(j, carry):
                rows = pl.ds(pl.multiple_of(j * MOBA_BLOCK, MOBA_BLOCK), MOBA_BLOCK)
                kaug[hh, rows, 0:HEAD_DIM] = k_ref[rows, cols]
                kaug[hh, rows, HEAD_DIM:] = jnp.where(lane == j, 1.0, 0.0).astype(BF16)
                vaug[hh, rows, 0:HEAD_DIM] = v_ref[rows, cols]
                vaug[hh, rows, HEAD_DIM:] = ones_col
                return carry
            lax.fori_loop(0, nb, fill, 0)

    own = pl.ds(pl.multiple_of(i * MOBA_BLOCK, MOBA_BLOCK), MOBA_BLOCK)
    head_cols = [slice(hh * HEAD_DIM, (hh + 1) * HEAD_DIM) for hh in range(hps)]

    def select(hh):
        q = q_ref[:, head_cols[hh]]
        qs = (q * (SOFTMAX_SCALE * np.log2(np.e))).astype(BF16)
        kmean = (ks_ref[0][:, head_cols[hh]] * (1.0 / MOBA_BLOCK)).astype(BF16)
        gate = _nt_dot(kmean, q.astype(BF16))
        blk = lax.broadcasted_iota(jnp.int32, (nb, tq), 0)
        blk_f = blk.astype(F32)
        gate = jnp.where(blk < i, gate, -jnp.inf)
        sel = jnp.zeros((nb, tq), F32)
        for _ in range(MOBA_TOP_K):
            mx = jnp.max(gate, axis=0, keepdims=True)
            first = jnp.min(jnp.where(gate == mx, blk_f, float(nb)), axis=0, keepdims=True)
            hit = blk_f == first
            sel = jnp.where(hit & (mx > -jnp.inf), 1.0, sel)
            gate = jnp.where(hit, -jnp.inf, gate)
        sel_pad = jnp.concatenate([sel, jnp.zeros((LANES - nb, tq), F32)], axis=0).astype(BF16)
        sel_q = _nt_dot(eye[...], sel_pad)
        bias = jnp.where(sel_q > 0.5, 0.0, MASKED).astype(BF16)
        return jnp.concatenate([qs, bias], axis=1)

    def variant(n_chunks):
        heads_ = range(hps)
        qa = [select(hh) for hh in heads_]
        r_id = lax.broadcasted_iota(jnp.int32, (tq, MOBA_BLOCK), 0)
        c_id = lax.broadcasted_iota(jnp.int32, (tq, MOBA_BLOCK), 1)
        s_own = [jnp.where(c_id <= r_id, _nt_dot(qa[hh][:, 0:HEAD_DIM], kaug[hh, own, 0:HEAD_DIM]), MASKED)
                 for hh in heads_]
        run = [jnp.maximum(s[:, 0:LANES], s[:, LANES:]) for s in s_own]
        for c in range(n_chunks):
            for hh in heads_:
                s_c = _nt_dot(qa[hh], kaug[hh, c * ck:(c + 1) * ck, :])
                s_sc[hh, :, c * ck:(c + 1) * ck] = s_c
                for u in range(0, ck, LANES):
                    run[hh] = jnp.maximum(run[hh], s_c[:, u:u + LANES])
        m = [jnp.max(r, axis=1, keepdims=True) for r in run]
        acc = [jnp.dot(jnp.exp2(s_own[hh] - m[hh]).astype(BF16), vaug[hh, own, :], preferred_element_type=F32)
               for hh in heads_]
        for c in range(n_chunks):
            for hh in heads_:
                p_c = jnp.exp2(s_sc[hh, :, c * ck:(c + 1) * ck] - m[hh]).astype(BF16)
                acc[hh] += jnp.dot(p_c, vaug[hh, c * ck:(c + 1) * ck, :], preferred_element_type=F32)
        for hh in heads_:
            o_ref[:, head_cols[hh]] = (acc[hh][:, 0:HEAD_DIM] / acc[hh][:, HEAD_DIM:HEAD_DIM + 1]).astype(o_ref.dtype)

    n_needed = (i + chunk_blocks - 1) // chunk_blocks
    for n in range(n_chunk_max + 1):
        @pl.when(n_needed == n)
        def _(n=n):
            side_work()
            variant(n)


def _scan_queries(q_ref):
    t = q_ref.shape[1]
    qs = q_ref[0] * SOFTMAX_SCALE
    q_hi = qs.astype(BF16).astype(F32)
    q_lo = (qs - q_hi).astype(BF16).astype(F32)
    return jnp.concatenate([q_hi, q_lo, jnp.zeros((16 - 2 * t, qs.shape[1]), F32)], axis=0).astype(BF16)


def _cache_scan_pages(p, page_refs, ksum_sc, *, heads, page):
    pps = len(page_refs)
    ppb = MOBA_BLOCK // page
    bps = pps // ppb
    sums = [jnp.sum(ref[...].reshape(page, heads, HEAD_DIM), axis=0) for ref in page_refs]
    for b in range(bps):
        blk_sum = functools.reduce(lambda x, y: x + y, sums[b * ppb:(b + 1) * ppb])
        ksum_sc[pl.ds(pl.multiple_of((p * bps + b) * heads, heads), heads), :] = blk_sum


def _cache_scan_gate(q_ref, ti_ref, ksum_sc, *, heads):
    t = q_ref.shape[1]
    q16 = _scan_queries(q_ref)
    nb = ksum_sc.shape[0] // heads
    lane = lax.broadcasted_iota(jnp.int32, (16, LANES), 1)
    lane_f = lane.astype(F32)
    for h in range(heads):
        kmean = ksum_sc[pl.ds(h, nb, stride=heads), :] * (1.0 / MOBA_BLOCK)
        km = jnp.concatenate([kmean, jnp.zeros((LANES - nb, HEAD_DIM), F32)], axis=0)
        km_hi, km_lo = _split_bf16(km)
        q_h = q16[:, h * HEAD_DIM:(h + 1) * HEAD_DIM]
        r1 = _nt_dot(q_h, km_hi)
        r2 = _nt_dot(q_h, km_lo)
        gate = r1 + pltpu.roll(r1, 16 - t, 0) + r2
        gate = jnp.where(lane < nb, gate, -jnp.inf)
        out = jnp.zeros((16, LANES), jnp.int32)
        for kk in range(MOBA_TOP_K):
            mx = jnp.max(gate, axis=1, keepdims=True)
            first = jnp.min(jnp.where(gate == mx, lane_f, float(LANES)), axis=1, keepdims=True)
            idx = jnp.where(mx > -jnp.inf, first, -1.0).astype(jnp.int32)
            out = jnp.where(lane == kk, idx, out)
            gate = jnp.where(lane_f == first, -jnp.inf, gate)
        ti_ref[0, h] = out[0:8]


def _attn_scan_kernel(pt_ref, q_ref, k_ref, v_ref, ks_ref, qs_ref, *rest, pps, hps, heads, page, scan_steps):
    page_refs = rest[:pps]
    o_ref, ti_ref, kaug, vaug, eye, s_sc, ksum_sc = rest[pps:]
    i = pl.program_id(2)
    flat = (pl.program_id(0) * pl.num_programs(1) + pl.program_id(1)) * pl.num_programs(2) + i
    p = flat % scan_steps

    def scan_pages():
        _cache_scan_pages(p, page_refs, ksum_sc, heads=heads, page=page)

    _attn_prompt_step(i, q_ref, k_ref, v_ref, ks_ref, o_ref, kaug, vaug, eye, s_sc,
                      hps=hps, chunk_blocks=ATTN_CHUNK_BLOCKS, side_work=scan_pages)

    @pl.when(p == scan_steps - 1)
    def _():
        _cache_scan_gate(qs_ref, ti_ref, ksum_sc, heads=heads)


def _attn_scan(page_table, q, k, v, ksum, q_s, cache, *, batch, seq, page):
    rows, attn_dim = q.shape
    heads = attn_dim // HEAD_DIM
    tq = MOBA_BLOCK
    nq = seq // tq
    nb = seq // MOBA_BLOCK
    hps = ATTN_HEADS_PER_STEP
    n_hp = heads // hps
    assert nb <= LANES and nb % ATTN_CHUNK_BLOCKS == 0 and heads % hps == 0

    nseq, n_pages = page_table.shape
    t = q_s.shape[1]
    past = n_pages * page
    nb_past = past // MOBA_BLOCK
    n_steps = batch * n_hp * nq
    assert n_steps % nseq == 0
    scan_steps = n_steps // nseq
    assert n_pages % scan_steps == 0
    pps = n_pages // scan_steps
    assert (pps * page) % MOBA_BLOCK == 0 and MOBA_BLOCK % page == 0
    assert nb_past <= LANES and 2 * t <= 8 and (pps * page // MOBA_BLOCK) % 8 == 0

    def flat(b, h, i):
        return (b * n_hp + h) * nq + i

    q_spec = pl.BlockSpec((tq, hps * HEAD_DIM), lambda b, h, i, pt: (b * nq + i, h))
    kv_spec = pl.BlockSpec((seq, hps * HEAD_DIM), lambda b, h, i, pt: (b, h))
    page_specs = [
        pl.BlockSpec((page * heads, HEAD_DIM),
                     functools.partial(lambda b, h, i, pt, u: (pt[flat(b, h, i) // scan_steps,
                                                                  (flat(b, h, i) % scan_steps) * pps + u], 0), u=u))
        for u in range(pps)
    ]
    grid_spec = pltpu.PrefetchScalarGridSpec(
        num_scalar_prefetch=1,
        grid=(batch, n_hp, nq),
        in_specs=[q_spec, kv_spec, kv_spec,
                  pl.BlockSpec((1, nb, hps * HEAD_DIM), lambda b, h, i, pt: (b, 0, h)),
                  pl.BlockSpec((1, t, attn_dim), lambda b, h, i, pt: (flat(b, h, i) // scan_steps, 0, 0))]
        + page_specs,
        out_specs=[
            q_spec,
            pl.BlockSpec((1, heads, 8, LANES), lambda b, h, i, pt: (flat(b, h, i) // scan_steps, 0, 0, 0)),
        ],
        scratch_shapes=[
            pltpu.VMEM((hps, seq, 2 * HEAD_DIM), BF16),
            pltpu.VMEM((hps, seq, 2 * HEAD_DIM), BF16),
            pltpu.VMEM((tq, tq), BF16),
            pltpu.VMEM((hps, tq, seq), F32),
            pltpu.VMEM((nb_past * heads, HEAD_DIM), F32),
        ],
    )
    return pl.pallas_call(
        functools.partial(_attn_scan_kernel, pps=pps, hps=hps, heads=heads, page=page, scan_steps=scan_steps),
        grid_spec=grid_spec,
        out_shape=[
            jax.ShapeDtypeStruct((rows, attn_dim), BF16),
            jax.ShapeDtypeStruct((nseq, heads, 8, LANES), jnp.int32),
        ],
        compiler_params=_params("arbitrary", "arbitrary", "arbitrary"),
        name="attn_scan",
    )(page_table, q, k, v, ksum.reshape(batch, nb, attn_dim), q_s, *([cache] * pps))


def _attn_sample_kernel(pt_ref, ti_ref, q_ref, kn_ref, vn_ref, ck_hbm, cv_hbm, o_ref, kbuf, vbuf, sem, *, heads):
    s = pl.program_id(0)
    h = pl.program_id(1)
    n = s * heads + h
    total = pl.num_programs(0) * heads
    t = q_ref.shape[1]
    nsel = t * MOBA_TOP_K
    page = cv_hbm.shape[1]
    ppb = MOBA_BLOCK // page

    def copies(s_, h_, slot_):
        out = []
        for c in range(nsel):
            blk = jnp.maximum(ti_ref[s_, h_ * nsel + c], 0)
            for e in range(ppb):
                pg = pt_ref[s_, blk * ppb + e]
                for src, dst in ((ck_hbm, kbuf), (cv_hbm, vbuf)):
                    out.append(pltpu.make_async_copy(
                        src.at[pg, :, h_, :], dst.at[slot_, c * ppb + e], sem.at[slot_]))
        return out

    n_slots = vbuf.shape[0]
    slot = n % n_slots

    def start_step(m):
        for cp in copies(m // heads, m % heads, m % n_slots):
            cp.start()

    @pl.when(n == 0)
    def _():
        for ahead in range(n_slots - 1):
            start_step(ahead)

    @pl.when(n + n_slots - 1 < total)
    def _():
        start_step(n + n_slots - 1)

    for cp in copies(s, h, slot):
        cp.wait()

    qs = q_ref[0] * SOFTMAX_SCALE
    kn = kn_ref[0]
    vn = vn_ref[0]
    width = nsel * MOBA_BLOCK
    seg_w = MOBA_TOP_K * MOBA_BLOCK
    q16 = jnp.concatenate([qs, jnp.zeros((16 - t, HEAD_DIM), F32)], axis=0).astype(BF16)
    ksel = kbuf[slot].reshape(width, HEAD_DIM).astype(BF16)
    lt = _nt_dot(q16, ksel)[0:t]
    segs = []
    for c in range(nsel):
        ok = ti_ref[s, h * nsel + c] >= 0
        segs.append(jnp.where(ok, lt[:, c * MOBA_BLOCK:(c + 1) * MOBA_BLOCK], MASKED))
    lm = jnp.concatenate(segs, axis=1)
    col = lax.broadcasted_iota(jnp.int32, (t, width), 1)
    row = lax.broadcasted_iota(jnp.int32, (t, width), 0)
    lm = jnp.where((col >= row * seg_w) & (col < (row + 1) * seg_w), lm, MASKED)
    row_o = lax.broadcasted_iota(jnp.int32, (t, LANES), 0)
    col_o = lax.broadcasted_iota(jnp.int32, (t, LANES), 1)
    own = jnp.full((t, LANES), MASKED, F32)
    for j in range(t):
        dj = jnp.sum(qs * kn[j:j + 1, :], axis=1, keepdims=True)
        own = jnp.where((col_o == j) & (j <= row_o), dj, own)
    full = jnp.concatenate([lm, own], axis=1)
    m = jnp.max(full, axis=1, keepdims=True)
    p = jnp.exp(full - m)
    l = jnp.sum(p, axis=1, keepdims=True)
    p_pad = jnp.concatenate([p[:, :width], jnp.zeros((8 - t, width), F32)], axis=0).astype(BF16)
    vsel = vbuf[slot].reshape(width, HEAD_DIM).astype(BF16)
    pv = jnp.dot(p_pad, vsel, preferred_element_type=F32)[0:t]
    for j in range(t):
        pv += p[:, width + j:width + j + 1] * vn[j:j + 1, :]
    o_ref[0] = pv / l


def _attn_sample(page_table, top_idx, q_s, k_s, v_s, cache_k, cache_v):
    nseq, t, attn_dim = q_s.shape
    heads = attn_dim // HEAD_DIM
    page = cache_v.shape[1]
    nsel = t * MOBA_TOP_K
    ppb = MOBA_BLOCK // page
    new_spec = pl.BlockSpec((1, t, HEAD_DIM), lambda s, h, pt, ti: (s, 0, h))
    grid_spec = pltpu.PrefetchScalarGridSpec(
        num_scalar_prefetch=2,
        grid=(nseq, heads),
        in_specs=[new_spec, new_spec, new_spec,
                  pl.BlockSpec(memory_space=pl.ANY), pl.BlockSpec(memory_space=pl.ANY)],
        out_specs=new_spec,
        scratch_shapes=[
            pltpu.VMEM((GATHER_SLOTS, nsel * ppb, page, HEAD_DIM), F32),
            pltpu.VMEM((GATHER_SLOTS, nsel * ppb, page, HEAD_DIM), F32),
            pltpu.SemaphoreType.DMA((GATHER_SLOTS,)),
        ],
    )
    return pl.pallas_call(
        functools.partial(_attn_sample_kernel, heads=heads),
        grid_spec=grid_spec,
        out_shape=jax.ShapeDtypeStruct((nseq, t, attn_dim), F32),
        compiler_params=_params("arbitrary", "arbitrary"),
        name="attn_sample",
    )(page_table, top_idx, q_s, k_s, v_s, cache_k, cache_v)


def _post_attn_kernel(x_ref, o_ref, mod_ref, g_ref, wo_ref, wup_ref, wdown_ref, y_ref):
    d = x_ref.shape[1]
    m = mod_ref[0]
    a = jnp.dot(o_ref[...].astype(BF16), wo_ref[...], preferred_element_type=F32)
    x1 = x_ref[...] + m[:, 2 * d:3 * d] * a
    h = _norm_mod(x1, g_ref[...], m[:, 3 * d:4 * d], m[:, 4 * d:5 * d])
    y_ref[...] = x1 + m[:, 5 * d:6 * d] * _mlp(h, wup_ref, wdown_ref)


def _post_attn(x, o, mod, g, wo, wup, wdown, *, tm, tiles_per_seq):
    rows, d = x.shape
    row_spec = pl.BlockSpec((tm, d), lambda i: (i, 0))
    return pl.pallas_call(
        _post_attn_kernel,
        grid=(rows // tm,),
        in_specs=[
            row_spec,
            pl.BlockSpec((tm, o.shape[1]), lambda i: (i, 0)),
            pl.BlockSpec((1, mod.shape[1], mod.shape[2]), lambda i: (i // tiles_per_seq, 0, 0)),
            _const_spec((1, d)),
            _const_spec(wo.shape), _const_spec(wup.shape), _const_spec(wdown.shape),
        ],
        out_specs=row_spec,
        out_shape=jax.ShapeDtypeStruct((rows, d), F32),
        compiler_params=_params("arbitrary"),
        name="post_attn",
    )(x, o, mod, g, wo, wup, wdown)


def _conv_layer_kernel(x_ref, mod_ref, g1_ref, g2_ref, gf_ref, wbcx_ref, wconv_ref, wout_ref, wup_ref, wdown_ref,
                       *rest, tiles_per_seq, period, tail):
    if period is None:
        y_ref, ut_ref, ext = rest
    else:
        e1_ref, e2_ref, y_ref, ut_ref, ext = rest
    i = pl.program_id(0)
    tm, d = x_ref.shape
    halo = 8
    m = mod_ref[0]
    x = x_ref[...]
    h = _norm_mod(x, g1_ref[...], m[:, 0:d], m[:, d:2 * d]).astype(BF16)
    bg = jnp.dot(h, wbcx_ref[:, 0:d], preferred_element_type=F32)
    cg = jnp.dot(h, wbcx_ref[:, d:2 * d], preferred_element_type=F32)
    xin = jnp.dot(h, wbcx_ref[:, 2 * d:3 * d], preferred_element_type=F32)
    u = cg * xin

    @pl.when(i % tiles_per_seq == 0)
    def _():
        ext[0:halo, :] = jnp.zeros((halo, d), F32)

    ext[halo:halo + tm, :] = u
    prev1 = ext[halo - 1:halo - 1 + tm, :]
    prev2 = ext[halo - 2:halo - 2 + tm, :]
    if period is not None:
        tpos = lax.broadcasted_iota(jnp.int32, (tm, d), 0) % period
        prev1 = jnp.where(tpos < 1, e1_ref[...], prev1)
        prev2 = jnp.where(tpos < 2, e2_ref[...], prev2)
    w = wconv_ref[...]
    conv = w[0:1, :] * prev2 + w[1:2, :] * prev1 + w[2:3, :] * u
    ext[0:halo, :] = ext[tm:tm + halo, :]
    ut_ref[0] = u[tm - tail:, :]

    y = jnp.dot((bg * conv).astype(BF16), wout_ref[...], preferred_element_type=F32)
    x1 = x + m[:, 2 * d:3 * d] * y
    h2 = _norm_mod(x1, g2_ref[...], m[:, 3 * d:4 * d], m[:, 4 * d:5 * d])
    x2 = x1 + m[:, 5 * d:6 * d] * _mlp(h2, wup_ref, wdown_ref)
    ms = jnp.mean(x2 * x2, axis=-1, keepdims=True)
    y_ref[...] = x2 * lax.rsqrt(ms + RMS_EPS) * gf_ref[...]


def _conv_layer(x, mod, g1, g2, gf, wbcx, wconv, wout, wup, wdown, state_rows, *, tm, tiles_per_seq, period, tail):
    rows, d = x.shape
    n_tiles = rows // tm
    row_spec = pl.BlockSpec((tm, d), lambda i: (i, 0))
    in_specs = [
        row_spec,
        pl.BlockSpec((1, mod.shape[1], mod.shape[2]), lambda i: (i // tiles_per_seq, 0, 0)),
        _const_spec((1, d)), _const_spec((1, d)), _const_spec((1, d)),
        _const_spec(wbcx.shape), _const_spec(wconv.shape), _const_spec(wout.shape),
        _const_spec(wup.shape), _const_spec(wdown.shape),
    ]
    args = [x, mod, g1, g2, gf, wbcx, wconv, wout, wup, wdown]
    if period is not None:
        in_specs += [row_spec, row_spec]
        args += list(state_rows)
    return pl.pallas_call(
        functools.partial(_conv_layer_kernel, tiles_per_seq=tiles_per_seq, period=period, tail=tail),
        grid=(n_tiles,),
        in_specs=in_specs,
        out_specs=[row_spec, pl.BlockSpec((1, tail, d), lambda i: (i, 0, 0))],
        out_shape=[jax.ShapeDtypeStruct((rows, d), F32), jax.ShapeDtypeStruct((n_tiles, tail, d), F32)],
        scratch_shapes=[pltpu.VMEM((tm + 8, d), F32)],
        compiler_params=_params("arbitrary"),
        name="conv_layer",
    )(*args)


def kernel(x_prompt, x_sample, cache_k, cache_v, state_conv, page_table, c_prompt, c_sample, w_ada, b_ada, g_norm,
           w_qkv, w_o, w_bcx, w_conv, w_out, w_up, w_down, g_final):
    batch, seq, d = x_prompt.shape
    nseq, t, _ = x_sample.shape
    depth = w_ada.shape[0]
    n_layers_attn, pool, page, heads, head_dim = cache_k.shape
    attn_dim = heads * head_dim
    n_pages = page_table.shape[1]
    past = n_pages * page
    kw = state_conv.shape[2] + 1
    assert depth == 2 and n_layers_attn == 1 and state_conv.shape[0] == 1, "layer 0 attention, layer 1 convolution"
    assert head_dim == HEAD_DIM and kw == 3 and t >= kw - 1
    assert seq % ROW_TILE == 0 and ROW_TILE % MOBA_BLOCK == 0 and past % MOBA_BLOCK == 0 and t <= MOBA_BLOCK
    rows_p, rows_s = batch * seq, nseq * t
    assert rows_s % 8 == 0

    n_c = batch + nseq
    pad = -n_c % 8
    c_all = jnp.concatenate([c_prompt, c_sample, jnp.zeros((pad, d), F32)], axis=0)
    mod = _ada(c_all, w_ada, b_ada)
    mod_p = mod[:, :batch, None, :]
    mod_s = jnp.repeat(mod[:, batch:n_c], t, axis=1)[:, None]

    bf = lambda a: a.astype(BF16)
    wqkv, wo, wbcx, wout = bf(w_qkv[0]), bf(w_o[0]), bf(w_bcx[0]), bf(w_out[0])
    wup, wdown = bf(w_up), bf(w_down)
    g = g_norm.reshape(depth, 2, 1, d)
    gf = g_final.reshape(1, d)

    xp = x_prompt.reshape(rows_p, d)
    xs = x_sample.reshape(rows_s, d)
    tps = seq // ROW_TILE
    tab_p = _rope_tables(np.arange(seq))
    tab_s = _rope_tables(past + (np.arange(rows_s) % t))

    q_p, k_p, v_p, kb_p, vb_p, ksum = _qkv(xp, mod_p[0], g[0, 0], wqkv, tab_p, tm=ROW_TILE, tiles_per_seq=tps,
                                           with_ksum=True)
    q_s, k_s, v_s, kb_s, vb_s = _qkv(xs, mod_s[0], g[0, 0], wqkv, tab_s, tm=rows_s, tiles_per_seq=1,
                                     with_ksum=False)
    q_s3, kn_s3, vn_s3 = (a.astype(F32).reshape(nseq, t, attn_dim) for a in (q_s, kb_s, vb_s))
    o_p, top = _attn_scan(page_table, q_p, kb_p, vb_p, ksum, q_s3,
                          cache_k.reshape(pool * page * heads, head_dim), batch=batch, seq=seq, page=page)
    top_idx = top[:, :, :t, :MOBA_TOP_K].reshape(nseq, heads * t * MOBA_TOP_K)
    o_s = _attn_sample(page_table, top_idx, q_s3, kn_s3, vn_s3, cache_k.reshape(pool, page, heads, head_dim),
                       cache_v.reshape(pool, page, heads, head_dim))

    x1_p = _post_attn(xp, o_p, mod_p[0], g[0, 1], wo, wup[0], wdown[0], tm=ROW_TILE, tiles_per_seq=tps)
    x1_s = _post_attn(xs, o_s.reshape(rows_s, attn_dim), mod_s[0], g[0, 1], wo, wup[0], wdown[0],
                      tm=rows_s, tiles_per_seq=1)

    y_p, ut_p = _conv_layer(x1_p, mod_p[1], g[1, 0], g[1, 1], gf, wbcx, w_conv[0], wout, wup[1], wdown[1], None,
                            tm=ROW_TILE, tiles_per_seq=tps, period=None, tail=8)
    st = state_conv[0]
    zeros = jnp.zeros((nseq, t - 1, d), F32)
    e1 = jnp.concatenate([st[:, 1:2], zeros], axis=1).reshape(rows_s, d)
    e2 = jnp.concatenate([st[:, 0:2], zeros[:, 1:]], axis=1).reshape(rows_s, d)
    y_s, ut_s = _conv_layer(x1_s, mod_s[1], g[1, 0], g[1, 1], gf, wbcx, w_conv[0], wout, wup[1], wdown[1], (e1, e2),
                            tm=rows_s, tiles_per_seq=1, period=t, tail=rows_s)

    conv_p = ut_p.reshape(batch, tps, 8, d)[:, -1, 8 - (kw - 1):, :]
    conv_s = ut_s.reshape(nseq, t, d)[:, t - (kw - 1):, :]
    return (
        y_p.reshape(batch, seq, d),
        y_s.reshape(nseq, t, d),
        k_p.reshape(1, batch, seq, heads, head_dim),
        v_p.reshape(1, batch, seq, heads, head_dim),
        k_s.reshape(1, nseq, t, heads, head_dim),
        v_s.reshape(1, nseq, t, heads, head_dim),
        conv_p[None],
        conv_s[None],
    )
```

```python
import functools

import numpy as np
import jax
import jax.numpy as jnp
from jax import lax
from jax.experimental import pallas as pl
from jax.experimental.pallas import tpu as pltpu

F32 = jnp.float32
BF16 = jnp.bfloat16

HEAD_DIM = 128
MOBA_BLOCK = 256
MOBA_TOP_K = 3
ROPE_DIM = HEAD_DIM // 4
ROPE_THETA = 500000.0
RMS_EPS = 1e-6
SOFTMAX_SCALE = HEAD_DIM ** -0.5
MASKED = -1e30

LANES = 128
VMEM_LIMIT = 56 * 1024 * 1024
ROW_TILE = 256
ADA_COL_TILE = 1536
SCAN_SLOTS = 3
GATHER_SLOTS = 4
ATTN_HEADS_PER_STEP = 2
ATTN_CHUNK_BLOCKS = 4

NT_DIMS = (((1,), (1,)), ((), ()))


def _nt_dot(a, b):
    return lax.dot_general(a, b, NT_DIMS, preferred_element_type=F32)


def _split_bf16(x):
    hi = x.astype(BF16)
    lo = (x - hi.astype(F32)).astype(BF16)
    return hi, lo


def _const_spec(shape):
    n = len(shape)
    return pl.BlockSpec(shape, lambda *_: (0,) * n, pipeline_mode=pl.Buffered(1))


def _params(*sem):
    return pltpu.CompilerParams(dimension_semantics=sem, vmem_limit_bytes=VMEM_LIMIT)


def _ada_kernel(c_ref, w_ref, b_ref, o_ref):
    c_hi, c_lo = _split_bf16(c_ref[...])
    w_hi, w_lo = _split_bf16(w_ref[0])
    acc = jnp.dot(c_hi, w_hi, preferred_element_type=F32)
    acc += jnp.dot(c_lo, w_hi, preferred_element_type=F32)
    acc += jnp.dot(c_hi, w_lo, preferred_element_type=F32)
    o_ref[0] = acc + b_ref[0]


def _ada(c_all, w_ada, b_ada):
    depth, d, n6 = w_ada.shape
    rows = c_all.shape[0]
    tn = ADA_COL_TILE
    return pl.pallas_call(
        _ada_kernel,
        grid=(depth, n6 // tn),
        in_specs=[
            pl.BlockSpec((rows, d), lambda l, n: (0, 0)),
            pl.BlockSpec((1, d, tn), lambda l, n: (l, 0, n)),
            pl.BlockSpec((1, 1, tn), lambda l, n: (l, 0, n)),
        ],
        out_specs=pl.BlockSpec((1, rows, tn), lambda l, n: (l, 0, n)),
        out_shape=jax.ShapeDtypeStruct((depth, rows, n6), F32),
        compiler_params=_params("arbitrary", "arbitrary"),
        name="ada",
    )(c_all, w_ada, b_ada.reshape(depth, 1, n6))


def _norm_mod(x, g, shift, scale):
    ms = jnp.mean(x * x, axis=-1, keepdims=True)
    y = x * lax.rsqrt(ms + RMS_EPS) * g
    return y * (1.0 + scale) + shift


def _mlp(h, wup_ref, wdown_ref):
    d, dff = wup_ref.shape
    hb = h.astype(BF16)
    acc = jnp.zeros((h.shape[0], d), F32)
    for c in range(0, dff, d):
        u = jnp.dot(hb, wup_ref[:, c:c + d], preferred_element_type=F32)
        u = jnp.square(jnp.maximum(u, 0.0)).astype(BF16)
        acc += jnp.dot(u, wdown_ref[c:c + d, :], preferred_element_type=F32)
    return acc


def _rope_tables(pos):
    half = ROPE_DIM // 2
    inv = np.float32(ROPE_THETA) ** (-np.arange(0, ROPE_DIM, 2, dtype=np.float32) / np.float32(ROPE_DIM))
    ang = (pos.astype(np.float32)[:, None] * inv.astype(np.float32)[None, :]).astype(np.float32)
    cos = np.cos(ang.astype(np.float64))
    sin = np.sin(ang.astype(np.float64))
    n = pos.shape[0]
    c = np.ones((n, HEAD_DIM), np.float64)
    s1 = np.zeros((n, HEAD_DIM), np.float64)
    s2 = np.zeros((n, HEAD_DIM), np.float64)
    c[:, :half] = cos
    c[:, half:ROPE_DIM] = cos
    s1[:, :half] = -sin
    s2[:, half:ROPE_DIM] = sin
    return tuple(jnp.asarray(t.astype(np.float32)) for t in (c, s1, s2))


def _qkv_kernel(x_ref, mod_ref, g_ref, w_ref, cos_ref, s1_ref, s2_ref, q_ref, k5_ref, v5_ref, kb_ref, vb_ref,
                *ks_ref, attn_dim):
    d = x_ref.shape[1]
    m = mod_ref[0]
    h = _norm_mod(x_ref[...], g_ref[...], m[:, 0:d], m[:, d:2 * d]).astype(BF16)
    cos, s1, s2 = cos_ref[...], s1_ref[...], s2_ref[...]
    tm = h.shape[0]
    heads = attn_dim // HEAD_DIM
    cw = 2 * HEAD_DIM
    for c in range(0, 3 * attn_dim, cw):
        r = jnp.dot(h, w_ref[:, c:c + cw], preferred_element_type=F32)
        if c < 2 * attn_dim:
            rot = []
            for u in range(0, cw, HEAD_DIM):
                z = r[:, u:u + HEAD_DIM]
                z = (z * cos + pltpu.roll(z, HEAD_DIM - ROPE_DIM // 2, 1) * s1
                     + pltpu.roll(z, ROPE_DIM // 2, 1) * s2)
                rot.append(z)
            r = jnp.concatenate(rot, axis=1)
        if c < attn_dim:
            q_ref[:, c:c + cw] = r
            continue
        is_k = c < 2 * attn_dim
        c0 = c - (attn_dim if is_k else 2 * attn_dim)
        (kb_ref if is_k else vb_ref)[:, c0:c0 + cw] = r.astype(BF16)
        for u in range(0, cw, HEAD_DIM):
            head = (c0 + u) // HEAD_DIM
            (k5_ref if is_k else v5_ref)[pl.ds(head, tm, stride=heads), :] = r[:, u:u + HEAD_DIM]
        if is_k and ks_ref:
            nblk = tm // MOBA_BLOCK
            ks_ref[0][0, :, c0:c0 + cw] = jnp.sum(r.reshape(nblk, MOBA_BLOCK, cw), axis=1)


def _qkv(x, mod, g, w, tables, *, tm, tiles_per_seq, with_ksum):
    rows, d = x.shape
    attn_dim = w.shape[1] // 3
    heads = attn_dim // HEAD_DIM
    n_tiles = rows // tm
    mod_rows = mod.shape[1]
    row_spec = pl.BlockSpec((tm, attn_dim), lambda i: (i, 0))
    tok_head_spec = pl.BlockSpec((tm * heads, HEAD_DIM), lambda i: (i, 0))
    tab_spec = pl.BlockSpec((tm, HEAD_DIM), lambda i: (i % tiles_per_seq, 0))
    out_specs = [row_spec, tok_head_spec, tok_head_spec, row_spec, row_spec]
    out_shape = ([jax.ShapeDtypeStruct((rows, attn_dim), F32)]
                 + [jax.ShapeDtypeStruct((rows * heads, HEAD_DIM), F32)] * 2
                 + [jax.ShapeDtypeStruct((rows, attn_dim), BF16)] * 2)
    if with_ksum:
        nblk = tm // MOBA_BLOCK
        out_specs.append(pl.BlockSpec((1, nblk, attn_dim), lambda i: (i, 0, 0)))
        out_shape.append(jax.ShapeDtypeStruct((n_tiles, nblk, attn_dim), F32))
    return pl.pallas_call(
        functools.partial(_qkv_kernel, attn_dim=attn_dim),
        grid=(n_tiles,),
        in_specs=[
            pl.BlockSpec((tm, d), lambda i: (i, 0)),
            pl.BlockSpec((1, mod_rows, mod.shape[2]), lambda i: (i // tiles_per_seq, 0, 0)),
            _const_spec((1, d)),
            _const_spec(w.shape),
            tab_spec, tab_spec, tab_spec,
        ],
        out_specs=out_specs,
        out_shape=out_shape,
        compiler_params=_params("arbitrary"),
        name="qkv",
    )(x, mod, g, w, *tables)


def _attn_prompt_step(i, q_ref, k_ref, v_ref, ks_ref, o_ref, kaug, vaug, eye, s_sc, *, hps, chunk_blocks, side_work):
    tq = q_ref.shape[0]
    seq = k_ref.shape[0]
    nb = seq // MOBA_BLOCK
    ck = chunk_blocks * MOBA_BLOCK
    n_chunk_max = nb // chunk_blocks

    @pl.when(i == 0)
    def _():
        r_id = lax.broadcasted_iota(jnp.int32, (tq, tq), 0)
        c_id = lax.broadcasted_iota(jnp.int32, (tq, tq), 1)
        eye[...] = jnp.where(r_id == c_id, 1.0, 0.0).astype(BF16)
        lane = lax.broadcasted_iota(jnp.int32, (MOBA_BLOCK, LANES), 1)
        ones_col = jnp.where(lane == 0, 1.0, 0.0).astype(BF16)
        for hh in range(hps):
            cols = slice(hh * HEAD_DIM, (hh + 1) * HEAD_DIM)

            def fill(j, carry):
                rows = pl.ds(pl.multiple_of(j * MOBA_BLOCK, MOBA_BLOCK), MOBA_BLOCK)
                kaug[hh, rows, 0:HEAD_DIM] = k_ref[rows, cols]
                kaug[hh, rows, HEAD_DIM:] = jnp.where(lane == j, 1.0, 0.0).astype(BF16)
                vaug[hh, rows, 0:HEAD_DIM] = v_ref[rows, cols]
                vaug[hh, rows, HEAD_DIM:] = ones_col
                return carry
            lax.fori_loop(0, nb, fill, 0)

    own = pl.ds(pl.multiple_of(i * MOBA_BLOCK, MOBA_BLOCK), MOBA_BLOCK)
    head_cols = [slice(hh * HEAD_DIM, (hh + 1) * HEAD_DIM) for hh in range(hps)]

    def select(hh):
        q = q_ref[:, head_cols[hh]]
        qs = (q * (SOFTMAX_SCALE * np.log2(np.e))).astype(BF16)
        kmean = (ks_ref[0][:, head_cols[hh]] * (1.0 / MOBA_BLOCK)).astype(BF16)
        gate = _nt_dot(kmean, q.astype(BF16))
        blk = lax.broadcasted_iota(jnp.int32, (nb, tq), 0)
        blk_f = blk.astype(F32)
        gate = jnp.where(blk < i, gate, -jnp.inf)
        sel = jnp.zeros((nb, tq), F32)
        for _ in range(MOBA_TOP_K):
            mx = jnp.max(gate, axis=0, keepdims=True)
            first = jnp.min(jnp.where(gate == mx, blk_f, float(nb)), axis=0, keepdims=True)
            hit = blk_f == first
            sel = jnp.where(hit & (mx > -jnp.inf), 1.0, sel)
            gate = jnp.where(hit, -jnp.inf, gate)
        sel_pad = jnp.concatenate([sel, jnp.zeros((LANES - nb, tq), F32)], axis=0).astype(BF16)
        sel_q = _nt_dot(eye[...], sel_pad)
        bias = jnp.where(sel_q > 0.5, 0.0, MASKED).astype(BF16)
        return jnp.concatenate([qs, bias], axis=1)

    def variant(n_chunks):
        heads_ = range(hps)
        qa = [select(hh) for hh in heads_]
        r_id = lax.broadcasted_iota(jnp.int32, (tq, MOBA_BLOCK), 0)
        c_id = lax.broadcasted_iota(jnp.int32, (tq, MOBA_BLOCK), 1)
        s_own = [jnp.where(c_id <= r_id, _nt_dot(qa[hh][:, 0:HEAD_DIM], kaug[hh, own, 0:HEAD_DIM]), MASKED)
                 for hh in heads_]
        run = [jnp.maximum(s[:, 0:LANES], s[:, LANES:]) for s in s_own]
        for c in range(n_chunks):
            for hh in heads_:
                s_c = _nt_dot(qa[hh], kaug[hh, c * ck:(c + 1) * ck, :])
                s_sc[hh, :, c * ck:(c + 1) * ck] = s_c
                for u in range(0, ck, LANES):
                    run[hh] = jnp.maximum(run[hh], s_c[:, u:u + LANES])
        m = [jnp.max(r, axis=1, keepdims=True) for r in run]
        acc = [jnp.dot(jnp.exp2(s_own[hh] - m[hh]).astype(BF16), vaug[hh, own, :], preferred_element_type=F32)
               for hh in heads_]
        for c in range(n_chunks):
            for hh in heads_:
                p_c = jnp.exp2(s_sc[hh, :, c * ck:(c + 1) * ck] - m[hh]).astype(BF16)
                acc[hh] += jnp.dot(p_c, vaug[hh, c * ck:(c + 1) * ck, :], preferred_element_type=F32)
        for hh in heads_:
            o_ref[:, head_cols[hh]] = (acc[hh][:, 0:HEAD_DIM] / acc[hh][:, HEAD_DIM:HEAD_DIM + 1]).astype(o_ref.dtype)

    n_needed = (i + chunk_blocks - 1) // chunk_blocks
    for n in range(n_chunk_max + 1):
        @pl.when(n_needed == n)
        def _(n=n):
            side_work()
            variant(n)


def _scan_queries(q_ref):
    t = q_ref.shape[1]
    qs = q_ref[0] * SOFTMAX_SCALE
    q_hi = qs.astype(BF16).astype(F32)
    q_lo = (qs - q_hi).astype(BF16).astype(F32)
    return jnp.concatenate([q_hi, q_lo, jnp.zeros((16 - 2 * t, qs.shape[1]), F32)], axis=0).astype(BF16)


def _cache_scan_pages(p, page_refs, ksum_sc, *, heads, page):
    pps = len(page_refs)
    ppb = MOBA_BLOCK // page
    bps = pps // ppb
    sums = [jnp.sum(ref[...].reshape(page, heads, HEAD_DIM), axis=0) for ref in page_refs]
    for b in range(bps):
        blk_sum = functools.reduce(lambda x, y: x + y, sums[b * ppb:(b + 1) * ppb])
        ksum_sc[pl.ds(pl.multiple_of((p * bps + b) * heads, heads), heads), :] = blk_sum


def _cache_scan_gate(q_ref, ti_ref, ksum_sc, *, heads):
    t = q_ref.shape[1]
    q16 = _scan_queries(q_ref)
    nb = ksum_sc.shape[0] // heads
    lane = lax.broadcasted_iota(jnp.int32, (16, LANES), 1)
    lane_f = lane.astype(F32)
    for h in range(heads):
        kmean = ksum_sc[pl.ds(h, nb, stride=heads), :] * (1.0 / MOBA_BLOCK)
        km = jnp.concatenate([kmean, jnp.zeros((LANES - nb, HEAD_DIM), F32)], axis=0)
        km_hi, km_lo = _split_bf16(km)
        q_h = q16[:, h * HEAD_DIM:(h + 1) * HEAD_DIM]
        r1 = _nt_dot(q_h, km_hi)
        r2 = _nt_dot(q_h, km_lo)
        gate = r1 + pltpu.roll(r1, 16 - t, 0) + r2
        gate = jnp.where(lane < nb, gate, -jnp.inf)
        out = jnp.zeros((16, LANES), jnp.int32)
        for kk in range(MOBA_TOP_K):
            mx = jnp.max(gate, axis=1, keepdims=True)
            first = jnp.min(jnp.where(gate == mx, lane_f, float(LANES)), axis=1, keepdims=True)
            idx = jnp.where(mx > -jnp.inf, first, -1.0).astype(jnp.int32)
            out = jnp.where(lane == kk, idx, out)
            gate = jnp.where(lane_f == first, -jnp.inf, gate)
        ti_ref[0, h] = out[0:8]


def _attn_scan_kernel(pt_ref, q_ref, k_ref, v_ref, ks_ref, qs_ref, cache_hbm, o_ref, ti_ref,
                      kaug, vaug, eye, s_sc, ksum_sc, pbuf, psem, *, hps, heads, page, scan_steps):
    n_slots, pps = pbuf.shape[0], pbuf.shape[1]
    page_rows = page * heads
    i = pl.program_id(2)
    flat = (pl.program_id(0) * pl.num_programs(1) + pl.program_id(1)) * pl.num_programs(2) + i
    total = pl.num_programs(0) * pl.num_programs(1) * pl.num_programs(2)
    p = flat % scan_steps
    slot = flat % n_slots

    def page_copies(step):
        seq_id, first = step // scan_steps, (step % scan_steps) * pps
        return [pltpu.make_async_copy(
            cache_hbm.at[pl.ds(pl.multiple_of(pt_ref[seq_id, first + u] * page_rows, page_rows), page_rows), :],
            pbuf.at[step % n_slots, u], psem.at[step % n_slots]) for u in range(pps)]

    @pl.when(flat == 0)
    def _():
        for ahead in range(n_slots - 1):
            for cp in page_copies(ahead):
                cp.start()

    @pl.when(flat + n_slots - 1 < total)
    def _():
        for cp in page_copies(flat + n_slots - 1):
            cp.start()

    for cp in page_copies(flat):
        cp.wait()
    page_refs = [pbuf.at[slot, u] for u in range(pps)]

    def scan_pages():
        _cache_scan_pages(p, page_refs, ksum_sc, heads=heads, page=page)

    _attn_prompt_step(i, q_ref, k_ref, v_ref, ks_ref, o_ref, kaug, vaug, eye, s_sc,
                      hps=hps, chunk_blocks=ATTN_CHUNK_BLOCKS, side_work=scan_pages)

    @pl.when(p == scan_steps - 1)
    def _():
        _cache_scan_gate(qs_ref, ti_ref, ksum_sc, heads=heads)


def _attn_scan(page_table, q, k, v, ksum, q_s, cache, *, batch, seq, page):
    rows, attn_dim = q.shape
    heads = attn_dim // HEAD_DIM
    tq = MOBA_BLOCK
    nq = seq // tq
    nb = seq // MOBA_BLOCK
    hps = ATTN_HEADS_PER_STEP
    n_hp = heads // hps
    assert nb <= LANES and nb % ATTN_CHUNK_BLOCKS == 0 and heads % hps == 0

    nseq, n_pages = page_table.shape
    t = q_s.shape[1]
    past = n_pages * page
    nb_past = past // MOBA_BLOCK
    n_steps = batch * n_hp * nq
    assert n_steps % nseq == 0
    scan_steps = n_steps // nseq
    assert n_pages % scan_steps == 0
    pps = n_pages // scan_steps
    assert (pps * page) % MOBA_BLOCK == 0 and MOBA_BLOCK % page == 0
    assert nb_past <= LANES and 2 * t <= 8 and (pps * page // MOBA_BLOCK) % 8 == 0

    def flat(b, h, i):
        return (b * n_hp + h) * nq + i

    q_spec = pl.BlockSpec((tq, hps * HEAD_DIM), lambda b, h, i, pt: (b * nq + i, h))
    kv_spec = pl.BlockSpec((seq, hps * HEAD_DIM), lambda b, h, i, pt: (b, h))
    assert n_steps >= SCAN_SLOTS
    grid_spec = pltpu.PrefetchScalarGridSpec(
        num_scalar_prefetch=1,
        grid=(batch, n_hp, nq),
        in_specs=[q_spec, kv_spec, kv_spec,
                  pl.BlockSpec((1, nb, hps * HEAD_DIM), lambda b, h, i, pt: (b, 0, h)),
                  pl.BlockSpec((1, t, attn_dim), lambda b, h, i, pt: (flat(b, h, i) // scan_steps, 0, 0)),
                  pl.BlockSpec(memory_space=pl.ANY)],
        out_specs=[
            q_spec,
            pl.BlockSpec((1, heads, 8, LANES), lambda b, h, i, pt: (flat(b, h, i) // scan_steps, 0, 0, 0)),
        ],
        scratch_shapes=[
            pltpu.VMEM((hps, seq, 2 * HEAD_DIM), BF16),
            pltpu.VMEM((hps, seq, 2 * HEAD_DIM), BF16),
            pltpu.VMEM((tq, tq), BF16),
            pltpu.VMEM((hps, tq, seq), F32),
            pltpu.VMEM((nb_past * heads, HEAD_DIM), F32),
            pltpu.VMEM((SCAN_SLOTS, pps, page * heads, HEAD_DIM), F32),
            pltpu.SemaphoreType.DMA((SCAN_SLOTS,)),
        ],
    )
    return pl.pallas_call(
        functools.partial(_attn_scan_kernel, hps=hps, heads=heads, page=page, scan_steps=scan_steps),
        grid_spec=grid_spec,
        out_shape=[
            jax.ShapeDtypeStruct((rows, attn_dim), BF16),
            jax.ShapeDtypeStruct((nseq, heads, 8, LANES), jnp.int32),
        ],
        compiler_params=_params("arbitrary", "arbitrary", "arbitrary"),
        name="attn_scan",
    )(page_table, q, k, v, ksum.reshape(batch, nb, attn_dim), q_s, cache)


def _attn_sample_kernel(pt_ref, ti_ref, q_ref, kn_ref, vn_ref, ck_hbm, cv_hbm, o_ref, kbuf, vbuf, sem, *, heads):
    s = pl.program_id(0)
    h = pl.program_id(1)
    n = s * heads + h
    total = pl.num_programs(0) * heads
    t = q_ref.shape[1]
    nsel = t * MOBA_TOP_K
    page = cv_hbm.shape[1]
    ppb = MOBA_BLOCK // page

    def copies(s_, h_, slot_):
        out = []
        for c in range(nsel):
            blk = jnp.maximum(ti_ref[s_, h_ * nsel + c], 0)
            for e in range(ppb):
                pg = pt_ref[s_, blk * ppb + e]
                for src, dst in ((ck_hbm, kbuf), (cv_hbm, vbuf)):
                    out.append(pltpu.make_async_copy(
                        src.at[pg, :, h_, :], dst.at[slot_, c * ppb + e], sem.at[slot_]))
        return out

    n_slots = vbuf.shape[0]
    slot = n % n_slots

    def start_step(m):
        for cp in copies(m // heads, m % heads, m % n_slots):
            cp.start()

    @pl.when(n == 0)
    def _():
        for ahead in range(n_slots - 1):
            start_step(ahead)

    @pl.when(n + n_slots - 1 < total)
    def _():
        start_step(n + n_slots - 1)

    for cp in copies(s, h, slot):
        cp.wait()

    qs = q_ref[0] * SOFTMAX_SCALE
    kn = kn_ref[0]
    vn = vn_ref[0]
    width = nsel * MOBA_BLOCK
    seg_w = MOBA_TOP_K * MOBA_BLOCK
    q16 = jnp.concatenate([qs, jnp.zeros((16 - t, HEAD_DIM), F32)], axis=0).astype(BF16)
    ksel = kbuf[slot].reshape(width, HEAD_DIM).astype(BF16)
    lt = _nt_dot(q16, ksel)[0:t]
    segs = []
    for c in range(nsel):
        ok = ti_ref[s, h * nsel + c] >= 0
        segs.append(jnp.where(ok, lt[:, c * MOBA_BLOCK:(c + 1) * MOBA_BLOCK], MASKED))
    lm = jnp.concatenate(segs, axis=1)
    col = lax.broadcasted_iota(jnp.int32, (t, width), 1)
    row = lax.broadcasted_iota(jnp.int32, (t, width), 0)
    lm = jnp.where((col >= row * seg_w) & (col < (row + 1) * seg_w), lm, MASKED)
    row_o = lax.broadcasted_iota(jnp.int32, (t, LANES), 0)
    col_o = lax.broadcasted_iota(jnp.int32, (t, LANES), 1)
    own = jnp.full((t, LANES), MASKED, F32)
    for j in range(t):
        dj = jnp.sum(qs * kn[j:j + 1, :], axis=1, keepdims=True)
        own = jnp.where((col_o == j) & (j <= row_o), dj, own)
    full = jnp.concatenate([lm, own], axis=1)
    m = jnp.max(full, axis=1, keepdims=True)
    p = jnp.exp(full - m)
    l = jnp.sum(p, axis=1, keepdims=True)
    p_pad = jnp.concatenate([p[:, :width], jnp.zeros((8 - t, width), F32)], axis=0).astype(BF16)
    vsel = vbuf[slot].reshape(width, HEAD_DIM).astype(BF16)
    pv = jnp.dot(p_pad, vsel, preferred_element_type=F32)[0:t]
    for j in range(t):
        pv += p[:, width + j:width + j + 1] * vn[j:j + 1, :]
    o_ref[0] = pv / l


def _attn_sample(page_table, top_idx, q_s, k_s, v_s, cache_k, cache_v):
    nseq, t, attn_dim = q_s.shape
    heads = attn_dim // HEAD_DIM
    page = cache_v.shape[1]
    nsel = t * MOBA_TOP_K
    ppb = MOBA_BLOCK // page
    new_spec = pl.BlockSpec((1, t, HEAD_DIM), lambda s, h, pt, ti: (s, 0, h))
    grid_spec = pltpu.PrefetchScalarGridSpec(
        num_scalar_prefetch=2,
        grid=(nseq, heads),
        in_specs=[new_spec, new_spec, new_spec,
                  pl.BlockSpec(memory_space=pl.ANY), pl.BlockSpec(memory_space=pl.ANY)],
        out_specs=new_spec,
        scratch_shapes=[
            pltpu.VMEM((GATHER_SLOTS, nsel * ppb, page, HEAD_DIM), F32),
            pltpu.VMEM((GATHER_SLOTS, nsel * ppb, page, HEAD_DIM), F32),
            pltpu.SemaphoreType.DMA((GATHER_SLOTS,)),
        ],
    )
    return pl.pallas_call(
        functools.partial(_attn_sample_kernel, heads=heads),
        grid_spec=grid_spec,
        out_shape=jax.ShapeDtypeStruct((nseq, t, attn_dim), F32),
        compiler_params=_params("arbitrary", "arbitrary"),
        name="attn_sample",
    )(page_table, top_idx, q_s, k_s, v_s, cache_k, cache_v)


def _post_attn_kernel(x_ref, o_ref, mod_ref, g_ref, wo_ref, wup_ref, wdown_ref, y_ref):
    d = x_ref.shape[1]
    m = mod_ref[0]
    a = jnp.dot(o_ref[...].astype(BF16), wo_ref[...], preferred_element_type=F32)
    x1 = x_ref[...] + m[:, 2 * d:3 * d] * a
    h = _norm_mod(x1, g_ref[...], m[:, 3 * d:4 * d], m[:, 4 * d:5 * d])
    y_ref[...] = x1 + m[:, 5 * d:6 * d] * _mlp(h, wup_ref, wdown_ref)


def _post_attn(x, o, mod, g, wo, wup, wdown, *, tm, tiles_per_seq):
    rows, d = x.shape
    row_spec = pl.BlockSpec((tm, d), lambda i: (i, 0))
    return pl.pallas_call(
        _post_attn_kernel,
        grid=(rows // tm,),
        in_specs=[
            row_spec,
            pl.BlockSpec((tm, o.shape[1]), lambda i: (i, 0)),
            pl.BlockSpec((1, mod.shape[1], mod.shape[2]), lambda i: (i // tiles_per_seq, 0, 0)),
            _const_spec((1, d)),
            _const_spec(wo.shape), _const_spec(wup.shape), _const_spec(wdown.shape),
        ],
        out_specs=row_spec,
        out_shape=jax.ShapeDtypeStruct((rows, d), F32),
        compiler_params=_params("arbitrary"),
        name="post_attn",
    )(x, o, mod, g, wo, wup, wdown)


def _conv_layer_kernel(x_ref, mod_ref, g1_ref, g2_ref, gf_ref, wbcx_ref, wconv_ref, wout_ref, wup_ref, wdown_ref,
                       *rest, tiles_per_seq, period, tail):
    if period is None:
        y_ref, ut_ref, ext = rest
    else:
        e1_ref, e2_ref, y_ref, ut_ref, ext = rest
    i = pl.program_id(0)
    tm, d = x_ref.shape
    halo = 8
    m = mod_ref[0]
    x = x_ref[...]
    h = _norm_mod(x, g1_ref[...], m[:, 0:d], m[:, d:2 * d]).astype(BF16)
    bg = jnp.dot(h, wbcx_ref[:, 0:d], preferred_element_type=F32)
    cg = jnp.dot(h, wbcx_ref[:, d:2 * d], preferred_element_type=F32)
    xin = jnp.dot(h, wbcx_ref[:, 2 * d:3 * d], preferred_element_type=F32)
    u = cg * xin

    @pl.when(i % tiles_per_seq == 0)
    def _():
        ext[0:halo, :] = jnp.zeros((halo, d), F32)

    ext[halo:halo + tm, :] = u
    prev1 = ext[halo - 1:halo - 1 + tm, :]
    prev2 = ext[halo - 2:halo - 2 + tm, :]
    if period is not None:
        tpos = lax.broadcasted_iota(jnp.int32, (tm, d), 0) % period
        prev1 = jnp.where(tpos < 1, e1_ref[...], prev1)
        prev2 = jnp.where(tpos < 2, e2_ref[...], prev2)
    w = wconv_ref[...]
    conv = w[0:1, :] * prev2 + w[1:2, :] * prev1 + w[2:3, :] * u
    ext[0:halo, :] = ext[tm:tm + halo, :]
    ut_ref[0] = u[tm - tail:, :]

    y = jnp.dot((bg * conv).astype(BF16), wout_ref[...], preferred_element_type=F32)
    x1 = x + m[:, 2 * d:3 * d] * y
    h2 = _norm_mod(x1, g2_ref[...], m[:, 3 * d:4 * d], m[:, 4 * d:5 * d])
    x2 = x1 + m[:, 5 * d:6 * d] * _mlp(h2, wup_ref, wdown_ref)
    ms = jnp.mean(x2 * x2, axis=-1, keepdims=True)
    y_ref[...] = x2 * lax.rsqrt(ms + RMS_EPS) * gf_ref[...]


def _conv_layer(x, mod, g1, g2, gf, wbcx, wconv, wout, wup, wdown, state_rows, *, tm, tiles_per_seq, period, tail):
    rows, d = x.shape
    n_tiles = rows // tm
    row_spec = pl.BlockSpec((tm, d), lambda i: (i, 0))
    in_specs = [
        row_spec,
        pl.BlockSpec((1, mod.shape[1], mod.shape[2]), lambda i: (i // tiles_per_seq, 0, 0)),
        _const_spec((1, d)), _const_spec((1, d)), _const_spec((1, d)),
        _const_spec(wbcx.shape), _const_spec(wconv.shape), _const_spec(wout.shape),
        _const_spec(wup.shape), _const_spec(wdown.shape),
    ]
    args = [x, mod, g1, g2, gf, wbcx, wconv, wout, wup, wdown]
    if period is not None:
        in_specs += [row_spec, row_spec]
        args += list(state_rows)
    return pl.pallas_call(
        functools.partial(_conv_layer_kernel, tiles_per_seq=tiles_per_seq, period=period, tail=tail),
        grid=(n_tiles,),
        in_specs=in_specs,
        out_specs=[row_spec, pl.BlockSpec((1, tail, d), lambda i: (i, 0, 0))],
        out_shape=[jax.ShapeDtypeStruct((rows, d), F32), jax.ShapeDtypeStruct((n_tiles, tail, d), F32)],
        scratch_shapes=[pltpu.VMEM((tm + 8, d), F32)],
        compiler_params=_params("arbitrary"),
        name="conv_layer",
    )(*args)


def kernel(x_prompt, x_sample, cache_k, cache_v, state_conv, page_table, c_prompt, c_sample, w_ada, b_ada, g_norm,
           w_qkv, w_o, w_bcx, w_conv, w_out, w_up, w_down, g_final):
    batch, seq, d = x_prompt.shape
    nseq, t, _ = x_sample.shape
    depth = w_ada.shape[0]
    n_layers_attn, pool, page, heads, head_dim = cache_k.shape
    attn_dim = heads * head_dim
    n_pages = page_table.shape[1]
    past = n_pages * page
    kw = state_conv.shape[2] + 1
    assert depth == 2 and n_layers_attn == 1 and state_conv.shape[0] == 1, "layer 0 attention, layer 1 convolution"
    assert head_dim == HEAD_DIM and kw == 3 and t >= kw - 1
    assert seq % ROW_TILE == 0 and ROW_TILE % MOBA_BLOCK == 0 and past % MOBA_BLOCK == 0 and t <= MOBA_BLOCK
    rows_p, rows_s = batch * seq, nseq * t
    assert rows_s % 8 == 0

    n_c = batch + nseq
    pad = -n_c % 8
    c_all = jnp.concatenate([c_prompt, c_sample, jnp.zeros((pad, d), F32)], axis=0)
    mod = _ada(c_all, w_ada, b_ada)
    mod_p = mod[:, :batch, None, :]
    mod_s = jnp.repeat(mod[:, batch:n_c], t, axis=1)[:, None]

    bf = lambda a: a.astype(BF16)
    wqkv, wo, wbcx, wout = bf(w_qkv[0]), bf(w_o[0]), bf(w_bcx[0]), bf(w_out[0])
    wup, wdown = bf(w_up), bf(w_down)
    g = g_norm.reshape(depth, 2, 1, d)
    gf = g_final.reshape(1, d)

    xp = x_prompt.reshape(rows_p, d)
    xs = x_sample.reshape(rows_s, d)
    tps = seq // ROW_TILE
    tab_p = _rope_tables(np.arange(seq))
    tab_s = _rope_tables(past + (np.arange(rows_s) % t))

    q_p, k_p, v_p, kb_p, vb_p, ksum = _qkv(xp, mod_p[0], g[0, 0], wqkv, tab_p, tm=ROW_TILE, tiles_per_seq=tps,
                                           with_ksum=True)
    q_s, k_s, v_s, kb_s, vb_s = _qkv(xs, mod_s[0], g[0, 0], wqkv, tab_s, tm=rows_s, tiles_per_seq=1,
                                     with_ksum=False)
    q_s3, kn_s3, vn_s3 = (a.astype(F32).reshape(nseq, t, attn_dim) for a in (q_s, kb_s, vb_s))
    o_p, top = _attn_scan(page_table, q_p, kb_p, vb_p, ksum, q_s3,
                          cache_k.reshape(pool * page * heads, head_dim), batch=batch, seq=seq, page=page)
    top_idx = top[:, :, :t, :MOBA_TOP_K].reshape(nseq, heads * t * MOBA_TOP_K)
    o_s = _attn_sample(page_table, top_idx, q_s3, kn_s3, vn_s3, cache_k.reshape(pool, page, heads, head_dim),
                       cache_v.reshape(pool, page, heads, head_dim))

    x1_p = _post_attn(xp, o_p, mod_p[0], g[0, 1], wo, wup[0], wdown[0], tm=ROW_TILE, tiles_per_seq=tps)
    x1_s = _post_attn(xs, o_s.reshape(rows_s, attn_dim), mod_s[0], g[0, 1], wo, wup[0], wdown[0],
                      tm=rows_s, tiles_per_seq=1)

    y_p, ut_p = _conv_layer(x1_p, mod_p[1], g[1, 0], g[1, 1], gf, wbcx, w_conv[0], wout, wup[1], wdown[1], None,
                            tm=ROW_TILE, tiles_per_seq=tps, period=None, tail=8)
    st = state_conv[0]
    zeros = jnp.zeros((nseq, t - 1, d), F32)
    e1 = jnp.concatenate([st[:, 1:2], zeros], axis=1).reshape(rows_s, d)
    e2 = jnp.concatenate([st[:, 0:2], zeros[:, 1:]], axis=1).reshape(rows_s, d)
    y_s, ut_s = _conv_layer(x1_s, mod_s[1], g[1, 0], g[1, 1], gf, wbcx, w_conv[0], wout, wup[1], wdown[1], (e1, e2),
                            tm=rows_s, tiles_per_seq=1, period=t, tail=rows_s)

    conv_p = ut_p.reshape(batch, tps, 8, d)[:, -1, 8 - (kw - 1):, :]
    conv_s = ut_s.reshape(nseq, t, d)[:, t - (kw - 1):, :]
    return (
        y_p.reshape(batch, seq, d),
        y_s.reshape(nseq, t, d),
        k_p.reshape(1, batch, seq, heads, head_dim),
        v_p.reshape(1, batch, seq, heads, head_dim),
        k_s.reshape(1, nseq, t, heads, head_dim),
        v_s.reshape(1, nseq, t, heads, head_dim),
        conv_p[None],
        conv_s[None],
    )
```

```python
import functools

import numpy as np
import jax
import jax.numpy as jnp
from jax import lax
from jax.experimental import pallas as pl
from jax.experimental.pallas import tpu as pltpu

F32 = jnp.float32
BF16 = jnp.bfloat16

HEAD_DIM = 128
MOBA_BLOCK = 256
MOBA_TOP_K = 3
ROPE_DIM = HEAD_DIM // 4
ROPE_THETA = 500000.0
RMS_EPS = 1e-6
SOFTMAX_SCALE = HEAD_DIM ** -0.5
MASKED = -1e30

LANES = 128
VMEM_LIMIT = 56 * 1024 * 1024
ROW_TILE = 256
ADA_COL_TILE = 1536
SCAN_SLOTS = 3
GATHER_SLOTS = 2
ATTN_HEADS_PER_STEP = 2
ATTN_CHUNK_BLOCKS = 4

NT_DIMS = (((1,), (1,)), ((), ()))


def _nt_dot(a, b):
    return lax.dot_general(a, b, NT_DIMS, preferred_element_type=F32)


def _split_bf16(x):
    hi = x.astype(BF16)
    lo = (x - hi.astype(F32)).astype(BF16)
    return hi, lo


def _const_spec(shape):
    n = len(shape)
    return pl.BlockSpec(shape, lambda *_: (0,) * n, pipeline_mode=pl.Buffered(1))


def _params(*sem):
    return pltpu.CompilerParams(dimension_semantics=sem, vmem_limit_bytes=VMEM_LIMIT)


def _ada_kernel(c_ref, w_ref, b_ref, o_ref):
    c_hi, c_lo = _split_bf16(c_ref[...])
    w_hi, w_lo = _split_bf16(w_ref[0])
    acc = jnp.dot(c_hi, w_hi, preferred_element_type=F32)
    acc += jnp.dot(c_lo, w_hi, preferred_element_type=F32)
    acc += jnp.dot(c_hi, w_lo, preferred_element_type=F32)
    o_ref[0] = acc + b_ref[0]


def _ada(c_all, w_ada, b_ada):
    depth, d, n6 = w_ada.shape
    rows = c_all.shape[0]
    tn = ADA_COL_TILE
    return pl.pallas_call(
        _ada_kernel,
        grid=(depth, n6 // tn),
        in_specs=[
            pl.BlockSpec((rows, d), lambda l, n: (0, 0)),
            pl.BlockSpec((1, d, tn), lambda l, n: (l, 0, n)),
            pl.BlockSpec((1, 1, tn), lambda l, n: (l, 0, n)),
        ],
        out_specs=pl.BlockSpec((1, rows, tn), lambda l, n: (l, 0, n)),
        out_shape=jax.ShapeDtypeStruct((depth, rows, n6), F32),
        compiler_params=_params("arbitrary", "arbitrary"),
        name="ada",
    )(c_all, w_ada, b_ada.reshape(depth, 1, n6))


def _norm_mod(x, g, shift, scale):
    ms = jnp.mean(x * x, axis=-1, keepdims=True)
    y = x * lax.rsqrt(ms + RMS_EPS) * g
    return y * (1.0 + scale) + shift


def _mlp(h, wup_ref, wdown_ref, after_chunk=None):
    d, dff = wup_ref.shape
    hb = h.astype(BF16)
    acc = jnp.zeros((h.shape[0], d), F32)
    for i, c in enumerate(range(0, dff, d)):
        u = jnp.dot(hb, wup_ref[:, c:c + d], preferred_element_type=F32)
        u = jnp.square(jnp.maximum(u, 0.0)).astype(BF16)
        acc += jnp.dot(u, wdown_ref[c:c + d, :], preferred_element_type=F32)
        if after_chunk is not None:
            after_chunk(i)
    return acc


def _rope_tables(pos):
    half = ROPE_DIM // 2
    inv = np.float32(ROPE_THETA) ** (-np.arange(0, ROPE_DIM, 2, dtype=np.float32) / np.float32(ROPE_DIM))
    ang = (pos.astype(np.float32)[:, None] * inv.astype(np.float32)[None, :]).astype(np.float32)
    cos = np.cos(ang.astype(np.float64))
    sin = np.sin(ang.astype(np.float64))
    n = pos.shape[0]
    c = np.ones((n, HEAD_DIM), np.float64)
    s1 = np.zeros((n, HEAD_DIM), np.float64)
    s2 = np.zeros((n, HEAD_DIM), np.float64)
    c[:, :half] = cos
    c[:, half:ROPE_DIM] = cos
    s1[:, :half] = -sin
    s2[:, half:ROPE_DIM] = sin
    return tuple(jnp.asarray(t.astype(np.float32)) for t in (c, s1, s2))


def _qkv_kernel(x_ref, mod_ref, g_ref, w_ref, cos_ref, s1_ref, s2_ref, q_ref, k5_ref, v5_ref, kb_ref, vb_ref,
                *ks_ref, attn_dim):
    d = x_ref.shape[1]
    m = mod_ref[0]
    h = _norm_mod(x_ref[...], g_ref[...], m[:, 0:d], m[:, d:2 * d]).astype(BF16)
    cos, s1, s2 = cos_ref[...], s1_ref[...], s2_ref[...]
    tm = h.shape[0]
    heads = attn_dim // HEAD_DIM
    cw = 2 * HEAD_DIM
    for c in range(0, 3 * attn_dim, cw):
        r = jnp.dot(h, w_ref[:, c:c + cw], preferred_element_type=F32)
        if c < 2 * attn_dim:
            rot = []
            for u in range(0, cw, HEAD_DIM):
                z = r[:, u:u + HEAD_DIM]
                z = (z * cos + pltpu.roll(z, HEAD_DIM - ROPE_DIM // 2, 1) * s1
                     + pltpu.roll(z, ROPE_DIM // 2, 1) * s2)
                rot.append(z)
            r = jnp.concatenate(rot, axis=1)
        if c < attn_dim:
            q_ref[:, c:c + cw] = r
            continue
        is_k = c < 2 * attn_dim
        c0 = c - (attn_dim if is_k else 2 * attn_dim)
        (kb_ref if is_k else vb_ref)[:, c0:c0 + cw] = r.astype(BF16)
        for u in range(0, cw, HEAD_DIM):
            head = (c0 + u) // HEAD_DIM
            (k5_ref if is_k else v5_ref)[pl.ds(head, tm, stride=heads), :] = r[:, u:u + HEAD_DIM]
        if is_k and ks_ref:
            nblk = tm // MOBA_BLOCK
            ks_ref[0][0, :, c0:c0 + cw] = jnp.sum(r.reshape(nblk, MOBA_BLOCK, cw), axis=1)


def _qkv(x, mod, g, w, tables, *, tm, tiles_per_seq, with_ksum):
    rows, d = x.shape
    attn_dim = w.shape[1] // 3
    heads = attn_dim // HEAD_DIM
    n_tiles = rows // tm
    mod_rows = mod.shape[1]
    row_spec = pl.BlockSpec((tm, attn_dim), lambda i: (i, 0))
    tok_head_spec = pl.BlockSpec((tm * heads, HEAD_DIM), lambda i: (i, 0))
    tab_spec = pl.BlockSpec((tm, HEAD_DIM), lambda i: (i % tiles_per_seq, 0))
    out_specs = [row_spec, tok_head_spec, tok_head_spec, row_spec, row_spec]
    out_shape = ([jax.ShapeDtypeStruct((rows, attn_dim), F32)]
                 + [jax.ShapeDtypeStruct((rows * heads, HEAD_DIM), F32)] * 2
                 + [jax.ShapeDtypeStruct((rows, attn_dim), BF16)] * 2)
    if with_ksum:
        nblk = tm // MOBA_BLOCK
        out_specs.append(pl.BlockSpec((1, nblk, attn_dim), lambda i: (i, 0, 0)))
        out_shape.append(jax.ShapeDtypeStruct((n_tiles, nblk, attn_dim), F32))
    return pl.pallas_call(
        functools.partial(_qkv_kernel, attn_dim=attn_dim),
        grid=(n_tiles,),
        in_specs=[
            pl.BlockSpec((tm, d), lambda i: (i, 0)),
            pl.BlockSpec((1, mod_rows, mod.shape[2]), lambda i: (i // tiles_per_seq, 0, 0)),
            _const_spec((1, d)),
            _const_spec(w.shape),
            tab_spec, tab_spec, tab_spec,
        ],
        out_specs=out_specs,
        out_shape=out_shape,
        compiler_params=_params("arbitrary"),
        name="qkv",
    )(x, mod, g, w, *tables)


def _attn_prompt_step(i, q_ref, k_ref, v_ref, ks_ref, o_ref, kaug, vaug, eye, s_sc, *, hps, chunk_blocks, side_work):
    tq = q_ref.shape[0]
    seq = k_ref.shape[0]
    nb = seq // MOBA_BLOCK
    ck = chunk_blocks * MOBA_BLOCK
    n_chunk_max = nb // chunk_blocks

    @pl.when(i == 0)
    def _():
        r_id = lax.broadcasted_iota(jnp.int32, (tq, tq), 0)
        c_id = lax.broadcasted_iota(jnp.int32, (tq, tq), 1)
        eye[...] = jnp.where(r_id == c_id, 1.0, 0.0).astype(BF16)
        lane = lax.broadcasted_iota(jnp.int32, (MOBA_BLOCK, LANES), 1)
        ones_col = jnp.where(lane == 0, 1.0, 0.0).astype(BF16)
        for hh in range(hps):
            cols = slice(hh * HEAD_DIM, (hh + 1) * HEAD_DIM)

            def fill(j, carry):
                rows = pl.ds(pl.multiple_of(j * MOBA_BLOCK, MOBA_BLOCK), MOBA_BLOCK)
                kaug[hh, rows, 0:HEAD_DIM] = k_ref[rows, cols]
                kaug[hh, rows, HEAD_DIM:] = jnp.where(lane == j, 1.0, 0.0).astype(BF16)
                vaug[hh, rows, 0:HEAD_DIM] = v_ref[rows, cols]
                vaug[hh, rows, HEAD_DIM:] = ones_col
                return carry
            lax.fori_loop(0, nb, fill, 0)

    own = pl.ds(pl.multiple_of(i * MOBA_BLOCK, MOBA_BLOCK), MOBA_BLOCK)
    head_cols = [slice(hh * HEAD_DIM, (hh + 1) * HEAD_DIM) for hh in range(hps)]

    def select(hh):
        q = q_ref[:, head_cols[hh]]
        qs = (q * (SOFTMAX_SCALE * np.log2(np.e))).astype(BF16)
        kmean = (ks_ref[0][:, head_cols[hh]] * (1.0 / MOBA_BLOCK)).astype(BF16)
        gate = _nt_dot(kmean, q.astype(BF16))
        blk = lax.broadcasted_iota(jnp.int32, (nb, tq), 0)
        blk_f = blk.astype(F32)
        gate = jnp.where(blk < i, gate, -jnp.inf)
        sel = jnp.zeros((nb, tq), F32)
        for _ in range(MOBA_TOP_K):
            mx = jnp.max(gate, axis=0, keepdims=True)
            first = jnp.min(jnp.where(gate == mx, blk_f, float(nb)), axis=0, keepdims=True)
            hit = blk_f == first
            sel = jnp.where(hit & (mx > -jnp.inf), 1.0, sel)
            gate = jnp.where(hit, -jnp.inf, gate)
        sel_pad = jnp.concatenate([sel, jnp.zeros((LANES - nb, tq), F32)], axis=0).astype(BF16)
        sel_q = _nt_dot(eye[...], sel_pad)
        bias = jnp.where(sel_q > 0.5, 0.0, MASKED).astype(BF16)
        return jnp.concatenate([qs, bias], axis=1)

    def variant(n_chunks):
        heads_ = range(hps)
        qa = [select(hh) for hh in heads_]
        r_id = lax.broadcasted_iota(jnp.int32, (tq, MOBA_BLOCK), 0)
        c_id = lax.broadcasted_iota(jnp.int32, (tq, MOBA_BLOCK), 1)
        s_own = [jnp.where(c_id <= r_id, _nt_dot(qa[hh][:, 0:HEAD_DIM], kaug[hh, own, 0:HEAD_DIM]), MASKED)
                 for hh in heads_]
        run = [jnp.maximum(s[:, 0:LANES], s[:, LANES:]) for s in s_own]
        for c in range(n_chunks):
            for hh in heads_:
                s_c = _nt_dot(qa[hh], kaug[hh, c * ck:(c + 1) * ck, :])
                s_sc[hh, :, c * ck:(c + 1) * ck] = s_c
                for u in range(0, ck, LANES):
                    run[hh] = jnp.maximum(run[hh], s_c[:, u:u + LANES])
        m = [jnp.max(r, axis=1, keepdims=True) for r in run]
        acc = [jnp.dot(jnp.exp2(s_own[hh] - m[hh]).astype(BF16), vaug[hh, own, :], preferred_element_type=F32)
               for hh in heads_]
        for c in range(n_chunks):
            for hh in heads_:
                p_c = jnp.exp2(s_sc[hh, :, c * ck:(c + 1) * ck] - m[hh]).astype(BF16)
                acc[hh] += jnp.dot(p_c, vaug[hh, c * ck:(c + 1) * ck, :], preferred_element_type=F32)
        for hh in heads_:
            o_ref[:, head_cols[hh]] = (acc[hh][:, 0:HEAD_DIM] / acc[hh][:, HEAD_DIM:HEAD_DIM + 1]).astype(o_ref.dtype)

    n_needed = (i + chunk_blocks - 1) // chunk_blocks
    for n in range(n_chunk_max + 1):
        @pl.when(n_needed == n)
        def _(n=n):
            side_work()
            variant(n)


def _scan_queries(q_ref):
    t = q_ref.shape[1]
    qs = q_ref[0] * SOFTMAX_SCALE
    q_hi = qs.astype(BF16).astype(F32)
    q_lo = (qs - q_hi).astype(BF16).astype(F32)
    return jnp.concatenate([q_hi, q_lo, jnp.zeros((16 - 2 * t, qs.shape[1]), F32)], axis=0).astype(BF16)


def _cache_scan_pages(p, page_refs, ksum_sc, *, heads, page):
    pps = len(page_refs)
    ppb = MOBA_BLOCK // page
    bps = pps // ppb
    sums = [jnp.sum(ref[...].reshape(page, heads, HEAD_DIM), axis=0) for ref in page_refs]
    for b in range(bps):
        blk_sum = functools.reduce(lambda x, y: x + y, sums[b * ppb:(b + 1) * ppb])
        ksum_sc[pl.ds(pl.multiple_of((p * bps + b) * heads, heads), heads), :] = blk_sum


def _cache_scan_gate(q_ref, ti_ref, ksum_sc, *, heads):
    t = q_ref.shape[1]
    q16 = _scan_queries(q_ref)
    nb = ksum_sc.shape[0] // heads
    lane = lax.broadcasted_iota(jnp.int32, (16, LANES), 1)
    lane_f = lane.astype(F32)
    for h in range(heads):
        kmean = ksum_sc[pl.ds(h, nb, stride=heads), :] * (1.0 / MOBA_BLOCK)
        km = jnp.concatenate([kmean, jnp.zeros((LANES - nb, HEAD_DIM), F32)], axis=0)
        km_hi, km_lo = _split_bf16(km)
        q_h = q16[:, h * HEAD_DIM:(h + 1) * HEAD_DIM]
        r1 = _nt_dot(q_h, km_hi)
        r2 = _nt_dot(q_h, km_lo)
        gate = r1 + pltpu.roll(r1, 16 - t, 0) + r2
        gate = jnp.where(lane < nb, gate, -jnp.inf)
        out = jnp.zeros((16, LANES), jnp.int32)
        for kk in range(MOBA_TOP_K):
            mx = jnp.max(gate, axis=1, keepdims=True)
            first = jnp.min(jnp.where(gate == mx, lane_f, float(LANES)), axis=1, keepdims=True)
            idx = jnp.where(mx > -jnp.inf, first, -1.0).astype(jnp.int32)
            out = jnp.where(lane == kk, idx, out)
            gate = jnp.where(lane_f == first, -jnp.inf, gate)
        ti_ref[0, h] = out[0:8]


def _attn_scan_kernel(pt_ref, q_ref, k_ref, v_ref, ks_ref, qs_ref, cache_hbm, o_ref, ti_ref,
                      kaug, vaug, eye, s_sc, ksum_sc, pbuf, psem, *, hps, heads, page, scan_steps):
    n_slots, pps = pbuf.shape[0], pbuf.shape[1]
    page_rows = page * heads
    i = pl.program_id(2)
    flat = (pl.program_id(0) * pl.num_programs(1) + pl.program_id(1)) * pl.num_programs(2) + i
    total = pl.num_programs(0) * pl.num_programs(1) * pl.num_programs(2)
    p = flat % scan_steps
    slot = flat % n_slots

    def page_copies(step):
        seq_id, first = step // scan_steps, (step % scan_steps) * pps
        return [pltpu.make_async_copy(
            cache_hbm.at[pl.ds(pl.multiple_of(pt_ref[seq_id, first + u] * page_rows, page_rows), page_rows), :],
            pbuf.at[step % n_slots, u], psem.at[step % n_slots]) for u in range(pps)]

    @pl.when(flat == 0)
    def _():
        for ahead in range(n_slots - 1):
            for cp in page_copies(ahead):
                cp.start()

    @pl.when(flat + n_slots - 1 < total)
    def _():
        for cp in page_copies(flat + n_slots - 1):
            cp.start()

    for cp in page_copies(flat):
        cp.wait()
    page_refs = [pbuf.at[slot, u] for u in range(pps)]

    def scan_pages():
        _cache_scan_pages(p, page_refs, ksum_sc, heads=heads, page=page)

    _attn_prompt_step(i, q_ref, k_ref, v_ref, ks_ref, o_ref, kaug, vaug, eye, s_sc,
                      hps=hps, chunk_blocks=ATTN_CHUNK_BLOCKS, side_work=scan_pages)

    @pl.when(p == scan_steps - 1)
    def _():
        _cache_scan_gate(qs_ref, ti_ref, ksum_sc, heads=heads)


def _attn_scan(page_table, q, k, v, ksum, q_s, cache, *, batch, seq, page):
    rows, attn_dim = q.shape
    heads = attn_dim // HEAD_DIM
    tq = MOBA_BLOCK
    nq = seq // tq
    nb = seq // MOBA_BLOCK
    hps = ATTN_HEADS_PER_STEP
    n_hp = heads // hps
    assert nb <= LANES and nb % ATTN_CHUNK_BLOCKS == 0 and heads % hps == 0

    nseq, n_pages = page_table.shape
    t = q_s.shape[1]
    past = n_pages * page
    nb_past = past // MOBA_BLOCK
    n_steps = batch * n_hp * nq
    assert n_steps % nseq == 0
    scan_steps = n_steps // nseq
    assert n_pages % scan_steps == 0
    pps = n_pages // scan_steps
    assert (pps * page) % MOBA_BLOCK == 0 and MOBA_BLOCK % page == 0
    assert nb_past <= LANES and 2 * t <= 8 and (pps * page // MOBA_BLOCK) % 8 == 0

    def flat(b, h, i):
        return (b * n_hp + h) * nq + i

    q_spec = pl.BlockSpec((tq, hps * HEAD_DIM), lambda b, h, i, pt: (b * nq + i, h))
    kv_spec = pl.BlockSpec((seq, hps * HEAD_DIM), lambda b, h, i, pt: (b, h))
    assert n_steps >= SCAN_SLOTS
    grid_spec = pltpu.PrefetchScalarGridSpec(
        num_scalar_prefetch=1,
        grid=(batch, n_hp, nq),
        in_specs=[q_spec, kv_spec, kv_spec,
                  pl.BlockSpec((1, nb, hps * HEAD_DIM), lambda b, h, i, pt: (b, 0, h)),
                  pl.BlockSpec((1, t, attn_dim), lambda b, h, i, pt: (flat(b, h, i) // scan_steps, 0, 0)),
                  pl.BlockSpec(memory_space=pl.ANY)],
        out_specs=[
            q_spec,
            pl.BlockSpec((1, heads, 8, LANES), lambda b, h, i, pt: (flat(b, h, i) // scan_steps, 0, 0, 0)),
        ],
        scratch_shapes=[
            pltpu.VMEM((hps, seq, 2 * HEAD_DIM), BF16),
            pltpu.VMEM((hps, seq, 2 * HEAD_DIM), BF16),
            pltpu.VMEM((tq, tq), BF16),
            pltpu.VMEM((hps, tq, seq), F32),
            pltpu.VMEM((nb_past * heads, HEAD_DIM), F32),
            pltpu.VMEM((SCAN_SLOTS, pps, page * heads, HEAD_DIM), F32),
            pltpu.SemaphoreType.DMA((SCAN_SLOTS,)),
        ],
    )
    return pl.pallas_call(
        functools.partial(_attn_scan_kernel, hps=hps, heads=heads, page=page, scan_steps=scan_steps),
        grid_spec=grid_spec,
        out_shape=[
            jax.ShapeDtypeStruct((rows, attn_dim), BF16),
            jax.ShapeDtypeStruct((nseq, heads, 8, LANES), jnp.int32),
        ],
        compiler_params=_params("arbitrary", "arbitrary", "arbitrary"),
        name="attn_scan",
    )(page_table, q, k, v, ksum.reshape(batch, nb, attn_dim), q_s, cache)


def _gather_copies(pt_ref, ti_ref, ck_hbm, cv_hbm, kbuf, vbuf, sem, step, slot, k, *, units, heads, nsel,
                   wait_only=False):
    page = cv_hbm.shape[1]
    ppb = MOBA_BLOCK // page
    unit = step * units + k
    s_, h_ = unit // heads, unit % heads
    out = []
    for c in range(nsel):
        blk = None if wait_only else jnp.maximum(ti_ref[s_, h_ * nsel + c], 0)
        for e in range(ppb):
            pg = 0 if wait_only else pt_ref[s_, blk * ppb + e]
            for src, dst in ((ck_hbm, kbuf), (cv_hbm, vbuf)):
                out.append(pltpu.make_async_copy(
                    src.at[pg, :, 0 if wait_only else h_, :], dst.at[slot, (k * nsel + c) * ppb + e],
                    sem.at[slot, k]))
    return out


def _sample_attend_probs(q, kn, kblk, oks):
    t = q.shape[0]
    nsel = len(oks)
    qs = q * SOFTMAX_SCALE
    width = nsel * MOBA_BLOCK
    seg_w = MOBA_TOP_K * MOBA_BLOCK
    q16 = jnp.concatenate([qs, jnp.zeros((16 - t, HEAD_DIM), F32)], axis=0).astype(BF16)
    lt = _nt_dot(q16, kblk.astype(BF16))[0:t]
    segs = [jnp.where(oks[c], lt[:, c * MOBA_BLOCK:(c + 1) * MOBA_BLOCK], MASKED) for c in range(nsel)]
    lm = jnp.concatenate(segs, axis=1)
    col = lax.broadcasted_iota(jnp.int32, (t, width), 1)
    row = lax.broadcasted_iota(jnp.int32, (t, width), 0)
    lm = jnp.where((col >= row * seg_w) & (col < (row + 1) * seg_w), lm, MASKED)
    row_o = lax.broadcasted_iota(jnp.int32, (t, LANES), 0)
    col_o = lax.broadcasted_iota(jnp.int32, (t, LANES), 1)
    own = jnp.full((t, LANES), MASKED, F32)
    for j in range(t):
        dj = jnp.sum(qs * kn[j:j + 1, :], axis=1, keepdims=True)
        own = jnp.where((col_o == j) & (j <= row_o), dj, own)
    full = jnp.concatenate([lm, own], axis=1)
    m = jnp.max(full, axis=1, keepdims=True)
    p = jnp.exp(full - m)
    return p, jnp.sum(p, axis=1, keepdims=True)


def _sample_attend_values(p, l, vn, vblk):
    t = vn.shape[0]
    width = vblk.shape[0]
    p_pad = jnp.concatenate([p[:, :width], jnp.zeros((8 - t, width), F32)], axis=0).astype(BF16)
    pv = jnp.dot(p_pad, vblk.astype(BF16), preferred_element_type=F32)[0:t]
    for j in range(t):
        pv += p[:, width + j:width + j + 1] * vn[j:j + 1, :]
    return pv / l


def _post_attn_kernel(*refs, gather):
    if gather:
        pt_ref, ti_ref, *refs = refs
    x_ref, o_ref, mod_ref, g_ref, wo_ref, wup_ref, wdown_ref, *refs = refs
    if gather:
        qs_ref, kn_ref, vn_ref, ck_hbm, cv_hbm, y_ref, os_ref, kbuf, vbuf, sem = refs
        units, heads = gather["units"], gather["heads"]
        t = qs_ref.shape[1]
        nsel = t * MOBA_TOP_K
        assert kbuf.shape[0] == 2
        j = pl.program_id(0)
        last = pl.num_programs(0) - 1
        slot = j % 2
        copies = functools.partial(_gather_copies, pt_ref, ti_ref, ck_hbm, cv_hbm, kbuf, vbuf, sem,
                                   units=units, heads=heads, nsel=nsel)

        @pl.when(j == 0)
        def _():
            for k in range(units):
                for cp in copies(0, 0, k):
                    cp.start()

        rows_per_unit = kbuf.shape[1] // units
        probs = {}
        ahead = [copies(jnp.minimum(j + 1, last), 1 - slot, k) for k in range(units)]
        landed = [copies(j, slot, k, wait_only=True) for k in range(units)]
        valid = [[ti_ref[(j * units + k) // heads, ((j * units + k) % heads) * nsel + c] >= 0
                  for c in range(nsel)] for k in range(units)]

        def begin(k):
            for cp in ahead[k]:
                cp.start()
            for cp in landed[k]:
                cp.wait()
            cols = slice(k * HEAD_DIM, (k + 1) * HEAD_DIM)
            kblk = kbuf[slot, k * rows_per_unit:(k + 1) * rows_per_unit].reshape(nsel * MOBA_BLOCK, HEAD_DIM)
            probs[k] = _sample_attend_probs(qs_ref[0][:, cols], kn_ref[0][:, cols], kblk, valid[k])

        def finish(k):
            cols = slice(k * HEAD_DIM, (k + 1) * HEAD_DIM)
            vblk = vbuf[slot, k * rows_per_unit:(k + 1) * rows_per_unit].reshape(nsel * MOBA_BLOCK, HEAD_DIM)
            os_ref[0, :, cols] = _sample_attend_values(*probs.pop(k), vn_ref[0][:, cols], vblk)

        def after_chunk(i):
            if 0 < i <= units:
                finish(i - 1)
            if i + 1 < units:
                begin(i + 1)
            if i == n_chunks - 1:
                for k in sorted(probs):
                    finish(k)
    else:
        (y_ref,) = refs
        units = 0
        begin = after_chunk = None
    d = x_ref.shape[1]
    n_chunks = wup_ref.shape[1] // d
    m = mod_ref[0]
    a = jnp.dot(o_ref[...].astype(BF16), wo_ref[...], preferred_element_type=F32)
    x1 = x_ref[...] + m[:, 2 * d:3 * d] * a
    h = _norm_mod(x1, g_ref[...], m[:, 3 * d:4 * d], m[:, 4 * d:5 * d])
    if units > 0:
        begin(0)
    y_ref[...] = x1 + m[:, 5 * d:6 * d] * _mlp(h, wup_ref, wdown_ref, after_chunk)
    if gather:
        assert not probs and units <= n_chunks, "more sample-attention units than MLP chunks"

        @pl.when(j == last)
        def _():
            for group in ahead:
                for cp in group:
                    cp.wait()


def _post_attn(x, o, mod, g, wo, wup, wdown, *, tm, tiles_per_seq, sample=None):
    rows, d = x.shape
    n_tiles = rows // tm
    row_spec = pl.BlockSpec((tm, d), lambda i, *_: (i, 0))
    in_specs = [
        row_spec,
        pl.BlockSpec((tm, o.shape[1]), lambda i, *_: (i, 0)),
        pl.BlockSpec((1, mod.shape[1], mod.shape[2]), lambda i, *_: (i // tiles_per_seq, 0, 0)),
        _const_spec((1, d)),
        _const_spec(wo.shape), _const_spec(wup.shape), _const_spec(wdown.shape),
    ]
    args = [x, o, mod, g, wo, wup, wdown]
    if sample is None:
        return pl.pallas_call(
            functools.partial(_post_attn_kernel, gather=None),
            grid=(n_tiles,),
            in_specs=in_specs,
            out_specs=row_spec,
            out_shape=jax.ShapeDtypeStruct((rows, d), F32),
            compiler_params=_params("arbitrary"),
            name="post_attn",
        )(*args)

    page_table, top_idx, q_s, k_s, v_s, cache_k, cache_v = sample
    nseq, t, attn_dim = q_s.shape
    heads = attn_dim // HEAD_DIM
    page = cache_v.shape[1]
    nsel = t * MOBA_TOP_K
    ppb = MOBA_BLOCK // page
    assert (nseq * heads) % n_tiles == 0
    units = nseq * heads // n_tiles
    assert heads % units == 0 and n_tiles >= GATHER_SLOTS
    groups = heads // units
    new_spec = pl.BlockSpec((1, t, units * HEAD_DIM), lambda i, *_: (i // groups, 0, i % groups))
    gather_buf = pltpu.VMEM((GATHER_SLOTS, units * nsel * ppb, page, HEAD_DIM), F32)
    grid_spec = pltpu.PrefetchScalarGridSpec(
        num_scalar_prefetch=2,
        grid=(n_tiles,),
        in_specs=in_specs + [new_spec, new_spec, new_spec,
                             pl.BlockSpec(memory_space=pl.ANY), pl.BlockSpec(memory_space=pl.ANY)],
        out_specs=[row_spec, new_spec],
        scratch_shapes=[gather_buf, gather_buf, pltpu.SemaphoreType.DMA((GATHER_SLOTS, units))],
    )
    return pl.pallas_call(
        functools.partial(_post_attn_kernel, gather=dict(units=units, heads=heads)),
        grid_spec=grid_spec,
        out_shape=[jax.ShapeDtypeStruct((rows, d), F32), jax.ShapeDtypeStruct((nseq, t, attn_dim), F32)],
        compiler_params=_params("arbitrary"),
        name="post_attn_gather",
    )(page_table, top_idx, *args, q_s, k_s, v_s, cache_k, cache_v)


def _conv_layer_kernel(x_ref, mod_ref, g1_ref, g2_ref, gf_ref, wbcx_ref, wconv_ref, wout_ref, wup_ref, wdown_ref,
                       *rest, tiles_per_seq, period, tail):
    if period is None:
        y_ref, ut_ref, ext = rest
    else:
        e1_ref, e2_ref, y_ref, ut_ref, ext = rest
    i = pl.program_id(0)
    tm, d = x_ref.shape
    halo = 8
    m = mod_ref[0]
    x = x_ref[...]
    h = _norm_mod(x, g1_ref[...], m[:, 0:d], m[:, d:2 * d]).astype(BF16)
    bg = jnp.dot(h, wbcx_ref[:, 0:d], preferred_element_type=F32)
    cg = jnp.dot(h, wbcx_ref[:, d:2 * d], preferred_element_type=F32)
    xin = jnp.dot(h, wbcx_ref[:, 2 * d:3 * d], preferred_element_type=F32)
    u = cg * xin

    @pl.when(i % tiles_per_seq == 0)
    def _():
        ext[0:halo, :] = jnp.zeros((halo, d), F32)

    ext[halo:halo + tm, :] = u
    prev1 = ext[halo - 1:halo - 1 + tm, :]
    prev2 = ext[halo - 2:halo - 2 + tm, :]
    if period is not None:
        tpos = lax.broadcasted_iota(jnp.int32, (tm, d), 0) % period
        prev1 = jnp.where(tpos < 1, e1_ref[...], prev1)
        prev2 = jnp.where(tpos < 2, e2_ref[...], prev2)
    w = wconv_ref[...]
    conv = w[0:1, :] * prev2 + w[1:2, :] * prev1 + w[2:3, :] * u
    ext[0:halo, :] = ext[tm:tm + halo, :]
    ut_ref[0] = u[tm - tail:, :]

    y = jnp.dot((bg * conv).astype(BF16), wout_ref[...], preferred_element_type=F32)
    x1 = x + m[:, 2 * d:3 * d] * y
    h2 = _norm_mod(x1, g2_ref[...], m[:, 3 * d:4 * d], m[:, 4 * d:5 * d])
    x2 = x1 + m[:, 5 * d:6 * d] * _mlp(h2, wup_ref, wdown_ref)
    ms = jnp.mean(x2 * x2, axis=-1, keepdims=True)
    y_ref[...] = x2 * lax.rsqrt(ms + RMS_EPS) * gf_ref[...]


def _conv_layer(x, mod, g1, g2, gf, wbcx, wconv, wout, wup, wdown, state_rows, *, tm, tiles_per_seq, period, tail):
    rows, d = x.shape
    n_tiles = rows // tm
    row_spec = pl.BlockSpec((tm, d), lambda i: (i, 0))
    in_specs = [
        row_spec,
        pl.BlockSpec((1, mod.shape[1], mod.shape[2]), lambda i: (i // tiles_per_seq, 0, 0)),
        _const_spec((1, d)), _const_spec((1, d)), _const_spec((1, d)),
        _const_spec(wbcx.shape), _const_spec(wconv.shape), _const_spec(wout.shape),
        _const_spec(wup.shape), _const_spec(wdown.shape),
    ]
    args = [x, mod, g1, g2, gf, wbcx, wconv, wout, wup, wdown]
    if period is not None:
        in_specs += [row_spec, row_spec]
        args += list(state_rows)
    return pl.pallas_call(
        functools.partial(_conv_layer_kernel, tiles_per_seq=tiles_per_seq, period=period, tail=tail),
        grid=(n_tiles,),
        in_specs=in_specs,
        out_specs=[row_spec, pl.BlockSpec((1, tail, d), lambda i: (i, 0, 0))],
        out_shape=[jax.ShapeDtypeStruct((rows, d), F32), jax.ShapeDtypeStruct((n_tiles, tail, d), F32)],
        scratch_shapes=[pltpu.VMEM((tm + 8, d), F32)],
        compiler_params=_params("arbitrary"),
        name="conv_layer",
    )(*args)


def kernel(x_prompt, x_sample, cache_k, cache_v, state_conv, page_table, c_prompt, c_sample, w_ada, b_ada, g_norm,
           w_qkv, w_o, w_bcx, w_conv, w_out, w_up, w_down, g_final):
    batch, seq, d = x_prompt.shape
    nseq, t, _ = x_sample.shape
    depth = w_ada.shape[0]
    n_layers_attn, pool, page, heads, head_dim = cache_k.shape
    attn_dim = heads * head_dim
    n_pages = page_table.shape[1]
    past = n_pages * page
    kw = state_conv.shape[2] + 1
    assert depth == 2 and n_layers_attn == 1 and state_conv.shape[0] == 1, "layer 0 attention, layer 1 convolution"
    assert head_dim == HEAD_DIM and kw == 3 and t >= kw - 1
    assert seq % ROW_TILE == 0 and ROW_TILE % MOBA_BLOCK == 0 and past % MOBA_BLOCK == 0 and t <= MOBA_BLOCK
    rows_p, rows_s = batch * seq, nseq * t
    assert rows_s % 8 == 0

    n_c = batch + nseq
    pad = -n_c % 8
    c_all = jnp.concatenate([c_prompt, c_sample, jnp.zeros((pad, d), F32)], axis=0)
    mod = _ada(c_all, w_ada, b_ada)
    mod_p = mod[:, :batch, None, :]
    mod_s = jnp.repeat(mod[:, batch:n_c], t, axis=1)[:, None]

    bf = lambda a: a.astype(BF16)
    wqkv, wo, wbcx, wout = bf(w_qkv[0]), bf(w_o[0]), bf(w_bcx[0]), bf(w_out[0])
    wup, wdown = bf(w_up), bf(w_down)
    g = g_norm.reshape(depth, 2, 1, d)
    gf = g_final.reshape(1, d)

    xp = x_prompt.reshape(rows_p, d)
    xs = x_sample.reshape(rows_s, d)
    tps = seq // ROW_TILE
    tab_p = _rope_tables(np.arange(seq))
    tab_s = _rope_tables(past + (np.arange(rows_s) % t))

    q_p, k_p, v_p, kb_p, vb_p, ksum = _qkv(xp, mod_p[0], g[0, 0], wqkv, tab_p, tm=ROW_TILE, tiles_per_seq=tps,
                                           with_ksum=True)
    q_s, k_s, v_s, kb_s, vb_s = _qkv(xs, mod_s[0], g[0, 0], wqkv, tab_s, tm=rows_s, tiles_per_seq=1,
                                     with_ksum=False)
    q_s3, kn_s3, vn_s3 = (a.astype(F32).reshape(nseq, t, attn_dim) for a in (q_s, kb_s, vb_s))
    o_p, top = _attn_scan(page_table, q_p, kb_p, vb_p, ksum, q_s3,
                          cache_k.reshape(pool * page * heads, head_dim), batch=batch, seq=seq, page=page)
    top_idx = top[:, :, :t, :MOBA_TOP_K].reshape(nseq, heads * t * MOBA_TOP_K)
    sample = (page_table, top_idx, q_s3, kn_s3, vn_s3, cache_k.reshape(pool, page, heads, head_dim),
              cache_v.reshape(pool, page, heads, head_dim))
    x1_p, o_s = _post_attn(xp, o_p, mod_p[0], g[0, 1], wo, wup[0], wdown[0], tm=ROW_TILE, tiles_per_seq=tps,
                           sample=sample)
    x1_s = _post_attn(xs, o_s.reshape(rows_s, attn_dim), mod_s[0], g[0, 1], wo, wup[0], wdown[0],
                      tm=rows_s, tiles_per_seq=1)

    y_p, ut_p = _conv_layer(x1_p, mod_p[1], g[1, 0], g[1, 1], gf, wbcx, w_conv[0], wout, wup[1], wdown[1], None,
                            tm=ROW_TILE, tiles_per_seq=tps, period=None, tail=8)
    st = state_conv[0]
    zeros = jnp.zeros((nseq, t - 1, d), F32)
    e1 = jnp.concatenate([st[:, 1:2], zeros], axis=1).reshape(rows_s, d)
    e2 = jnp.concatenate([st[:, 0:2], zeros[:, 1:]], axis=1).reshape(rows_s, d)
    y_s, ut_s = _conv_layer(x1_s, mod_s[1], g[1, 0], g[1, 1], gf, wbcx, w_conv[0], wout, wup[1], wdown[1], (e1, e2),
                            tm=rows_s, tiles_per_seq=1, period=t, tail=rows_s)

    conv_p = ut_p.reshape(batch, tps, 8, d)[:, -1, 8 - (kw - 1):, :]
    conv_s = ut_s.reshape(nseq, t, d)[:, t - (kw - 1):, :]
    return (
        y_p.reshape(batch, seq, d),
        y_s.reshape(nseq, t, d),
        k_p.reshape(1, batch, seq, heads, head_dim),
        v_p.reshape(1, batch, seq, heads, head_dim),
        k_s.reshape(1, nseq, t, heads, head_dim),
        v_s.reshape(1, nseq, t, heads, head_dim),
        conv_p[None],
        conv_s[None],
    )
```

```python
import functools

import numpy as np
import jax
import jax.numpy as jnp
from jax import lax
from jax.experimental import pallas as pl
from jax.experimental.pallas import tpu as pltpu

F32 = jnp.float32
BF16 = jnp.bfloat16

HEAD_DIM = 128
MOBA_BLOCK = 256
MOBA_TOP_K = 3
ROPE_DIM = HEAD_DIM // 4
ROPE_THETA = 500000.0
RMS_EPS = 1e-6
SOFTMAX_SCALE = HEAD_DIM ** -0.5
MASKED = -1e30

LANES = 128
VMEM_LIMIT = 56 * 1024 * 1024
ROW_TILE = 256
WIDE_ROW_TILE = 512
CONV_SUB_ROWS = 256
ADA_COL_TILE = 1536
SCAN_SLOTS = 3
GATHER_SLOTS = 2
ATTN_HEADS_PER_STEP = 2
ATTN_CHUNK_BLOCKS = 2

NT_DIMS = (((1,), (1,)), ((), ()))


def _nt_dot(a, b):
    return lax.dot_general(a, b, NT_DIMS, preferred_element_type=F32)


def _split_bf16(x):
    hi = x.astype(BF16)
    lo = (x - hi.astype(F32)).astype(BF16)
    return hi, lo


def _const_spec(shape):
    n = len(shape)
    return pl.BlockSpec(shape, lambda *_: (0,) * n, pipeline_mode=pl.Buffered(1))


def _params(*sem):
    return pltpu.CompilerParams(dimension_semantics=sem, vmem_limit_bytes=VMEM_LIMIT)


def _ada_kernel(c_ref, w_ref, b_ref, o_ref):
    c_hi, c_lo = _split_bf16(c_ref[...])
    w_hi, w_lo = _split_bf16(w_ref[0])
    acc = jnp.dot(c_hi, w_hi, preferred_element_type=F32)
    acc += jnp.dot(c_lo, w_hi, preferred_element_type=F32)
    acc += jnp.dot(c_hi, w_lo, preferred_element_type=F32)
    o_ref[0] = acc + b_ref[0]


def _ada(c_all, w_ada, b_ada):
    depth, d, n6 = w_ada.shape
    rows = c_all.shape[0]
    tn = ADA_COL_TILE
    return pl.pallas_call(
        _ada_kernel,
        grid=(depth, n6 // tn),
        in_specs=[
            pl.BlockSpec((rows, d), lambda l, n: (0, 0)),
            pl.BlockSpec((1, d, tn), lambda l, n: (l, 0, n)),
            pl.BlockSpec((1, 1, tn), lambda l, n: (l, 0, n)),
        ],
        out_specs=pl.BlockSpec((1, rows, tn), lambda l, n: (l, 0, n)),
        out_shape=jax.ShapeDtypeStruct((depth, rows, n6), F32),
        compiler_params=_params("arbitrary", "arbitrary"),
        name="ada",
    )(c_all, w_ada, b_ada.reshape(depth, 1, n6))


def _norm_mod(x, g, shift, scale):
    ms = jnp.mean(x * x, axis=-1, keepdims=True)
    y = x * lax.rsqrt(ms + RMS_EPS) * g
    return y * (1.0 + scale) + shift


def _mlp(h, wup_ref, wdown_ref, after_chunk=None):
    return _mlp_tiles([h], wup_ref, wdown_ref, after_chunk)[0]


def _mlp_tiles(hs, wup_ref, wdown_ref, after_chunk=None):
    d, dff = wup_ref.shape
    hb = [h.astype(BF16) for h in hs]
    acc = [jnp.zeros((h.shape[0], d), F32) for h in hs]
    for i, c in enumerate(range(0, dff, d)):
        for n in range(len(hs)):
            u = jnp.dot(hb[n], wup_ref[:, c:c + d], preferred_element_type=F32)
            u = jnp.square(jnp.maximum(u, 0.0)).astype(BF16)
            acc[n] += jnp.dot(u, wdown_ref[c:c + d, :], preferred_element_type=F32)
        if after_chunk is not None:
            after_chunk(i)
    return acc


def _rope_tables(pos):
    half = ROPE_DIM // 2
    inv = np.float32(ROPE_THETA) ** (-np.arange(0, ROPE_DIM, 2, dtype=np.float32) / np.float32(ROPE_DIM))
    ang = (pos.astype(np.float32)[:, None] * inv.astype(np.float32)[None, :]).astype(np.float32)
    cos = np.cos(ang.astype(np.float64))
    sin = np.sin(ang.astype(np.float64))
    n = pos.shape[0]
    c = np.ones((n, HEAD_DIM), np.float64)
    s1 = np.zeros((n, HEAD_DIM), np.float64)
    s2 = np.zeros((n, HEAD_DIM), np.float64)
    c[:, :half] = cos
    c[:, half:ROPE_DIM] = cos
    s1[:, :half] = -sin
    s2[:, half:ROPE_DIM] = sin
    return tuple(jnp.asarray(t.astype(np.float32)) for t in (c, s1, s2))


def _qkv_kernel(x_ref, mod_ref, g_ref, w_ref, cos_ref, s1_ref, s2_ref, q_ref, k5_ref, v5_ref, kb_ref, vb_ref,
                *ks_ref, attn_dim):
    d = x_ref.shape[1]
    m = mod_ref[0]
    h = _norm_mod(x_ref[...], g_ref[...], m[:, 0:d], m[:, d:2 * d]).astype(BF16)
    cos, s1, s2 = cos_ref[...], s1_ref[...], s2_ref[...]
    tm = h.shape[0]
    heads = attn_dim // HEAD_DIM
    cw = 2 * HEAD_DIM
    for c in range(0, 3 * attn_dim, cw):
        r = jnp.dot(h, w_ref[:, c:c + cw], preferred_element_type=F32)
        if c < 2 * attn_dim:
            rot = []
            for u in range(0, cw, HEAD_DIM):
                z = r[:, u:u + HEAD_DIM]
                z = (z * cos + pltpu.roll(z, HEAD_DIM - ROPE_DIM // 2, 1) * s1
                     + pltpu.roll(z, ROPE_DIM // 2, 1) * s2)
                rot.append(z)
            r = jnp.concatenate(rot, axis=1)
        if c < attn_dim:
            q_ref[:, c:c + cw] = r
            continue
        is_k = c < 2 * attn_dim
        c0 = c - (attn_dim if is_k else 2 * attn_dim)
        (kb_ref if is_k else vb_ref)[:, c0:c0 + cw] = r.astype(BF16)
        for u in range(0, cw, HEAD_DIM):
            head = (c0 + u) // HEAD_DIM
            (k5_ref if is_k else v5_ref)[pl.ds(head, tm, stride=heads), :] = r[:, u:u + HEAD_DIM]
        if is_k and ks_ref:
            nblk = tm // MOBA_BLOCK
            ks_ref[0][0, :, c0:c0 + cw] = jnp.sum(r.reshape(nblk, MOBA_BLOCK, cw), axis=1)


def _qkv(x, mod, g, w, tables, *, tm, tiles_per_seq, with_ksum):
    rows, d = x.shape
    attn_dim = w.shape[1] // 3
    heads = attn_dim // HEAD_DIM
    n_tiles = rows // tm
    mod_rows = mod.shape[1]
    row_spec = pl.BlockSpec((tm, attn_dim), lambda i: (i, 0))
    tok_head_spec = pl.BlockSpec((tm * heads, HEAD_DIM), lambda i: (i, 0))
    tab_spec = pl.BlockSpec((tm, HEAD_DIM), lambda i: (i % tiles_per_seq, 0))
    out_specs = [row_spec, tok_head_spec, tok_head_spec, row_spec, row_spec]
    out_shape = ([jax.ShapeDtypeStruct((rows, attn_dim), F32)]
                 + [jax.ShapeDtypeStruct((rows * heads, HEAD_DIM), F32)] * 2
                 + [jax.ShapeDtypeStruct((rows, attn_dim), BF16)] * 2)
    if with_ksum:
        nblk = tm // MOBA_BLOCK
        out_specs.append(pl.BlockSpec((1, nblk, attn_dim), lambda i: (i, 0, 0)))
        out_shape.append(jax.ShapeDtypeStruct((n_tiles, nblk, attn_dim), F32))
    return pl.pallas_call(
        functools.partial(_qkv_kernel, attn_dim=attn_dim),
        grid=(n_tiles,),
        in_specs=[
            pl.BlockSpec((tm, d), lambda i: (i, 0)),
            pl.BlockSpec((1, mod_rows, mod.shape[2]), lambda i: (i // tiles_per_seq, 0, 0)),
            _const_spec((1, d)),
            _const_spec(w.shape),
            tab_spec, tab_spec, tab_spec,
        ],
        out_specs=out_specs,
        out_shape=out_shape,
        compiler_params=_params("arbitrary"),
        name="qkv",
    )(x, mod, g, w, *tables)


def _attn_prompt_step(i, q_ref, k_ref, v_ref, ks_ref, o_ref, kaug, vaug, eye, s_sc, *, hps, chunk_blocks, side_work):
    tq = q_ref.shape[0]
    seq = k_ref.shape[0]
    nb = seq // MOBA_BLOCK
    ck = chunk_blocks * MOBA_BLOCK
    n_chunk_max = nb // chunk_blocks

    @pl.when(i == 0)
    def _():
        r_id = lax.broadcasted_iota(jnp.int32, (tq, tq), 0)
        c_id = lax.broadcasted_iota(jnp.int32, (tq, tq), 1)
        eye[...] = jnp.where(r_id == c_id, 1.0, 0.0).astype(BF16)
        lane = lax.broadcasted_iota(jnp.int32, (MOBA_BLOCK, LANES), 1)
        ones_col = jnp.where(lane == 0, 1.0, 0.0).astype(BF16)
        for hh in range(hps):
            cols = slice(hh * HEAD_DIM, (hh + 1) * HEAD_DIM)

            def fill(j, carry):
                rows = pl.ds(pl.multiple_of(j * MOBA_BLOCK, MOBA_BLOCK), MOBA_BLOCK)
                kaug[hh, rows, 0:HEAD_DIM] = k_ref[rows, cols]
                kaug[hh, rows, HEAD_DIM:] = jnp.where(lane == j, 1.0, 0.0).astype(BF16)
                vaug[hh, rows, 0:HEAD_DIM] = v_ref[rows, cols]
                vaug[hh, rows, HEAD_DIM:] = ones_col
                return carry
            lax.fori_loop(0, nb, fill, 0)

    own = pl.ds(pl.multiple_of(i * MOBA_BLOCK, MOBA_BLOCK), MOBA_BLOCK)
    head_cols = [slice(hh * HEAD_DIM, (hh + 1) * HEAD_DIM) for hh in range(hps)]

    def select(hh):
        q = q_ref[:, head_cols[hh]]
        qs = (q * (SOFTMAX_SCALE * np.log2(np.e))).astype(BF16)
        kmean = (ks_ref[0][:, head_cols[hh]] * (1.0 / MOBA_BLOCK)).astype(BF16)
        gate = _nt_dot(kmean, q.astype(BF16))
        blk = lax.broadcasted_iota(jnp.int32, (nb, tq), 0)
        blk_f = blk.astype(F32)
        gate = jnp.where(blk < i, gate, -jnp.inf)
        sel = jnp.zeros((nb, tq), F32)
        for _ in range(MOBA_TOP_K):
            mx = jnp.max(gate, axis=0, keepdims=True)
            first = jnp.min(jnp.where(gate == mx, blk_f, float(nb)), axis=0, keepdims=True)
            hit = blk_f == first
            sel = jnp.where(hit & (mx > -jnp.inf), 1.0, sel)
            gate = jnp.where(hit, -jnp.inf, gate)
        sel_pad = jnp.concatenate([sel, jnp.zeros((LANES - nb, tq), F32)], axis=0).astype(BF16)
        sel_q = _nt_dot(eye[...], sel_pad)
        bias = jnp.where(sel_q > 0.5, 0.0, MASKED).astype(BF16)
        return jnp.concatenate([qs, bias], axis=1)

    def variant(n_chunks):
        heads_ = range(hps)
        qa = [select(hh) for hh in heads_]
        r_id = lax.broadcasted_iota(jnp.int32, (tq, MOBA_BLOCK), 0)
        c_id = lax.broadcasted_iota(jnp.int32, (tq, MOBA_BLOCK), 1)
        s_own = [jnp.where(c_id <= r_id, _nt_dot(qa[hh][:, 0:HEAD_DIM], kaug[hh, own, 0:HEAD_DIM]), MASKED)
                 for hh in heads_]
        run = [jnp.maximum(s[:, 0:LANES], s[:, LANES:]) for s in s_own]
        for c in range(n_chunks):
            for hh in heads_:
                s_c = _nt_dot(qa[hh], kaug[hh, c * ck:(c + 1) * ck, :])
                s_sc[hh, :, c * ck:(c + 1) * ck] = s_c
                for u in range(0, ck, LANES):
                    run[hh] = jnp.maximum(run[hh], s_c[:, u:u + LANES])
        m = [jnp.max(r, axis=1, keepdims=True) for r in run]
        acc = [jnp.dot(jnp.exp2(s_own[hh] - m[hh]).astype(BF16), vaug[hh, own, :], preferred_element_type=F32)
               for hh in heads_]
        for c in range(n_chunks):
            for hh in heads_:
                p_c = jnp.exp2(s_sc[hh, :, c * ck:(c + 1) * ck] - m[hh]).astype(BF16)
                acc[hh] += jnp.dot(p_c, vaug[hh, c * ck:(c + 1) * ck, :], preferred_element_type=F32)
        for hh in heads_:
            o_ref[:, head_cols[hh]] = (acc[hh][:, 0:HEAD_DIM] / acc[hh][:, HEAD_DIM:HEAD_DIM + 1]).astype(o_ref.dtype)

    n_needed = (i + chunk_blocks - 1) // chunk_blocks
    for n in range(n_chunk_max + 1):
        @pl.when(n_needed == n)
        def _(n=n):
            side_work()
            variant(n)


def _scan_queries(q_ref):
    t = q_ref.shape[1]
    qs = q_ref[0] * SOFTMAX_SCALE
    q_hi = qs.astype(BF16).astype(F32)
    q_lo = (qs - q_hi).astype(BF16).astype(F32)
    return jnp.concatenate([q_hi, q_lo, jnp.zeros((16 - 2 * t, qs.shape[1]), F32)], axis=0).astype(BF16)


def _cache_scan_pages(p, page_refs, ksum_sc, *, heads, page):
    pps = len(page_refs)
    ppb = MOBA_BLOCK // page
    bps = pps // ppb
    sums = [jnp.sum(ref[...].reshape(page, heads, HEAD_DIM), axis=0) for ref in page_refs]
    for b in range(bps):
        blk_sum = functools.reduce(lambda x, y: x + y, sums[b * ppb:(b + 1) * ppb])
        ksum_sc[pl.ds(pl.multiple_of((p * bps + b) * heads, heads), heads), :] = blk_sum


def _cache_scan_gate(q_ref, ti_ref, ksum_sc, *, heads):
    t = q_ref.shape[1]
    q16 = _scan_queries(q_ref)
    nb = ksum_sc.shape[0] // heads
    lane = lax.broadcasted_iota(jnp.int32, (16, LANES), 1)
    lane_f = lane.astype(F32)
    for h in range(heads):
        kmean = ksum_sc[pl.ds(h, nb, stride=heads), :] * (1.0 / MOBA_BLOCK)
        km = jnp.concatenate([kmean, jnp.zeros((LANES - nb, HEAD_DIM), F32)], axis=0)
        km_hi, km_lo = _split_bf16(km)
        q_h = q16[:, h * HEAD_DIM:(h + 1) * HEAD_DIM]
        r1 = _nt_dot(q_h, km_hi)
        r2 = _nt_dot(q_h, km_lo)
        gate = r1 + pltpu.roll(r1, 16 - t, 0) + r2
        gate = jnp.where(lane < nb, gate, -jnp.inf)
        out = jnp.zeros((16, LANES), jnp.int32)
        for kk in range(MOBA_TOP_K):
            mx = jnp.max(gate, axis=1, keepdims=True)
            first = jnp.min(jnp.where(gate == mx, lane_f, float(LANES)), axis=1, keepdims=True)
            idx = jnp.where(mx > -jnp.inf, first, -1.0).astype(jnp.int32)
            out = jnp.where(lane == kk, idx, out)
            gate = jnp.where(lane_f == first, -jnp.inf, gate)
        ti_ref[0, h] = out[0:8]


def _attn_scan_kernel(pt_ref, q_ref, k_ref, v_ref, ks_ref, qs_ref, cache_hbm, o_ref, ti_ref,
                      kaug, vaug, eye, s_sc, ksum_sc, pbuf, psem, *, hps, heads, page, scan_steps):
    n_slots, pps = pbuf.shape[0], pbuf.shape[1]
    page_rows = page * heads
    i = pl.program_id(2)
    flat = (pl.program_id(0) * pl.num_programs(1) + pl.program_id(1)) * pl.num_programs(2) + i
    total = pl.num_programs(0) * pl.num_programs(1) * pl.num_programs(2)
    p = flat % scan_steps
    slot = flat % n_slots

    def page_copies(step):
        seq_id, first = step // scan_steps, (step % scan_steps) * pps
        return [pltpu.make_async_copy(
            cache_hbm.at[pl.ds(pl.multiple_of(pt_ref[seq_id, first + u] * page_rows, page_rows), page_rows), :],
            pbuf.at[step % n_slots, u], psem.at[step % n_slots]) for u in range(pps)]

    @pl.when(flat == 0)
    def _():
        for ahead in range(n_slots - 1):
            for cp in page_copies(ahead):
                cp.start()

    @pl.when(flat + n_slots - 1 < total)
    def _():
        for cp in page_copies(flat + n_slots - 1):
            cp.start()

    for cp in page_copies(flat):
        cp.wait()
    page_refs = [pbuf.at[slot, u] for u in range(pps)]

    def scan_pages():
        _cache_scan_pages(p, page_refs, ksum_sc, heads=heads, page=page)

    _attn_prompt_step(i, q_ref, k_ref, v_ref, ks_ref, o_ref, kaug, vaug, eye, s_sc,
                      hps=hps, chunk_blocks=ATTN_CHUNK_BLOCKS, side_work=scan_pages)

    @pl.when(p == scan_steps - 1)
    def _():
        _cache_scan_gate(qs_ref, ti_ref, ksum_sc, heads=heads)


def _attn_scan(page_table, q, k, v, ksum, q_s, cache, *, batch, seq, page):
    rows, attn_dim = q.shape
    heads = attn_dim // HEAD_DIM
    tq = MOBA_BLOCK
    nq = seq // tq
    nb = seq // MOBA_BLOCK
    hps = ATTN_HEADS_PER_STEP
    n_hp = heads // hps
    assert nb <= LANES and nb % ATTN_CHUNK_BLOCKS == 0 and heads % hps == 0

    nseq, n_pages = page_table.shape
    t = q_s.shape[1]
    past = n_pages * page
    nb_past = past // MOBA_BLOCK
    n_steps = batch * n_hp * nq
    assert n_steps % nseq == 0
    scan_steps = n_steps // nseq
    assert n_pages % scan_steps == 0
    pps = n_pages // scan_steps
    assert (pps * page) % MOBA_BLOCK == 0 and MOBA_BLOCK % page == 0
    assert nb_past <= LANES and 2 * t <= 8 and (pps * page // MOBA_BLOCK) % 8 == 0

    def flat(b, h, i):
        return (b * n_hp + h) * nq + i

    q_spec = pl.BlockSpec((tq, hps * HEAD_DIM), lambda b, h, i, pt: (b * nq + i, h))
    kv_spec = pl.BlockSpec((seq, hps * HEAD_DIM), lambda b, h, i, pt: (b, h))
    assert n_steps >= SCAN_SLOTS
    grid_spec = pltpu.PrefetchScalarGridSpec(
        num_scalar_prefetch=1,
        grid=(batch, n_hp, nq),
        in_specs=[q_spec, kv_spec, kv_spec,
                  pl.BlockSpec((1, nb, hps * HEAD_DIM), lambda b, h, i, pt: (b, 0, h)),
                  pl.BlockSpec((1, t, attn_dim), lambda b, h, i, pt: (flat(b, h, i) // scan_steps, 0, 0)),
                  pl.BlockSpec(memory_space=pl.ANY)],
        out_specs=[
            q_spec,
            pl.BlockSpec((1, heads, 8, LANES), lambda b, h, i, pt: (flat(b, h, i) // scan_steps, 0, 0, 0)),
        ],
        scratch_shapes=[
            pltpu.VMEM((hps, seq, 2 * HEAD_DIM), BF16),
            pltpu.VMEM((hps, seq, 2 * HEAD_DIM), BF16),
            pltpu.VMEM((tq, tq), BF16),
            pltpu.VMEM((hps, tq, seq), F32),
            pltpu.VMEM((nb_past * heads, HEAD_DIM), F32),
            pltpu.VMEM((SCAN_SLOTS, pps, page * heads, HEAD_DIM), F32),
            pltpu.SemaphoreType.DMA((SCAN_SLOTS,)),
        ],
    )
    return pl.pallas_call(
        functools.partial(_attn_scan_kernel, hps=hps, heads=heads, page=page, scan_steps=scan_steps),
        grid_spec=grid_spec,
        out_shape=[
            jax.ShapeDtypeStruct((rows, attn_dim), BF16),
            jax.ShapeDtypeStruct((nseq, heads, 8, LANES), jnp.int32),
        ],
        compiler_params=_params("arbitrary", "arbitrary", "arbitrary"),
        name="attn_scan",
    )(page_table, q, k, v, ksum.reshape(batch, nb, attn_dim), q_s, cache)


def _gather_copies(pt_ref, ti_ref, ck_hbm, cv_hbm, kbuf, vbuf, sem, step, slot, k, *, units, heads, nsel,
                   wait_only=False):
    page = cv_hbm.shape[1]
    ppb = MOBA_BLOCK // page
    unit = step * units + k
    s_, h_ = unit // heads, unit % heads
    out = []
    for c in range(nsel):
        blk = None if wait_only else jnp.maximum(ti_ref[s_, h_ * nsel + c], 0)
        for e in range(ppb):
            pg = 0 if wait_only else pt_ref[s_, blk * ppb + e]
            for src, dst in ((ck_hbm, kbuf), (cv_hbm, vbuf)):
                out.append(pltpu.make_async_copy(
                    src.at[pg, :, 0 if wait_only else h_, :], dst.at[slot, (k * nsel + c) * ppb + e],
                    sem.at[slot, k]))
    return out


def _sample_attend_probs(q, kn, kblk, oks):
    t = q.shape[0]
    nsel = len(oks)
    qs = q * SOFTMAX_SCALE
    width = nsel * MOBA_BLOCK
    seg_w = MOBA_TOP_K * MOBA_BLOCK
    q16 = jnp.concatenate([qs, jnp.zeros((16 - t, HEAD_DIM), F32)], axis=0).astype(BF16)
    lt = _nt_dot(q16, kblk.astype(BF16))[0:t]
    segs = [jnp.where(oks[c], lt[:, c * MOBA_BLOCK:(c + 1) * MOBA_BLOCK], MASKED) for c in range(nsel)]
    lm = jnp.concatenate(segs, axis=1)
    col = lax.broadcasted_iota(jnp.int32, (t, width), 1)
    row = lax.broadcasted_iota(jnp.int32, (t, width), 0)
    lm = jnp.where((col >= row * seg_w) & (col < (row + 1) * seg_w), lm, MASKED)
    row_o = lax.broadcasted_iota(jnp.int32, (t, LANES), 0)
    col_o = lax.broadcasted_iota(jnp.int32, (t, LANES), 1)
    own = jnp.full((t, LANES), MASKED, F32)
    for j in range(t):
        dj = jnp.sum(qs * kn[j:j + 1, :], axis=1, keepdims=True)
        own = jnp.where((col_o == j) & (j <= row_o), dj, own)
    full = jnp.concatenate([lm, own], axis=1)
    m = jnp.max(full, axis=1, keepdims=True)
    p = jnp.exp(full - m)
    return p, jnp.sum(p, axis=1, keepdims=True)


def _sample_attend_values(p, l, vn, vblk):
    t = vn.shape[0]
    width = vblk.shape[0]
    p_pad = jnp.concatenate([p[:, :width], jnp.zeros((8 - t, width), F32)], axis=0).astype(BF16)
    pv = jnp.dot(p_pad, vblk.astype(BF16), preferred_element_type=F32)[0:t]
    for j in range(t):
        pv += p[:, width + j:width + j + 1] * vn[j:j + 1, :]
    return pv / l


def _post_attn_kernel(*refs, gather):
    if gather:
        pt_ref, ti_ref, *refs = refs
    x_ref, o_ref, mod_ref, g_ref, wo_ref, wup_ref, wdown_ref, *refs = refs
    if gather:
        qs_ref, kn_ref, vn_ref, ck_hbm, cv_hbm, y_ref, os_ref, kbuf, vbuf, sem = refs
        units, heads = gather["units"], gather["heads"]
        t = qs_ref.shape[1]
        nsel = t * MOBA_TOP_K
        assert kbuf.shape[0] == 2
        j = pl.program_id(0)
        last = pl.num_programs(0) - 1
        slot = j % 2
        copies = functools.partial(_gather_copies, pt_ref, ti_ref, ck_hbm, cv_hbm, kbuf, vbuf, sem,
                                   units=units, heads=heads, nsel=nsel)

        @pl.when(j == 0)
        def _():
            for k in range(units):
                for cp in copies(0, 0, k):
                    cp.start()

        rows_per_unit = kbuf.shape[1] // units
        probs = {}
        ahead = [copies(jnp.minimum(j + 1, last), 1 - slot, k) for k in range(units)]
        landed = [copies(j, slot, k, wait_only=True) for k in range(units)]
        valid = [[ti_ref[(j * units + k) // heads, ((j * units + k) % heads) * nsel + c] >= 0
                  for c in range(nsel)] for k in range(units)]

        def begin(k):
            for cp in ahead[k]:
                cp.start()
            for cp in landed[k]:
                cp.wait()
            cols = slice(k * HEAD_DIM, (k + 1) * HEAD_DIM)
            kblk = kbuf[slot, k * rows_per_unit:(k + 1) * rows_per_unit].reshape(nsel * MOBA_BLOCK, HEAD_DIM)
            probs[k] = _sample_attend_probs(qs_ref[0][:, cols], kn_ref[0][:, cols], kblk, valid[k])

        def finish(k):
            cols = slice(k * HEAD_DIM, (k + 1) * HEAD_DIM)
            vblk = vbuf[slot, k * rows_per_unit:(k + 1) * rows_per_unit].reshape(nsel * MOBA_BLOCK, HEAD_DIM)
            os_ref[0, :, cols] = _sample_attend_values(*probs.pop(k), vn_ref[0][:, cols], vblk)

        def after_chunk(i):
            if 0 < i <= units:
                finish(i - 1)
            if i + 1 < units:
                begin(i + 1)
            if i == n_chunks - 1:
                for k in sorted(probs):
                    finish(k)
    else:
        (y_ref,) = refs
        units = 0
        begin = after_chunk = None
    d = x_ref.shape[1]
    n_chunks = wup_ref.shape[1] // d
    m = mod_ref[0]
    a = jnp.dot(o_ref[...].astype(BF16), wo_ref[...], preferred_element_type=F32)
    x1 = x_ref[...] + m[:, 2 * d:3 * d] * a
    h = _norm_mod(x1, g_ref[...], m[:, 3 * d:4 * d], m[:, 4 * d:5 * d])
    if units > 0:
        begin(0)
    y_ref[...] = x1 + m[:, 5 * d:6 * d] * _mlp(h, wup_ref, wdown_ref, after_chunk)
    if gather:
        assert not probs and units <= n_chunks, "more sample-attention units than MLP chunks"

        @pl.when(j == last)
        def _():
            for group in ahead:
                for cp in group:
                    cp.wait()


def _post_attn(x, o, mod, g, wo, wup, wdown, *, tm, tiles_per_seq, sample=None):
    rows, d = x.shape
    n_tiles = rows // tm
    row_spec = pl.BlockSpec((tm, d), lambda i, *_: (i, 0))
    in_specs = [
        row_spec,
        pl.BlockSpec((tm, o.shape[1]), lambda i, *_: (i, 0)),
        pl.BlockSpec((1, mod.shape[1], mod.shape[2]), lambda i, *_: (i // tiles_per_seq, 0, 0)),
        _const_spec((1, d)),
        _const_spec(wo.shape), _const_spec(wup.shape), _const_spec(wdown.shape),
    ]
    args = [x, o, mod, g, wo, wup, wdown]
    if sample is None:
        return pl.pallas_call(
            functools.partial(_post_attn_kernel, gather=None),
            grid=(n_tiles,),
            in_specs=in_specs,
            out_specs=row_spec,
            out_shape=jax.ShapeDtypeStruct((rows, d), F32),
            compiler_params=_params("arbitrary"),
            name="post_attn",
        )(*args)

    page_table, top_idx, q_s, k_s, v_s, cache_k, cache_v = sample
    nseq, t, attn_dim = q_s.shape
    heads = attn_dim // HEAD_DIM
    page = cache_v.shape[1]
    nsel = t * MOBA_TOP_K
    ppb = MOBA_BLOCK // page
    assert (nseq * heads) % n_tiles == 0
    units = nseq * heads // n_tiles
    assert heads % units == 0 and n_tiles >= GATHER_SLOTS
    groups = heads // units
    new_spec = pl.BlockSpec((1, t, units * HEAD_DIM), lambda i, *_: (i // groups, 0, i % groups))
    gather_buf = pltpu.VMEM((GATHER_SLOTS, units * nsel * ppb, page, HEAD_DIM), F32)
    grid_spec = pltpu.PrefetchScalarGridSpec(
        num_scalar_prefetch=2,
        grid=(n_tiles,),
        in_specs=in_specs + [new_spec, new_spec, new_spec,
                             pl.BlockSpec(memory_space=pl.ANY), pl.BlockSpec(memory_space=pl.ANY)],
        out_specs=[row_spec, new_spec],
        scratch_shapes=[gather_buf, gather_buf, pltpu.SemaphoreType.DMA((GATHER_SLOTS, units))],
    )
    return pl.pallas_call(
        functools.partial(_post_attn_kernel, gather=dict(units=units, heads=heads)),
        grid_spec=grid_spec,
        out_shape=[jax.ShapeDtypeStruct((rows, d), F32), jax.ShapeDtypeStruct((nseq, t, attn_dim), F32)],
        compiler_params=_params("arbitrary"),
        name="post_attn_gather",
    )(page_table, top_idx, *args, q_s, k_s, v_s, cache_k, cache_v)


def _conv_layer_kernel(x_ref, mod_ref, g1_ref, g2_ref, gf_ref, wbcx_ref, wconv_ref, wout_ref, wup_ref, wdown_ref,
                       *rest, tiles_per_seq, period, tail):
    if period is None:
        y_ref, ut_ref, ext = rest
    else:
        e1_ref, e2_ref, y_ref, ut_ref, ext = rest
    i = pl.program_id(0)
    tm, d = x_ref.shape
    halo = 8
    sub = min(tm, CONV_SUB_ROWS)
    subs = [slice(r, r + sub) for r in range(0, tm, sub)]
    m_all = mod_ref[0]
    ms_ = [m_all if m_all.shape[0] == 1 else m_all[s] for s in subs]
    x = [x_ref[s, :] for s in subs]
    h = [_norm_mod(x[n], g1_ref[...], ms_[n][:, 0:d], ms_[n][:, d:2 * d]).astype(BF16) for n in range(len(subs))]
    bcx = [[jnp.dot(hn, wbcx_ref[:, c * d:(c + 1) * d], preferred_element_type=F32) for c in range(3)] for hn in h]
    u = [cg * xin for _, cg, xin in bcx]

    @pl.when(i % tiles_per_seq == 0)
    def _():
        ext[0:halo, :] = jnp.zeros((halo, d), F32)

    w = wconv_ref[...]
    gated = []
    for n, s in enumerate(subs):
        ext[halo + s.start:halo + s.stop, :] = u[n]
        prev1 = ext[halo - 1 + s.start:halo - 1 + s.stop, :]
        prev2 = ext[halo - 2 + s.start:halo - 2 + s.stop, :]
        if period is not None:
            tpos = (lax.broadcasted_iota(jnp.int32, (sub, d), 0) + s.start) % period
            prev1 = jnp.where(tpos < 1, e1_ref[s, :], prev1)
            prev2 = jnp.where(tpos < 2, e2_ref[s, :], prev2)
        conv = w[0:1, :] * prev2 + w[1:2, :] * prev1 + w[2:3, :] * u[n]
        gated.append((bcx[n][0] * conv).astype(BF16))
    ext[0:halo, :] = ext[tm:tm + halo, :]
    ut_ref[0] = ext[halo + tm - tail:halo + tm, :]

    y = [jnp.dot(gn, wout_ref[...], preferred_element_type=F32) for gn in gated]
    x1 = [x[n] + ms_[n][:, 2 * d:3 * d] * y[n] for n in range(len(subs))]
    h2 = [_norm_mod(x1[n], g2_ref[...], ms_[n][:, 3 * d:4 * d], ms_[n][:, 4 * d:5 * d]) for n in range(len(subs))]
    mlp = _mlp_tiles(h2, wup_ref, wdown_ref)
    for n, s in enumerate(subs):
        x2 = x1[n] + ms_[n][:, 5 * d:6 * d] * mlp[n]
        ms2 = jnp.mean(x2 * x2, axis=-1, keepdims=True)
        y_ref[s, :] = x2 * lax.rsqrt(ms2 + RMS_EPS) * gf_ref[...]


def _conv_layer(x, mod, g1, g2, gf, wbcx, wconv, wout, wup, wdown, state_rows, *, tm, tiles_per_seq, period, tail):
    rows, d = x.shape
    n_tiles = rows // tm
    row_spec = pl.BlockSpec((tm, d), lambda i: (i, 0))
    in_specs = [
        row_spec,
        pl.BlockSpec((1, mod.shape[1], mod.shape[2]), lambda i: (i // tiles_per_seq, 0, 0)),
        _const_spec((1, d)), _const_spec((1, d)), _const_spec((1, d)),
        _const_spec(wbcx.shape), _const_spec(wconv.shape), _const_spec(wout.shape),
        _const_spec(wup.shape), _const_spec(wdown.shape),
    ]
    args = [x, mod, g1, g2, gf, wbcx, wconv, wout, wup, wdown]
    if period is not None:
        in_specs += [row_spec, row_spec]
        args += list(state_rows)
    return pl.pallas_call(
        functools.partial(_conv_layer_kernel, tiles_per_seq=tiles_per_seq, period=period, tail=tail),
        grid=(n_tiles,),
        in_specs=in_specs,
        out_specs=[row_spec, pl.BlockSpec((1, tail, d), lambda i: (i, 0, 0))],
        out_shape=[jax.ShapeDtypeStruct((rows, d), F32), jax.ShapeDtypeStruct((n_tiles, tail, d), F32)],
        scratch_shapes=[pltpu.VMEM((tm + 8, d), F32)],
        compiler_params=_params("arbitrary"),
        name="conv_layer",
    )(*args)


def kernel(x_prompt, x_sample, cache_k, cache_v, state_conv, page_table, c_prompt, c_sample, w_ada, b_ada, g_norm,
           w_qkv, w_o, w_bcx, w_conv, w_out, w_up, w_down, g_final):
    batch, seq, d = x_prompt.shape
    nseq, t, _ = x_sample.shape
    depth = w_ada.shape[0]
    n_layers_attn, pool, page, heads, head_dim = cache_k.shape
    attn_dim = heads * head_dim
    n_pages = page_table.shape[1]
    past = n_pages * page
    kw = state_conv.shape[2] + 1
    assert depth == 2 and n_layers_attn == 1 and state_conv.shape[0] == 1, "layer 0 attention, layer 1 convolution"
    assert head_dim == HEAD_DIM and kw == 3 and t >= kw - 1
    assert seq % ROW_TILE == 0 and ROW_TILE % MOBA_BLOCK == 0 and past % MOBA_BLOCK == 0 and t <= MOBA_BLOCK
    assert seq % WIDE_ROW_TILE == 0 and WIDE_ROW_TILE % CONV_SUB_ROWS == 0
    rows_p, rows_s = batch * seq, nseq * t
    assert rows_s % 8 == 0

    pad = -(rows_s + batch) % 8
    c_all = jnp.concatenate([jnp.repeat(c_sample, t, axis=0), c_prompt, jnp.zeros((pad, d), F32)], axis=0)
    mod = _ada(c_all, w_ada, b_ada)
    mod_s = mod[:, None, :rows_s]
    mod_p = mod[:, rows_s:rows_s + batch, None, :]

    bf = lambda a: a.astype(BF16)
    wqkv, wo, wbcx, wout = bf(w_qkv[0]), bf(w_o[0]), bf(w_bcx[0]), bf(w_out[0])
    wup, wdown = [bf(w_up[i]) for i in range(depth)], [bf(w_down[i]) for i in range(depth)]
    g = g_norm.reshape(depth, 2, 1, d)
    gf = g_final.reshape(1, d)

    xp = x_prompt.reshape(rows_p, d)
    xs = x_sample.reshape(rows_s, d)
    tps = seq // ROW_TILE
    tab_p = _rope_tables(np.arange(seq))
    tab_s = _rope_tables(past + (np.arange(rows_s) % t))

    q_p, k_p, v_p, kb_p, vb_p, ksum = _qkv(xp, mod_p[0], g[0, 0], wqkv, tab_p, tm=WIDE_ROW_TILE,
                                           tiles_per_seq=seq // WIDE_ROW_TILE,
                                           with_ksum=True)
    q_s, k_s, v_s, kb_s, vb_s = _qkv(xs, mod_s[0], g[0, 0], wqkv, tab_s, tm=rows_s, tiles_per_seq=1,
                                     with_ksum=False)
    q_s3, kn_s3, vn_s3 = (a.astype(F32).reshape(nseq, t, attn_dim) for a in (q_s, kb_s, vb_s))
    o_p, top = _attn_scan(page_table, q_p, kb_p, vb_p, ksum, q_s3,
                          cache_k.reshape(pool * page * heads, head_dim), batch=batch, seq=seq, page=page)
    top_idx = top[:, :, :t, :MOBA_TOP_K].reshape(nseq, heads * t * MOBA_TOP_K)
    sample = (page_table, top_idx, q_s3, kn_s3, vn_s3, cache_k.reshape(pool, page, heads, head_dim),
              cache_v.reshape(pool, page, heads, head_dim))
    x1_p, o_s = _post_attn(xp, o_p, mod_p[0], g[0, 1], wo, wup[0], wdown[0], tm=ROW_TILE, tiles_per_seq=tps,
                           sample=sample)
    x1_s = _post_attn(xs, o_s.reshape(rows_s, attn_dim), mod_s[0], g[0, 1], wo, wup[0], wdown[0],
                      tm=rows_s, tiles_per_seq=1)

    y_p, ut_p = _conv_layer(x1_p, mod_p[1], g[1, 0], g[1, 1], gf, wbcx, w_conv[0], wout, wup[1], wdown[1], None,
                            tm=WIDE_ROW_TILE, tiles_per_seq=seq // WIDE_ROW_TILE, period=None, tail=8)
    st = state_conv[0]
    zeros = jnp.zeros((nseq, t - 1, d), F32)
    e1 = jnp.concatenate([st[:, 1:2], zeros], axis=1).reshape(rows_s, d)
    e2 = jnp.concatenate([st[:, 0:2], zeros[:, 1:]], axis=1).reshape(rows_s, d)
    y_s, ut_s = _conv_layer(x1_s, mod_s[1], g[1, 0], g[1, 1], gf, wbcx, w_conv[0], wout, wup[1], wdown[1], (e1, e2),
                            tm=rows_s, tiles_per_seq=1, period=t, tail=rows_s)

    conv_p = ut_p.reshape(batch, seq // WIDE_ROW_TILE, 8, d)[:, -1, 8 - (kw - 1):, :]
    conv_s = ut_s.reshape(nseq, t, d)[:, t - (kw - 1):, :]
    return (
        y_p.reshape(batch, seq, d),
        y_s.reshape(nseq, t, d),
        k_p.reshape(1, batch, seq, heads, head_dim),
        v_p.reshape(1, batch, seq, heads, head_dim),
        k_s.reshape(1, nseq, t, heads, head_dim),
        v_s.reshape(1, nseq, t, heads, head_dim),
        conv_p[None],
        conv_s[None],
    )
```

```python
import functools

import numpy as np
import jax
import jax.numpy as jnp
from jax import lax
from jax.experimental import pallas as pl
from jax.experimental.pallas import tpu as pltpu

F32 = jnp.float32
BF16 = jnp.bfloat16

HEAD_DIM = 128
MOBA_BLOCK = 256
MOBA_TOP_K = 3
ROPE_DIM = HEAD_DIM // 4
ROPE_THETA = 500000.0
RMS_EPS = 1e-6
SOFTMAX_SCALE = HEAD_DIM ** -0.5
MASKED = -1e30

LANES = 128
VMEM_LIMIT = 56 * 1024 * 1024
ROW_TILE = 256
WIDE_ROW_TILE = 512
CONV_SUB_ROWS = 256
ADA_COL_TILE = 1536
SCAN_SLOTS = 3
GATHER_SLOTS = 2
ATTN_HEADS_PER_STEP = 2
ATTN_CHUNK_BLOCKS = 2

NT_DIMS = (((1,), (1,)), ((), ()))


def _nt_dot(a, b):
    return lax.dot_general(a, b, NT_DIMS, preferred_element_type=F32)


def _split_bf16(x):
    hi = x.astype(BF16)
    lo = (x - hi.astype(F32)).astype(BF16)
    return hi, lo


def _const_spec(shape):
    n = len(shape)
    return pl.BlockSpec(shape, lambda *_: (0,) * n, pipeline_mode=pl.Buffered(1))


def _layer_spec(shape, layer):
    n = len(shape) - 1
    return pl.BlockSpec((None,) + tuple(shape[1:]), lambda *_: (layer,) + (0,) * n, pipeline_mode=pl.Buffered(1))


def _params(*sem):
    return pltpu.CompilerParams(dimension_semantics=sem, vmem_limit_bytes=VMEM_LIMIT)


def _ada_kernel(c_ref, w_ref, b_ref, o_ref):
    c_hi, c_lo = _split_bf16(c_ref[...])
    w_hi, w_lo = _split_bf16(w_ref[0])
    acc = jnp.dot(c_hi, w_hi, preferred_element_type=F32)
    acc += jnp.dot(c_lo, w_hi, preferred_element_type=F32)
    acc += jnp.dot(c_hi, w_lo, preferred_element_type=F32)
    o_ref[0] = acc + b_ref[0]


def _ada(c_all, w_ada, b_ada):
    depth, d, n6 = w_ada.shape
    rows = c_all.shape[0]
    tn = ADA_COL_TILE
    return pl.pallas_call(
        _ada_kernel,
        grid=(depth, n6 // tn),
        in_specs=[
            pl.BlockSpec((rows, d), lambda l, n: (0, 0)),
            pl.BlockSpec((1, d, tn), lambda l, n: (l, 0, n)),
            pl.BlockSpec((1, 1, tn), lambda l, n: (l, 0, n)),
        ],
        out_specs=pl.BlockSpec((1, rows, tn), lambda l, n: (l, 0, n)),
        out_shape=jax.ShapeDtypeStruct((depth, rows, n6), F32),
        compiler_params=_params("arbitrary", "arbitrary"),
        name="ada",
    )(c_all, w_ada, b_ada.reshape(depth, 1, n6))


def _norm_mod(x, g, shift, scale):
    ms = jnp.mean(x * x, axis=-1, keepdims=True)
    y = x * lax.rsqrt(ms + RMS_EPS) * g
    return y * (1.0 + scale) + shift


def _mlp(h, wup_ref, wdown_ref, after_chunk=None):
    return _mlp_tiles([h], wup_ref, wdown_ref, after_chunk)[0]


def _mlp_tiles(hs, wup_ref, wdown_ref, after_chunk=None):
    d, dff = wup_ref.shape
    hb = [h.astype(BF16) for h in hs]
    acc = [jnp.zeros((h.shape[0], d), F32) for h in hs]
    for i, c in enumerate(range(0, dff, d)):
        for n in range(len(hs)):
            u = jnp.dot(hb[n], wup_ref[:, c:c + d], preferred_element_type=F32)
            u = jnp.square(jnp.maximum(u, 0.0)).astype(BF16)
            acc[n] += jnp.dot(u, wdown_ref[c:c + d, :], preferred_element_type=F32)
        if after_chunk is not None:
            after_chunk(i)
    return acc


def _rope_tables(pos):
    half = ROPE_DIM // 2
    inv = np.float32(ROPE_THETA) ** (-np.arange(0, ROPE_DIM, 2, dtype=np.float32) / np.float32(ROPE_DIM))
    ang = (pos.astype(np.float32)[:, None] * inv.astype(np.float32)[None, :]).astype(np.float32)
    cos = np.cos(ang.astype(np.float64))
    sin = np.sin(ang.astype(np.float64))
    n = pos.shape[0]
    c = np.ones((n, HEAD_DIM), np.float64)
    s1 = np.zeros((n, HEAD_DIM), np.float64)
    s2 = np.zeros((n, HEAD_DIM), np.float64)
    c[:, :half] = cos
    c[:, half:ROPE_DIM] = cos
    s1[:, :half] = -sin
    s2[:, half:ROPE_DIM] = sin
    return tuple(jnp.asarray(t.astype(np.float32)) for t in (c, s1, s2))


def _qkv_kernel(x_ref, mod_ref, g_ref, w_ref, cos_ref, s1_ref, s2_ref, q_ref, k5_ref, v5_ref, kb_ref, vb_ref,
                *ks_ref, attn_dim):
    d = x_ref.shape[1]
    m = mod_ref[0]
    h = _norm_mod(x_ref[...], g_ref[...], m[:, 0:d], m[:, d:2 * d]).astype(BF16)
    cos, s1, s2 = cos_ref[...], s1_ref[...], s2_ref[...]
    tm = h.shape[0]
    heads = attn_dim // HEAD_DIM
    cw = 2 * HEAD_DIM
    for c in range(0, 3 * attn_dim, cw):
        r = jnp.dot(h, w_ref[:, c:c + cw], preferred_element_type=F32)
        if c < 2 * attn_dim:
            rot = []
            for u in range(0, cw, HEAD_DIM):
                z = r[:, u:u + HEAD_DIM]
                z = (z * cos + pltpu.roll(z, HEAD_DIM - ROPE_DIM // 2, 1) * s1
                     + pltpu.roll(z, ROPE_DIM // 2, 1) * s2)
                rot.append(z)
            r = jnp.concatenate(rot, axis=1)
        if c < attn_dim:
            q_ref[:, c:c + cw] = r
            continue
        is_k = c < 2 * attn_dim
        c0 = c - (attn_dim if is_k else 2 * attn_dim)
        (kb_ref if is_k else vb_ref)[:, c0:c0 + cw] = r.astype(BF16)
        for u in range(0, cw, HEAD_DIM):
            head = (c0 + u) // HEAD_DIM
            (k5_ref if is_k else v5_ref)[pl.ds(head, tm, stride=heads), :] = r[:, u:u + HEAD_DIM]
        if is_k and ks_ref:
            nblk = tm // MOBA_BLOCK
            ks_ref[0][0, :, c0:c0 + cw] = jnp.sum(r.reshape(nblk, MOBA_BLOCK, cw), axis=1)


def _qkv(x, mod, g, w, tables, *, tm, tiles_per_seq, with_ksum):
    rows, d = x.shape
    attn_dim = w.shape[1] // 3
    heads = attn_dim // HEAD_DIM
    n_tiles = rows // tm
    mod_rows = mod.shape[1]
    row_spec = pl.BlockSpec((tm, attn_dim), lambda i: (i, 0))
    tok_head_spec = pl.BlockSpec((tm * heads, HEAD_DIM), lambda i: (i, 0))
    tab_spec = pl.BlockSpec((tm, HEAD_DIM), lambda i: (i % tiles_per_seq, 0))
    out_specs = [row_spec, tok_head_spec, tok_head_spec, row_spec, row_spec]
    out_shape = ([jax.ShapeDtypeStruct((rows, attn_dim), F32)]
                 + [jax.ShapeDtypeStruct((rows * heads, HEAD_DIM), F32)] * 2
                 + [jax.ShapeDtypeStruct((rows, attn_dim), BF16)] * 2)
    if with_ksum:
        nblk = tm // MOBA_BLOCK
        out_specs.append(pl.BlockSpec((1, nblk, attn_dim), lambda i: (i, 0, 0)))
        out_shape.append(jax.ShapeDtypeStruct((n_tiles, nblk, attn_dim), F32))
    return pl.pallas_call(
        functools.partial(_qkv_kernel, attn_dim=attn_dim),
        grid=(n_tiles,),
        in_specs=[
            pl.BlockSpec((tm, d), lambda i: (i, 0)),
            pl.BlockSpec((1, mod_rows, mod.shape[2]), lambda i: (i // tiles_per_seq, 0, 0)),
            _const_spec((1, d)),
            _const_spec(w.shape),
            tab_spec, tab_spec, tab_spec,
        ],
        out_specs=out_specs,
        out_shape=out_shape,
        compiler_params=_params("arbitrary"),
        name="qkv",
    )(x, mod, g, w, *tables)


def _attn_prompt_step(first, i, q_ref, k_ref, v_ref, ks_ref, o_ref, kaug, vaug, eye, s_sc,
                      *, hps, chunk_blocks, side_work):
    tq = q_ref.shape[0]
    seq = k_ref.shape[0]
    nb = seq // MOBA_BLOCK
    ck = chunk_blocks * MOBA_BLOCK
    n_chunk_max = nb // chunk_blocks

    @pl.when(first)
    def _():
        r_id = lax.broadcasted_iota(jnp.int32, (tq, tq), 0)
        c_id = lax.broadcasted_iota(jnp.int32, (tq, tq), 1)
        eye[...] = jnp.where(r_id == c_id, 1.0, 0.0).astype(BF16)
        lane = lax.broadcasted_iota(jnp.int32, (MOBA_BLOCK, LANES), 1)
        ones_col = jnp.where(lane == 0, 1.0, 0.0).astype(BF16)
        for hh in range(hps):
            cols = slice(hh * HEAD_DIM, (hh + 1) * HEAD_DIM)

            def fill(j, carry):
                rows = pl.ds(pl.multiple_of(j * MOBA_BLOCK, MOBA_BLOCK), MOBA_BLOCK)
                kaug[hh, rows, 0:HEAD_DIM] = k_ref[rows, cols]
                kaug[hh, rows, HEAD_DIM:] = jnp.where(lane == j, 1.0, 0.0).astype(BF16)
                vaug[hh, rows, 0:HEAD_DIM] = v_ref[rows, cols]
                vaug[hh, rows, HEAD_DIM:] = ones_col
                return carry
            lax.fori_loop(0, nb, fill, 0)

    own = pl.ds(pl.multiple_of(i * MOBA_BLOCK, MOBA_BLOCK), MOBA_BLOCK)
    head_cols = [slice(hh * HEAD_DIM, (hh + 1) * HEAD_DIM) for hh in range(hps)]

    def select(hh):
        q = q_ref[:, head_cols[hh]]
        qs = (q * (SOFTMAX_SCALE * np.log2(np.e))).astype(BF16)
        kmean = (ks_ref[0][:, head_cols[hh]] * (1.0 / MOBA_BLOCK)).astype(BF16)
        gate = _nt_dot(kmean, q.astype(BF16))
        blk = lax.broadcasted_iota(jnp.int32, (nb, tq), 0)
        blk_f = blk.astype(F32)
        gate = jnp.where(blk < i, gate, -jnp.inf)
        sel = jnp.zeros((nb, tq), F32)
        for _ in range(MOBA_TOP_K):
            mx = jnp.max(gate, axis=0, keepdims=True)
            first = jnp.min(jnp.where(gate == mx, blk_f, float(nb)), axis=0, keepdims=True)
            hit = blk_f == first
            sel = jnp.where(hit & (mx > -jnp.inf), 1.0, sel)
            gate = jnp.where(hit, -jnp.inf, gate)
        sel_pad = jnp.concatenate([sel, jnp.zeros((LANES - nb, tq), F32)], axis=0).astype(BF16)
        sel_q = _nt_dot(eye[...], sel_pad)
        bias = jnp.where(sel_q > 0.5, 0.0, MASKED).astype(BF16)
        return jnp.concatenate([qs, bias], axis=1)

    def variant(n_chunks):
        heads_ = range(hps)
        qa = [select(hh) for hh in heads_]
        r_id = lax.broadcasted_iota(jnp.int32, (tq, MOBA_BLOCK), 0)
        c_id = lax.broadcasted_iota(jnp.int32, (tq, MOBA_BLOCK), 1)
        s_own = [jnp.where(c_id <= r_id, _nt_dot(qa[hh][:, 0:HEAD_DIM], kaug[hh, own, 0:HEAD_DIM]), MASKED)
                 for hh in heads_]
        run = [jnp.maximum(s[:, 0:LANES], s[:, LANES:]) for s in s_own]
        for c in range(n_chunks):
            for hh in heads_:
                s_c = _nt_dot(qa[hh], kaug[hh, c * ck:(c + 1) * ck, :])
                s_sc[hh, :, c * ck:(c + 1) * ck] = s_c
                for u in range(0, ck, LANES):
                    run[hh] = jnp.maximum(run[hh], s_c[:, u:u + LANES])
        m = [jnp.max(r, axis=1, keepdims=True) for r in run]
        acc = [jnp.dot(jnp.exp2(s_own[hh] - m[hh]).astype(BF16), vaug[hh, own, :], preferred_element_type=F32)
               for hh in heads_]
        for c in range(n_chunks):
            for hh in heads_:
                p_c = jnp.exp2(s_sc[hh, :, c * ck:(c + 1) * ck] - m[hh]).astype(BF16)
                acc[hh] += jnp.dot(p_c, vaug[hh, c * ck:(c + 1) * ck, :], preferred_element_type=F32)
        for hh in heads_:
            o_ref[:, head_cols[hh]] = (acc[hh][:, 0:HEAD_DIM] / acc[hh][:, HEAD_DIM:HEAD_DIM + 1]).astype(o_ref.dtype)

    n_needed = (i + chunk_blocks - 1) // chunk_blocks
    for n in range(n_chunk_max + 1):
        @pl.when(n_needed == n)
        def _(n=n):
            side_work()
            variant(n)


def _scan_queries(q_ref):
    t = q_ref.shape[1]
    qs = q_ref[0] * SOFTMAX_SCALE
    q_hi = qs.astype(BF16).astype(F32)
    q_lo = (qs - q_hi).astype(BF16).astype(F32)
    return jnp.concatenate([q_hi, q_lo, jnp.zeros((16 - 2 * t, qs.shape[1]), F32)], axis=0).astype(BF16)


def _cache_scan_pages(p, page_refs, ksum_sc, *, heads, page):
    pps = len(page_refs)
    ppb = MOBA_BLOCK // page
    bps = pps // ppb
    sums = [jnp.sum(ref[...].reshape(page, heads, HEAD_DIM), axis=0) for ref in page_refs]
    for b in range(bps):
        blk_sum = functools.reduce(lambda x, y: x + y, sums[b * ppb:(b + 1) * ppb])
        ksum_sc[pl.ds(pl.multiple_of((p * bps + b) * heads, heads), heads), :] = blk_sum


def _cache_scan_gate(q_ref, ti_ref, ksum_sc, *, heads):
    t = q_ref.shape[1]
    q16 = _scan_queries(q_ref)
    nb = ksum_sc.shape[0] // heads
    lane = lax.broadcasted_iota(jnp.int32, (16, LANES), 1)
    lane_f = lane.astype(F32)
    for h in range(heads):
        kmean = ksum_sc[pl.ds(h, nb, stride=heads), :] * (1.0 / MOBA_BLOCK)
        km = jnp.concatenate([kmean, jnp.zeros((LANES - nb, HEAD_DIM), F32)], axis=0)
        km_hi, km_lo = _split_bf16(km)
        q_h = q16[:, h * HEAD_DIM:(h + 1) * HEAD_DIM]
        r1 = _nt_dot(q_h, km_hi)
        r2 = _nt_dot(q_h, km_lo)
        gate = r1 + pltpu.roll(r1, 16 - t, 0) + r2
        gate = jnp.where(lane < nb, gate, -jnp.inf)
        out = jnp.zeros((16, LANES), jnp.int32)
        for kk in range(MOBA_TOP_K):
            mx = jnp.max(gate, axis=1, keepdims=True)
            first = jnp.min(jnp.where(gate == mx, lane_f, float(LANES)), axis=1, keepdims=True)
            idx = jnp.where(mx > -jnp.inf, first, -1.0).astype(jnp.int32)
            out = jnp.where(lane == kk, idx, out)
            gate = jnp.where(lane_f == first, -jnp.inf, gate)
        ti_ref[0, h] = out[0:8]


def _query_block_order(i, n):
    return jnp.where(i % 2 == 0, i // 2, n - 1 - i // 2)


def _attn_scan_kernel(pt_ref, q_ref, k_ref, v_ref, ks_ref, qs_ref, cache_hbm, o_ref, ti_ref,
                      kaug, vaug, eye, s_sc, ksum_sc, pbuf, psem, *, hps, heads, page, scan_steps):
    n_slots, pps = pbuf.shape[0], pbuf.shape[1]
    page_rows = page * heads
    i = pl.program_id(2)
    flat = (pl.program_id(0) * pl.num_programs(1) + pl.program_id(1)) * pl.num_programs(2) + i
    total = pl.num_programs(0) * pl.num_programs(1) * pl.num_programs(2)
    p = flat % scan_steps
    slot = flat % n_slots

    def page_copies(step):
        seq_id, first = step // scan_steps, (step % scan_steps) * pps
        return [pltpu.make_async_copy(
            cache_hbm.at[pl.ds(pl.multiple_of(pt_ref[seq_id, first + u] * page_rows, page_rows), page_rows), :],
            pbuf.at[step % n_slots, u], psem.at[step % n_slots]) for u in range(pps)]

    @pl.when(flat == 0)
    def _():
        for ahead in range(n_slots - 1):
            for cp in page_copies(ahead):
                cp.start()

    @pl.when(flat + n_slots - 1 < total)
    def _():
        for cp in page_copies(flat + n_slots - 1):
            cp.start()

    for cp in page_copies(flat):
        cp.wait()
    page_refs = [pbuf.at[slot, u] for u in range(pps)]

    def scan_pages():
        _cache_scan_pages(p, page_refs, ksum_sc, heads=heads, page=page)

    _attn_prompt_step(i == 0, _query_block_order(i, pl.num_programs(2)), q_ref, k_ref, v_ref, ks_ref, o_ref,
                      kaug, vaug, eye, s_sc, hps=hps, chunk_blocks=ATTN_CHUNK_BLOCKS, side_work=scan_pages)

    @pl.when(p == scan_steps - 1)
    def _():
        _cache_scan_gate(qs_ref, ti_ref, ksum_sc, heads=heads)


def _attn_scan(page_table, q, k, v, ksum, q_s, cache, *, batch, seq, page):
    rows, attn_dim = q.shape
    heads = attn_dim // HEAD_DIM
    tq = MOBA_BLOCK
    nq = seq // tq
    nb = seq // MOBA_BLOCK
    hps = ATTN_HEADS_PER_STEP
    n_hp = heads // hps
    assert nb <= LANES and nb % ATTN_CHUNK_BLOCKS == 0 and heads % hps == 0

    nseq, n_pages = page_table.shape
    t = q_s.shape[1]
    past = n_pages * page
    nb_past = past // MOBA_BLOCK
    n_steps = batch * n_hp * nq
    assert n_steps % nseq == 0
    scan_steps = n_steps // nseq
    assert n_pages % scan_steps == 0
    pps = n_pages // scan_steps
    assert (pps * page) % MOBA_BLOCK == 0 and MOBA_BLOCK % page == 0
    assert nb_past <= LANES and 2 * t <= 8 and (pps * page // MOBA_BLOCK) % 8 == 0

    def flat(b, h, i):
        return (b * n_hp + h) * nq + i

    q_spec = pl.BlockSpec((tq, hps * HEAD_DIM), lambda b, h, i, pt: (b * nq + _query_block_order(i, nq), h))
    kv_spec = pl.BlockSpec((seq, hps * HEAD_DIM), lambda b, h, i, pt: (b, h))
    assert n_steps >= SCAN_SLOTS
    grid_spec = pltpu.PrefetchScalarGridSpec(
        num_scalar_prefetch=1,
        grid=(batch, n_hp, nq),
        in_specs=[q_spec, kv_spec, kv_spec,
                  pl.BlockSpec((1, nb, hps * HEAD_DIM), lambda b, h, i, pt: (b, 0, h)),
                  pl.BlockSpec((1, t, attn_dim), lambda b, h, i, pt: (flat(b, h, i) // scan_steps, 0, 0)),
                  pl.BlockSpec(memory_space=pl.ANY)],
        out_specs=[
            q_spec,
            pl.BlockSpec((1, heads, 8, LANES), lambda b, h, i, pt: (flat(b, h, i) // scan_steps, 0, 0, 0)),
        ],
        scratch_shapes=[
            pltpu.VMEM((hps, seq, 2 * HEAD_DIM), BF16),
            pltpu.VMEM((hps, seq, 2 * HEAD_DIM), BF16),
            pltpu.VMEM((tq, tq), BF16),
            pltpu.VMEM((hps, tq, seq), F32),
            pltpu.VMEM((nb_past * heads, HEAD_DIM), F32),
            pltpu.VMEM((SCAN_SLOTS, pps, page * heads, HEAD_DIM), F32),
            pltpu.SemaphoreType.DMA((SCAN_SLOTS,)),
        ],
    )
    return pl.pallas_call(
        functools.partial(_attn_scan_kernel, hps=hps, heads=heads, page=page, scan_steps=scan_steps),
        grid_spec=grid_spec,
        out_shape=[
            jax.ShapeDtypeStruct((rows, attn_dim), BF16),
            jax.ShapeDtypeStruct((nseq, heads, 8, LANES), jnp.int32),
        ],
        compiler_params=_params("arbitrary", "arbitrary", "arbitrary"),
        name="attn_scan",
    )(page_table, q, k, v, ksum.reshape(batch, nb, attn_dim), q_s, cache)


def _gather_copies(pt_ref, ti_ref, ck_hbm, cv_hbm, kbuf, vbuf, sem, step, slot, k, *, units, heads, nsel,
                   wait_only=False):
    page = cv_hbm.shape[1]
    ppb = MOBA_BLOCK // page
    unit = step * units + k
    s_, h_ = unit // heads, unit % heads
    out = []
    for c in range(nsel):
        blk = None if wait_only else jnp.maximum(ti_ref[s_, h_ * nsel + c], 0)
        for e in range(ppb):
            pg = 0 if wait_only else pt_ref[s_, blk * ppb + e]
            for src, dst in ((ck_hbm, kbuf), (cv_hbm, vbuf)):
                out.append(pltpu.make_async_copy(
                    src.at[pg, :, 0 if wait_only else h_, :], dst.at[slot, (k * nsel + c) * ppb + e],
                    sem.at[slot, k]))
    return out


def _sample_attend_probs(q, kn, kblk, oks):
    t = q.shape[0]
    nsel = len(oks)
    qs = q * SOFTMAX_SCALE
    width = nsel * MOBA_BLOCK
    seg_w = MOBA_TOP_K * MOBA_BLOCK
    q16 = jnp.concatenate([qs, jnp.zeros((16 - t, HEAD_DIM), F32)], axis=0).astype(BF16)
    lt = _nt_dot(q16, kblk.astype(BF16))[0:t]
    segs = [jnp.where(oks[c], lt[:, c * MOBA_BLOCK:(c + 1) * MOBA_BLOCK], MASKED) for c in range(nsel)]
    lm = jnp.concatenate(segs, axis=1)
    col = lax.broadcasted_iota(jnp.int32, (t, width), 1)
    row = lax.broadcasted_iota(jnp.int32, (t, width), 0)
    lm = jnp.where((col >= row * seg_w) & (col < (row + 1) * seg_w), lm, MASKED)
    row_o = lax.broadcasted_iota(jnp.int32, (t, LANES), 0)
    col_o = lax.broadcasted_iota(jnp.int32, (t, LANES), 1)
    own = jnp.full((t, LANES), MASKED, F32)
    for j in range(t):
        dj = jnp.sum(qs * kn[j:j + 1, :], axis=1, keepdims=True)
        own = jnp.where((col_o == j) & (j <= row_o), dj, own)
    full = jnp.concatenate([lm, own], axis=1)
    m = jnp.max(full, axis=1, keepdims=True)
    p = jnp.exp(full - m)
    return p, jnp.sum(p, axis=1, keepdims=True)


def _sample_attend_values(p, l, vn, vblk):
    t = vn.shape[0]
    width = vblk.shape[0]
    p_pad = jnp.concatenate([p[:, :width], jnp.zeros((8 - t, width), F32)], axis=0).astype(BF16)
    pv = jnp.dot(p_pad, vblk.astype(BF16), preferred_element_type=F32)[0:t]
    for j in range(t):
        pv += p[:, width + j:width + j + 1] * vn[j:j + 1, :]
    return pv / l


def _post_attn_kernel(*refs, gather):
    if gather:
        pt_ref, ti_ref, *refs = refs
    x_ref, o_ref, mod_ref, g_ref, wo_ref, wup_ref, wdown_ref, *refs = refs
    if gather:
        qs_ref, kn_ref, vn_ref, ck_hbm, cv_hbm, y_ref, os_ref, kbuf, vbuf, sem = refs
        units, heads = gather["units"], gather["heads"]
        t = qs_ref.shape[1]
        nsel = t * MOBA_TOP_K
        assert kbuf.shape[0] == 2
        j = pl.program_id(0)
        last = pl.num_programs(0) - 1
        slot = j % 2
        copies = functools.partial(_gather_copies, pt_ref, ti_ref, ck_hbm, cv_hbm, kbuf, vbuf, sem,
                                   units=units, heads=heads, nsel=nsel)

        @pl.when(j == 0)
        def _():
            for k in range(units):
                for cp in copies(0, 0, k):
                    cp.start()

        rows_per_unit = kbuf.shape[1] // units
        probs = {}
        ahead = [copies(jnp.minimum(j + 1, last), 1 - slot, k) for k in range(units)]
        landed = [copies(j, slot, k, wait_only=True) for k in range(units)]
        valid = [[ti_ref[(j * units + k) // heads, ((j * units + k) % heads) * nsel + c] >= 0
                  for c in range(nsel)] for k in range(units)]

        def begin(k):
            for cp in ahead[k]:
                cp.start()
            for cp in landed[k]:
                cp.wait()
            cols = slice(k * HEAD_DIM, (k + 1) * HEAD_DIM)
            kblk = kbuf[slot, k * rows_per_unit:(k + 1) * rows_per_unit].reshape(nsel * MOBA_BLOCK, HEAD_DIM)
            probs[k] = _sample_attend_probs(qs_ref[0][:, cols], kn_ref[0][:, cols], kblk, valid[k])

        def finish(k):
            cols = slice(k * HEAD_DIM, (k + 1) * HEAD_DIM)
            vblk = vbuf[slot, k * rows_per_unit:(k + 1) * rows_per_unit].reshape(nsel * MOBA_BLOCK, HEAD_DIM)
            os_ref[0, :, cols] = _sample_attend_values(*probs.pop(k), vn_ref[0][:, cols], vblk)

        def after_chunk(i):
            if 0 < i <= units:
                finish(i - 1)
            if i + 1 < units:
                begin(i + 1)
            if i == n_chunks - 1:
                for k in sorted(probs):
                    finish(k)
    else:
        (y_ref,) = refs
        units = 0
        begin = after_chunk = None
    d = x_ref.shape[1]
    n_chunks = wup_ref.shape[1] // d
    m = mod_ref[0]
    a = jnp.dot(o_ref[...].astype(BF16), wo_ref[...], preferred_element_type=F32)
    x1 = x_ref[...] + m[:, 2 * d:3 * d] * a
    h = _norm_mod(x1, g_ref[...], m[:, 3 * d:4 * d], m[:, 4 * d:5 * d])
    if units > 0:
        begin(0)
    y_ref[...] = x1 + m[:, 5 * d:6 * d] * _mlp(h, wup_ref, wdown_ref, after_chunk)
    if gather:
        assert not probs and units <= n_chunks, "more sample-attention units than MLP chunks"

        @pl.when(j == last)
        def _():
            for group in ahead:
                for cp in group:
                    cp.wait()


def _post_attn(x, o, mod, g, wo, wup, wdown, *, layer, tm, tiles_per_seq, sample=None):
    rows, d = x.shape
    n_tiles = rows // tm
    row_spec = pl.BlockSpec((tm, d), lambda i, *_: (i, 0))
    in_specs = [
        row_spec,
        pl.BlockSpec((tm, o.shape[1]), lambda i, *_: (i, 0)),
        pl.BlockSpec((1, mod.shape[1], mod.shape[2]), lambda i, *_: (i // tiles_per_seq, 0, 0)),
        _const_spec((1, d)),
        _const_spec(wo.shape), _layer_spec(wup.shape, layer), _layer_spec(wdown.shape, layer),
    ]
    args = [x, o, mod, g, wo, wup, wdown]
    if sample is None:
        return pl.pallas_call(
            functools.partial(_post_attn_kernel, gather=None),
            grid=(n_tiles,),
            in_specs=in_specs,
            out_specs=row_spec,
            out_shape=jax.ShapeDtypeStruct((rows, d), F32),
            compiler_params=_params("arbitrary"),
            name="post_attn",
        )(*args)

    page_table, top_idx, q_s, k_s, v_s, cache_k, cache_v = sample
    nseq, t, attn_dim = q_s.shape
    heads = attn_dim // HEAD_DIM
    page = cache_v.shape[1]
    nsel = t * MOBA_TOP_K
    ppb = MOBA_BLOCK // page
    assert (nseq * heads) % n_tiles == 0
    units = nseq * heads // n_tiles
    assert heads % units == 0 and n_tiles >= GATHER_SLOTS
    groups = heads // units
    new_spec = pl.BlockSpec((1, t, units * HEAD_DIM), lambda i, *_: (i // groups, 0, i % groups))
    gather_buf = pltpu.VMEM((GATHER_SLOTS, units * nsel * ppb, page, HEAD_DIM), F32)
    grid_spec = pltpu.PrefetchScalarGridSpec(
        num_scalar_prefetch=2,
        grid=(n_tiles,),
        in_specs=in_specs + [new_spec, new_spec, new_spec,
                             pl.BlockSpec(memory_space=pl.ANY), pl.BlockSpec(memory_space=pl.ANY)],
        out_specs=[row_spec, new_spec],
        scratch_shapes=[gather_buf, gather_buf, pltpu.SemaphoreType.DMA((GATHER_SLOTS, units))],
    )
    return pl.pallas_call(
        functools.partial(_post_attn_kernel, gather=dict(units=units, heads=heads)),
        grid_spec=grid_spec,
        out_shape=[jax.ShapeDtypeStruct((rows, d), F32), jax.ShapeDtypeStruct((nseq, t, attn_dim), F32)],
        compiler_params=_params("arbitrary"),
        name="post_attn_gather",
    )(page_table, top_idx, *args, q_s, k_s, v_s, cache_k, cache_v)


def _conv_layer_kernel(x_ref, mod_ref, g1_ref, g2_ref, gf_ref, wbcx_ref, wconv_ref, wout_ref, wup_ref, wdown_ref,
                       *rest, tiles_per_seq, period, tail):
    if period is None:
        y_ref, ut_ref, ext = rest
    else:
        e1_ref, e2_ref, y_ref, ut_ref, ext = rest
    i = pl.program_id(0)
    tm, d = x_ref.shape
    halo = 8
    sub = min(tm, CONV_SUB_ROWS)
    subs = [slice(r, r + sub) for r in range(0, tm, sub)]
    m_all = mod_ref[0]
    ms_ = [m_all if m_all.shape[0] == 1 else m_all[s] for s in subs]
    x = [x_ref[s, :] for s in subs]
    h = [_norm_mod(x[n], g1_ref[...], ms_[n][:, 0:d], ms_[n][:, d:2 * d]).astype(BF16) for n in range(len(subs))]
    bcx = [[jnp.dot(hn, wbcx_ref[:, c * d:(c + 1) * d], preferred_element_type=F32) for c in range(3)] for hn in h]
    u = [cg * xin for _, cg, xin in bcx]

    @pl.when(i % tiles_per_seq == 0)
    def _():
        ext[0:halo, :] = jnp.zeros((halo, d), F32)

    w = wconv_ref[...]
    gated = []
    for n, s in enumerate(subs):
        ext[halo + s.start:halo + s.stop, :] = u[n]
        prev1 = ext[halo - 1 + s.start:halo - 1 + s.stop, :]
        prev2 = ext[halo - 2 + s.start:halo - 2 + s.stop, :]
        if period is not None:
            tpos = (lax.broadcasted_iota(jnp.int32, (sub, d), 0) + s.start) % period
            prev1 = jnp.where(tpos < 1, e1_ref[s, :], prev1)
            prev2 = jnp.where(tpos < 2, e2_ref[s, :], prev2)
        conv = w[0:1, :] * prev2 + w[1:2, :] * prev1 + w[2:3, :] * u[n]
        gated.append((bcx[n][0] * conv).astype(BF16))
    ext[0:halo, :] = ext[tm:tm + halo, :]
    ut_ref[0] = ext[halo + tm - tail:halo + tm, :]

    y = [jnp.dot(gn, wout_ref[...], preferred_element_type=F32) for gn in gated]
    x1 = [x[n] + ms_[n][:, 2 * d:3 * d] * y[n] for n in range(len(subs))]
    h2 = [_norm_mod(x1[n], g2_ref[...], ms_[n][:, 3 * d:4 * d], ms_[n][:, 4 * d:5 * d]) for n in range(len(subs))]
    mlp = _mlp_tiles(h2, wup_ref, wdown_ref)
    for n, s in enumerate(subs):
        x2 = x1[n] + ms_[n][:, 5 * d:6 * d] * mlp[n]
        ms2 = jnp.mean(x2 * x2, axis=-1, keepdims=True)
        y_ref[s, :] = x2 * lax.rsqrt(ms2 + RMS_EPS) * gf_ref[...]


def _conv_layer(x, mod, g1, g2, gf, wbcx, wconv, wout, wup, wdown, state_rows,
                *, layer, tm, tiles_per_seq, period, tail):
    rows, d = x.shape
    n_tiles = rows // tm
    row_spec = pl.BlockSpec((tm, d), lambda i: (i, 0))
    in_specs = [
        row_spec,
        pl.BlockSpec((1, mod.shape[1], mod.shape[2]), lambda i: (i // tiles_per_seq, 0, 0)),
        _const_spec((1, d)), _const_spec((1, d)), _const_spec((1, d)),
        _const_spec(wbcx.shape), _const_spec(wconv.shape), _const_spec(wout.shape),
        _layer_spec(wup.shape, layer), _layer_spec(wdown.shape, layer),
    ]
    args = [x, mod, g1, g2, gf, wbcx, wconv, wout, wup, wdown]
    if period is not None:
        in_specs += [row_spec, row_spec]
        args += list(state_rows)
    return pl.pallas_call(
        functools.partial(_conv_layer_kernel, tiles_per_seq=tiles_per_seq, period=period, tail=tail),
        grid=(n_tiles,),
        in_specs=in_specs,
        out_specs=[row_spec, pl.BlockSpec((1, tail, d), lambda i: (i, 0, 0))],
        out_shape=[jax.ShapeDtypeStruct((rows, d), F32), jax.ShapeDtypeStruct((n_tiles, tail, d), F32)],
        scratch_shapes=[pltpu.VMEM((tm + 8, d), F32)],
        compiler_params=_params("arbitrary"),
        name="conv_layer",
    )(*args)


def kernel(x_prompt, x_sample, cache_k, cache_v, state_conv, page_table, c_prompt, c_sample, w_ada, b_ada, g_norm,
           w_qkv, w_o, w_bcx, w_conv, w_out, w_up, w_down, g_final):
    batch, seq, d = x_prompt.shape
    nseq, t, _ = x_sample.shape
    depth = w_ada.shape[0]
    n_layers_attn, pool, page, heads, head_dim = cache_k.shape
    attn_dim = heads * head_dim
    n_pages = page_table.shape[1]
    past = n_pages * page
    kw = state_conv.shape[2] + 1
    assert depth == 2 and n_layers_attn == 1 and state_conv.shape[0] == 1, "layer 0 attention, layer 1 convolution"
    assert head_dim == HEAD_DIM and kw == 3 and t >= kw - 1
    assert seq % ROW_TILE == 0 and ROW_TILE % MOBA_BLOCK == 0 and past % MOBA_BLOCK == 0 and t <= MOBA_BLOCK
    assert seq % WIDE_ROW_TILE == 0 and WIDE_ROW_TILE % CONV_SUB_ROWS == 0
    rows_p, rows_s = batch * seq, nseq * t
    assert rows_s % 8 == 0

    pad = -(rows_s + batch) % 8
    c_all = jnp.concatenate([jnp.repeat(c_sample, t, axis=0), c_prompt, jnp.zeros((pad, d), F32)], axis=0)
    mod = _ada(c_all, w_ada, b_ada)
    mod_s = mod[:, None, :rows_s]
    mod_p = mod[:, rows_s:rows_s + batch, None, :]

    bf = lambda a: a.astype(BF16)
    wqkv, wo, wbcx, wout = bf(w_qkv[0]), bf(w_o[0]), bf(w_bcx[0]), bf(w_out[0])
    wup, wdown = bf(w_up), bf(w_down)
    g = g_norm.reshape(depth, 2, 1, d)
    gf = g_final.reshape(1, d)

    xp = x_prompt.reshape(rows_p, d)
    xs = x_sample.reshape(rows_s, d)
    tps = seq // ROW_TILE
    tab_p = _rope_tables(np.arange(seq))
    tab_s = _rope_tables(past + (np.arange(rows_s) % t))

    q_p, k_p, v_p, kb_p, vb_p, ksum = _qkv(xp, mod_p[0], g[0, 0], wqkv, tab_p, tm=WIDE_ROW_TILE,
                                           tiles_per_seq=seq // WIDE_ROW_TILE,
                                           with_ksum=True)
    q_s, k_s, v_s, kb_s, vb_s = _qkv(xs, mod_s[0], g[0, 0], wqkv, tab_s, tm=rows_s, tiles_per_seq=1,
                                     with_ksum=False)
    q_s3, kn_s3, vn_s3 = (a.astype(F32).reshape(nseq, t, attn_dim) for a in (q_s, kb_s, vb_s))
    o_p, top = _attn_scan(page_table, q_p, kb_p, vb_p, ksum, q_s3,
                          cache_k.reshape(pool * page * heads, head_dim), batch=batch, seq=seq, page=page)
    top_idx = top[:, :, :t, :MOBA_TOP_K].reshape(nseq, heads * t * MOBA_TOP_K)
    sample = (page_table, top_idx, q_s3, kn_s3, vn_s3, cache_k.reshape(pool, page, heads, head_dim),
              cache_v.reshape(pool, page, heads, head_dim))
    x1_p, o_s = _post_attn(xp, o_p, mod_p[0], g[0, 1], wo, wup, wdown, layer=0, tm=ROW_TILE, tiles_per_seq=tps,
                           sample=sample)
    x1_s = _post_attn(xs, o_s.reshape(rows_s, attn_dim), mod_s[0], g[0, 1], wo, wup, wdown, layer=0,
                      tm=rows_s, tiles_per_seq=1)

    y_p, ut_p = _conv_layer(x1_p, mod_p[1], g[1, 0], g[1, 1], gf, wbcx, w_conv[0], wout, wup, wdown, None, layer=1,
                            tm=WIDE_ROW_TILE, tiles_per_seq=seq // WIDE_ROW_TILE, period=None, tail=8)
    st = state_conv[0]
    zeros = jnp.zeros((nseq, t - 1, d), F32)
    e1 = jnp.concatenate([st[:, 1:2], zeros], axis=1).reshape(rows_s, d)
    e2 = jnp.concatenate([st[:, 0:2], zeros[:, 1:]], axis=1).reshape(rows_s, d)
    y_s, ut_s = _conv_layer(x1_s, mod_s[1], g[1, 0], g[1, 1], gf, wbcx, w_conv[0], wout, wup, wdown, (e1, e2),
                            layer=1, tm=rows_s, tiles_per_seq=1, period=t, tail=rows_s)

    conv_p = ut_p.reshape(batch, seq // WIDE_ROW_TILE, 8, d)[:, -1, 8 - (kw - 1):, :]
    conv_s = ut_s.reshape(nseq, t, d)[:, t - (kw - 1):, :]
    return (
        y_p.reshape(batch, seq, d),
        y_s.reshape(nseq, t, d),
        k_p.reshape(1, batch, seq, heads, head_dim),
        v_p.reshape(1, batch, seq, heads, head_dim),
        k_s.reshape(1, nseq, t, heads, head_dim),
        v_s.reshape(1, nseq, t, heads, head_dim),
        conv_p[None],
        conv_s[None],
    )
```

```python
import functools

import numpy as np
import jax
import jax.numpy as jnp
from jax import lax
from jax.experimental import pallas as pl
from jax.experimental.pallas import tpu as pltpu

F32 = jnp.float32
BF16 = jnp.bfloat16

HEAD_DIM = 128
MOBA_BLOCK = 256
MOBA_TOP_K = 3
ROPE_DIM = HEAD_DIM // 4
ROPE_THETA = 500000.0
RMS_EPS = 1e-6
SOFTMAX_SCALE = HEAD_DIM ** -0.5
MASKED = -1e30

LANES = 128
VMEM_LIMIT = 60 * 1024 * 1024
ROW_TILE = 256
WIDE_ROW_TILE = 512
CONV_SUB_ROWS = 256
ADA_COL_TILE = 1536
BULK_DMA_PRIORITY = 1
SCAN_SLOTS = 4
GATHER_SLOTS = 2
ATTN_HEADS_PER_STEP = 2
ATTN_CHUNK_BLOCKS = 2

NT_DIMS = (((1,), (1,)), ((), ()))


def _nt_dot(a, b):
    return lax.dot_general(a, b, NT_DIMS, preferred_element_type=F32)


def _split_bf16(x):
    hi = x.astype(BF16)
    lo = (x - hi.astype(F32)).astype(BF16)
    return hi, lo


def _const_spec(shape):
    n = len(shape)
    return pl.BlockSpec(shape, lambda *_: (0,) * n, pipeline_mode=pl.Buffered(1))


def _layer_spec(shape, layer):
    n = len(shape) - 1
    return pl.BlockSpec((None,) + tuple(shape[1:]), lambda *_: (layer,) + (0,) * n, pipeline_mode=pl.Buffered(1))


def _params(*sem):
    return pltpu.CompilerParams(dimension_semantics=sem, vmem_limit_bytes=VMEM_LIMIT)


def _ada_kernel(c_ref, w_ref, b_ref, o_ref):
    c_hi, c_lo = _split_bf16(c_ref[...])
    w_hi, w_lo = _split_bf16(w_ref[0])
    acc = jnp.dot(c_hi, w_hi, preferred_element_type=F32)
    acc += jnp.dot(c_lo, w_hi, preferred_element_type=F32)
    acc += jnp.dot(c_hi, w_lo, preferred_element_type=F32)
    o_ref[0] = acc + b_ref[0]


def _ada(c_all, w_ada, b_ada):
    depth, d, n6 = w_ada.shape
    rows = c_all.shape[0]
    tn = ADA_COL_TILE
    return pl.pallas_call(
        _ada_kernel,
        grid=(depth, n6 // tn),
        in_specs=[
            pl.BlockSpec((rows, d), lambda l, n: (0, 0)),
            pl.BlockSpec((1, d, tn), lambda l, n: (l, 0, n)),
            pl.BlockSpec((1, 1, tn), lambda l, n: (l, 0, n)),
        ],
        out_specs=pl.BlockSpec((1, rows, tn), lambda l, n: (l, 0, n)),
        out_shape=jax.ShapeDtypeStruct((depth, rows, n6), F32),
        compiler_params=_params("arbitrary", "arbitrary"),
        name="ada",
    )(c_all, w_ada, b_ada.reshape(depth, 1, n6))


def _norm_mod(x, g, shift, scale):
    ms = jnp.mean(x * x, axis=-1, keepdims=True)
    y = x * lax.rsqrt(ms + RMS_EPS) * g
    return y * (1.0 + scale) + shift


def _mlp(h, wup_ref, wdown_ref, after_chunk=None):
    return _mlp_tiles([h], wup_ref, wdown_ref, after_chunk)[0]


def _mlp_tiles(hs, wup_ref, wdown_ref, after_chunk=None):
    d, dff = wup_ref.shape
    hb = [h.astype(BF16) for h in hs]
    acc = [jnp.zeros((h.shape[0], d), F32) for h in hs]
    for i, c in enumerate(range(0, dff, d)):
        for n in range(len(hs)):
            u = jnp.dot(hb[n], wup_ref[:, c:c + d], preferred_element_type=F32)
            u = jnp.square(jnp.maximum(u, 0.0)).astype(BF16)
            acc[n] += jnp.dot(u, wdown_ref[c:c + d, :], preferred_element_type=F32)
        if after_chunk is not None:
            after_chunk(i)
    return acc


def _rope_tables(pos):
    half = ROPE_DIM // 2
    inv = np.float32(ROPE_THETA) ** (-np.arange(0, ROPE_DIM, 2, dtype=np.float32) / np.float32(ROPE_DIM))
    ang = (pos.astype(np.float32)[:, None] * inv.astype(np.float32)[None, :]).astype(np.float32)
    cos = np.cos(ang.astype(np.float64))
    sin = np.sin(ang.astype(np.float64))
    n = pos.shape[0]
    c = np.ones((n, HEAD_DIM), np.float64)
    s1 = np.zeros((n, HEAD_DIM), np.float64)
    s2 = np.zeros((n, HEAD_DIM), np.float64)
    c[:, :half] = cos
    c[:, half:ROPE_DIM] = cos
    s1[:, :half] = -sin
    s2[:, half:ROPE_DIM] = sin
    return tuple(jnp.asarray(t.astype(np.float32)) for t in (c, s1, s2))


def _qkv_kernel(x_ref, mod_ref, g_ref, w_ref, cos_ref, s1_ref, s2_ref, q_ref, k5_ref, v5_ref, kb_ref, vb_ref,
                *ks_ref, attn_dim):
    d = x_ref.shape[1]
    m = mod_ref[0]
    h = _norm_mod(x_ref[...], g_ref[...], m[:, 0:d], m[:, d:2 * d]).astype(BF16)
    cos, s1, s2 = cos_ref[...], s1_ref[...], s2_ref[...]
    tm = h.shape[0]
    heads = attn_dim // HEAD_DIM
    cw = 2 * HEAD_DIM
    for c in range(0, 3 * attn_dim, cw):
        r = jnp.dot(h, w_ref[:, c:c + cw], preferred_element_type=F32)
        if c < 2 * attn_dim:
            rot = []
            for u in range(0, cw, HEAD_DIM):
                z = r[:, u:u + HEAD_DIM]
                z = (z * cos + pltpu.roll(z, HEAD_DIM - ROPE_DIM // 2, 1) * s1
                     + pltpu.roll(z, ROPE_DIM // 2, 1) * s2)
                rot.append(z)
            r = jnp.concatenate(rot, axis=1)
        if c < attn_dim:
            q_ref[:, c:c + cw] = r
            continue
        is_k = c < 2 * attn_dim
        c0 = c - (attn_dim if is_k else 2 * attn_dim)
        (kb_ref if is_k else vb_ref)[:, c0:c0 + cw] = r.astype(BF16)
        for u in range(0, cw, HEAD_DIM):
            head = (c0 + u) // HEAD_DIM
            (k5_ref if is_k else v5_ref)[pl.ds(head, tm, stride=heads), :] = r[:, u:u + HEAD_DIM]
        if is_k and ks_ref:
            nblk = tm // MOBA_BLOCK
            ks_ref[0][0, :, c0:c0 + cw] = jnp.sum(r.reshape(nblk, MOBA_BLOCK, cw), axis=1)


def _qkv(x, mod, g, w, tables, *, tm, tiles_per_seq, with_ksum):
    rows, d = x.shape
    attn_dim = w.shape[1] // 3
    heads = attn_dim // HEAD_DIM
    n_tiles = rows // tm
    mod_rows = mod.shape[1]
    row_spec = pl.BlockSpec((tm, attn_dim), lambda i: (i, 0))
    tok_head_spec = pl.BlockSpec((tm * heads, HEAD_DIM), lambda i: (i, 0))
    tab_spec = pl.BlockSpec((tm, HEAD_DIM), lambda i: (i % tiles_per_seq, 0))
    out_specs = [row_spec, tok_head_spec, tok_head_spec, row_spec, row_spec]
    out_shape = ([jax.ShapeDtypeStruct((rows, attn_dim), F32)]
                 + [jax.ShapeDtypeStruct((rows * heads, HEAD_DIM), F32)] * 2
                 + [jax.ShapeDtypeStruct((rows, attn_dim), BF16)] * 2)
    if with_ksum:
        nblk = tm // MOBA_BLOCK
        out_specs.append(pl.BlockSpec((1, nblk, attn_dim), lambda i: (i, 0, 0)))
        out_shape.append(jax.ShapeDtypeStruct((n_tiles, nblk, attn_dim), F32))
    return pl.pallas_call(
        functools.partial(_qkv_kernel, attn_dim=attn_dim),
        grid=(n_tiles,),
        in_specs=[
            pl.BlockSpec((tm, d), lambda i: (i, 0)),
            pl.BlockSpec((1, mod_rows, mod.shape[2]), lambda i: (i // tiles_per_seq, 0, 0)),
            _const_spec((1, d)),
            _const_spec(w.shape),
            tab_spec, tab_spec, tab_spec,
        ],
        out_specs=out_specs,
        out_shape=out_shape,
        compiler_params=_params("arbitrary"),
        name="qkv",
    )(x, mod, g, w, *tables)


def _attn_prompt_step(first, i, q_ref, k_ref, v_ref, ks_ref, o_ref, kaug, vaug, eye, s_sc,
                      *, hps, chunk_blocks, side_work):
    tq = q_ref.shape[0]
    seq = k_ref.shape[0]
    nb = seq // MOBA_BLOCK
    ck = chunk_blocks * MOBA_BLOCK
    n_chunk_max = nb // chunk_blocks

    @pl.when(first)
    def _():
        r_id = lax.broadcasted_iota(jnp.int32, (tq, tq), 0)
        c_id = lax.broadcasted_iota(jnp.int32, (tq, tq), 1)
        eye[...] = jnp.where(r_id == c_id, 1.0, 0.0).astype(BF16)
        lane = lax.broadcasted_iota(jnp.int32, (MOBA_BLOCK, LANES), 1)
        ones_col = jnp.where(lane == 0, 1.0, 0.0).astype(BF16)
        for hh in range(hps):
            cols = slice(hh * HEAD_DIM, (hh + 1) * HEAD_DIM)

            def fill(j, carry):
                rows = pl.ds(pl.multiple_of(j * MOBA_BLOCK, MOBA_BLOCK), MOBA_BLOCK)
                kaug[hh, rows, 0:HEAD_DIM] = k_ref[rows, cols]
                kaug[hh, rows, HEAD_DIM:] = jnp.where(lane == j, 1.0, 0.0).astype(BF16)
                vaug[hh, rows, 0:HEAD_DIM] = v_ref[rows, cols]
                vaug[hh, rows, HEAD_DIM:] = ones_col
                return carry
            lax.fori_loop(0, nb, fill, 0)

    own = pl.ds(pl.multiple_of(i * MOBA_BLOCK, MOBA_BLOCK), MOBA_BLOCK)
    head_cols = [slice(hh * HEAD_DIM, (hh + 1) * HEAD_DIM) for hh in range(hps)]

    def select(hh):
        q = q_ref[:, head_cols[hh]]
        qs = (q * (SOFTMAX_SCALE * np.log2(np.e))).astype(BF16)
        kmean = (ks_ref[0][:, head_cols[hh]] * (1.0 / MOBA_BLOCK)).astype(BF16)
        gate = _nt_dot(kmean, q.astype(BF16))
        blk = lax.broadcasted_iota(jnp.int32, (nb, tq), 0)
        blk_f = blk.astype(F32)
        gate = jnp.where(blk < i, gate, -jnp.inf)
        sel = jnp.zeros((nb, tq), F32)
        for _ in range(MOBA_TOP_K):
            mx = jnp.max(gate, axis=0, keepdims=True)
            first = jnp.min(jnp.where(gate == mx, blk_f, float(nb)), axis=0, keepdims=True)
            hit = blk_f == first
            sel = jnp.where(hit & (mx > -jnp.inf), 1.0, sel)
            gate = jnp.where(hit, -jnp.inf, gate)
        sel_pad = jnp.concatenate([sel, jnp.zeros((LANES - nb, tq), F32)], axis=0).astype(BF16)
        sel_q = _nt_dot(eye[...], sel_pad)
        bias = jnp.where(sel_q > 0.5, 0.0, MASKED).astype(BF16)
        return jnp.concatenate([qs, bias], axis=1)

    def variant(n_chunks):
        heads_ = range(hps)
        qa = [select(hh) for hh in heads_]
        r_id = lax.broadcasted_iota(jnp.int32, (tq, MOBA_BLOCK), 0)
        c_id = lax.broadcasted_iota(jnp.int32, (tq, MOBA_BLOCK), 1)
        s_own = [jnp.where(c_id <= r_id, _nt_dot(qa[hh][:, 0:HEAD_DIM], kaug[hh, own, 0:HEAD_DIM]), MASKED)
                 for hh in heads_]
        run = [jnp.maximum(s[:, 0:LANES], s[:, LANES:]) for s in s_own]
        for c in range(n_chunks):
            for hh in heads_:
                s_c = _nt_dot(qa[hh], kaug[hh, c * ck:(c + 1) * ck, :])
                s_sc[hh, :, c * ck:(c + 1) * ck] = s_c
                for u in range(0, ck, LANES):
                    run[hh] = jnp.maximum(run[hh], s_c[:, u:u + LANES])
        m = [jnp.max(r, axis=1, keepdims=True) for r in run]
        acc = [jnp.dot(jnp.exp2(s_own[hh] - m[hh]).astype(BF16), vaug[hh, own, :], preferred_element_type=F32)
               for hh in heads_]
        for c in range(n_chunks):
            for hh in heads_:
                p_c = jnp.exp2(s_sc[hh, :, c * ck:(c + 1) * ck] - m[hh]).astype(BF16)
                acc[hh] += jnp.dot(p_c, vaug[hh, c * ck:(c + 1) * ck, :], preferred_element_type=F32)
        for hh in heads_:
            o_ref[:, head_cols[hh]] = (acc[hh][:, 0:HEAD_DIM] / acc[hh][:, HEAD_DIM:HEAD_DIM + 1]).astype(o_ref.dtype)

    n_needed = (i + chunk_blocks - 1) // chunk_blocks
    for n in range(n_chunk_max + 1):
        @pl.when(n_needed == n)
        def _(n=n):
            side_work()
            variant(n)


def _scan_queries(q_ref):
    t = q_ref.shape[1]
    qs = q_ref[0] * SOFTMAX_SCALE
    q_hi = qs.astype(BF16).astype(F32)
    q_lo = (qs - q_hi).astype(BF16).astype(F32)
    return jnp.concatenate([q_hi, q_lo, jnp.zeros((16 - 2 * t, qs.shape[1]), F32)], axis=0).astype(BF16)


def _cache_scan_pages(p, page_refs, ksum_sc, *, heads, page):
    pps = len(page_refs)
    ppb = MOBA_BLOCK // page
    bps = pps // ppb
    sums = [jnp.sum(ref[...].reshape(page, heads, HEAD_DIM), axis=0) for ref in page_refs]
    for b in range(bps):
        blk_sum = functools.reduce(lambda x, y: x + y, sums[b * ppb:(b + 1) * ppb])
        ksum_sc[pl.ds(pl.multiple_of((p * bps + b) * heads, heads), heads), :] = blk_sum


def _cache_scan_gate(q_ref, ti_ref, ksum_sc, *, heads):
    t = q_ref.shape[1]
    q16 = _scan_queries(q_ref)
    nb = ksum_sc.shape[0] // heads
    lane = lax.broadcasted_iota(jnp.int32, (16, LANES), 1)
    lane_f = lane.astype(F32)
    for h in range(heads):
        kmean = ksum_sc[pl.ds(h, nb, stride=heads), :] * (1.0 / MOBA_BLOCK)
        km = jnp.concatenate([kmean, jnp.zeros((LANES - nb, HEAD_DIM), F32)], axis=0)
        km_hi, km_lo = _split_bf16(km)
        q_h = q16[:, h * HEAD_DIM:(h + 1) * HEAD_DIM]
        r1 = _nt_dot(q_h, km_hi)
        r2 = _nt_dot(q_h, km_lo)
        gate = r1 + pltpu.roll(r1, 16 - t, 0) + r2
        gate = jnp.where(lane < nb, gate, -jnp.inf)
        out = jnp.zeros((16, LANES), jnp.int32)
        for kk in range(MOBA_TOP_K):
            mx = jnp.max(gate, axis=1, keepdims=True)
            first = jnp.min(jnp.where(gate == mx, lane_f, float(LANES)), axis=1, keepdims=True)
            idx = jnp.where(mx > -jnp.inf, first, -1.0).astype(jnp.int32)
            out = jnp.where(lane == kk, idx, out)
            gate = jnp.where(lane_f == first, -jnp.inf, gate)
        ti_ref[0, h] = out[0:8]


def _query_block_order(i, n):
    return jnp.where(i % 2 == 0, i // 2, n - 1 - i // 2)


def _attn_scan_kernel(pt_ref, q_ref, k_ref, v_ref, ks_ref, qs_ref, cache_hbm, o_ref, ti_ref,
                      kaug, vaug, eye, s_sc, ksum_sc, pbuf, psem, *, hps, heads, page, scan_steps):
    n_slots, pps = pbuf.shape[0], pbuf.shape[1]
    page_rows = page * heads
    i = pl.program_id(2)
    flat = (pl.program_id(0) * pl.num_programs(1) + pl.program_id(1)) * pl.num_programs(2) + i
    total = pl.num_programs(0) * pl.num_programs(1) * pl.num_programs(2)
    p = flat % scan_steps
    slot = flat % n_slots

    def page_copies(step):
        seq_id, first = step // scan_steps, (step % scan_steps) * pps
        return [pltpu.make_async_copy(
            cache_hbm.at[pl.ds(pl.multiple_of(pt_ref[seq_id, first + u] * page_rows, page_rows), page_rows), :],
            pbuf.at[step % n_slots, u], psem.at[step % n_slots]) for u in range(pps)]

    @pl.when(flat == 0)
    def _():
        for ahead in range(n_slots - 1):
            for cp in page_copies(ahead):
                cp.start(priority=BULK_DMA_PRIORITY)

    @pl.when(flat + n_slots - 1 < total)
    def _():
        for cp in page_copies(flat + n_slots - 1):
            cp.start(priority=BULK_DMA_PRIORITY)

    for cp in page_copies(flat):
        cp.wait()
    page_refs = [pbuf.at[slot, u] for u in range(pps)]

    def scan_pages():
        _cache_scan_pages(p, page_refs, ksum_sc, heads=heads, page=page)

    _attn_prompt_step(i == 0, _query_block_order(i, pl.num_programs(2)), q_ref, k_ref, v_ref, ks_ref, o_ref,
                      kaug, vaug, eye, s_sc, hps=hps, chunk_blocks=ATTN_CHUNK_BLOCKS, side_work=scan_pages)

    @pl.when(p == scan_steps - 1)
    def _():
        _cache_scan_gate(qs_ref, ti_ref, ksum_sc, heads=heads)


def _attn_scan(page_table, q, k, v, ksum, q_s, cache, *, batch, seq, page):
    rows, attn_dim = q.shape
    heads = attn_dim // HEAD_DIM
    tq = MOBA_BLOCK
    nq = seq // tq
    nb = seq // MOBA_BLOCK
    hps = ATTN_HEADS_PER_STEP
    n_hp = heads // hps
    assert nb <= LANES and nb % ATTN_CHUNK_BLOCKS == 0 and heads % hps == 0

    nseq, n_pages = page_table.shape
    t = q_s.shape[1]
    past = n_pages * page
    nb_past = past // MOBA_BLOCK
    n_steps = batch * n_hp * nq
    assert n_steps % nseq == 0
    scan_steps = n_steps // nseq
    assert n_pages % scan_steps == 0
    pps = n_pages // scan_steps
    assert (pps * page) % MOBA_BLOCK == 0 and MOBA_BLOCK % page == 0
    assert nb_past <= LANES and 2 * t <= 8 and (pps * page // MOBA_BLOCK) % 8 == 0

    def flat(b, h, i):
        return (b * n_hp + h) * nq + i

    q_spec = pl.BlockSpec((tq, hps * HEAD_DIM), lambda b, h, i, pt: (b * nq + _query_block_order(i, nq), h))
    kv_spec = pl.BlockSpec((seq, hps * HEAD_DIM), lambda b, h, i, pt: (b, h))
    assert n_steps >= SCAN_SLOTS
    grid_spec = pltpu.PrefetchScalarGridSpec(
        num_scalar_prefetch=1,
        grid=(batch, n_hp, nq),
        in_specs=[q_spec, kv_spec, kv_spec,
                  pl.BlockSpec((1, nb, hps * HEAD_DIM), lambda b, h, i, pt: (b, 0, h)),
                  pl.BlockSpec((1, t, attn_dim), lambda b, h, i, pt: (flat(b, h, i) // scan_steps, 0, 0)),
                  pl.BlockSpec(memory_space=pl.ANY)],
        out_specs=[
            q_spec,
            pl.BlockSpec((1, heads, 8, LANES), lambda b, h, i, pt: (flat(b, h, i) // scan_steps, 0, 0, 0)),
        ],
        scratch_shapes=[
            pltpu.VMEM((hps, seq, 2 * HEAD_DIM), BF16),
            pltpu.VMEM((hps, seq, 2 * HEAD_DIM), BF16),
            pltpu.VMEM((tq, tq), BF16),
            pltpu.VMEM((hps, tq, seq), F32),
            pltpu.VMEM((nb_past * heads, HEAD_DIM), F32),
            pltpu.VMEM((SCAN_SLOTS, pps, page * heads, HEAD_DIM), F32),
            pltpu.SemaphoreType.DMA((SCAN_SLOTS,)),
        ],
    )
    return pl.pallas_call(
        functools.partial(_attn_scan_kernel, hps=hps, heads=heads, page=page, scan_steps=scan_steps),
        grid_spec=grid_spec,
        out_shape=[
            jax.ShapeDtypeStruct((rows, attn_dim), BF16),
            jax.ShapeDtypeStruct((nseq, heads, 8, LANES), jnp.int32),
        ],
        compiler_params=_params("arbitrary", "arbitrary", "arbitrary"),
        name="attn_scan",
    )(page_table, q, k, v, ksum.reshape(batch, nb, attn_dim), q_s, cache)


def _gather_copies(pt_ref, ti_ref, ck_hbm, cv_hbm, kbuf, vbuf, sem, step, slot, k, *, units, heads, nsel,
                   wait_only=False):
    page = cv_hbm.shape[1]
    ppb = MOBA_BLOCK // page
    unit = step * units + k
    s_, h_ = unit // heads, unit % heads
    out = []
    for c in range(nsel):
        blk = None if wait_only else jnp.maximum(ti_ref[s_, h_ * nsel + c], 0)
        for e in range(ppb):
            pg = 0 if wait_only else pt_ref[s_, blk * ppb + e]
            for src, dst in ((ck_hbm, kbuf), (cv_hbm, vbuf)):
                out.append(pltpu.make_async_copy(
                    src.at[pg, :, 0 if wait_only else h_, :], dst.at[slot, (k * nsel + c) * ppb + e],
                    sem.at[slot, k]))
    return out


def _sample_attend_probs(q, kn, kblk, oks):
    t = q.shape[0]
    nsel = len(oks)
    qs = q * SOFTMAX_SCALE
    width = nsel * MOBA_BLOCK
    seg_w = MOBA_TOP_K * MOBA_BLOCK
    q16 = jnp.concatenate([qs, jnp.zeros((16 - t, HEAD_DIM), F32)], axis=0).astype(BF16)
    lt = _nt_dot(q16, kblk.astype(BF16))[0:t]
    segs = [jnp.where(oks[c], lt[:, c * MOBA_BLOCK:(c + 1) * MOBA_BLOCK], MASKED) for c in range(nsel)]
    lm = jnp.concatenate(segs, axis=1)
    col = lax.broadcasted_iota(jnp.int32, (t, width), 1)
    row = lax.broadcasted_iota(jnp.int32, (t, width), 0)
    lm = jnp.where((col >= row * seg_w) & (col < (row + 1) * seg_w), lm, MASKED)
    row_o = lax.broadcasted_iota(jnp.int32, (t, LANES), 0)
    col_o = lax.broadcasted_iota(jnp.int32, (t, LANES), 1)
    own = jnp.full((t, LANES), MASKED, F32)
    for j in range(t):
        dj = jnp.sum(qs * kn[j:j + 1, :], axis=1, keepdims=True)
        own = jnp.where((col_o == j) & (j <= row_o), dj, own)
    full = jnp.concatenate([lm, own], axis=1)
    m = jnp.max(full, axis=1, keepdims=True)
    p = jnp.exp(full - m)
    return p, jnp.sum(p, axis=1, keepdims=True)


def _sample_attend_values(p, l, vn, vblk):
    t = vn.shape[0]
    width = vblk.shape[0]
    p_pad = jnp.concatenate([p[:, :width], jnp.zeros((8 - t, width), F32)], axis=0).astype(BF16)
    pv = jnp.dot(p_pad, vblk.astype(BF16), preferred_element_type=F32)[0:t]
    for j in range(t):
        pv += p[:, width + j:width + j + 1] * vn[j:j + 1, :]
    return pv / l


def _post_attn_kernel(*refs, gather):
    if gather:
        pt_ref, ti_ref, *refs = refs
    x_ref, o_ref, mod_ref, g_ref, wo_ref, wup_ref, wdown_ref, *refs = refs
    if gather:
        qs_ref, kn_ref, vn_ref, ck_hbm, cv_hbm, y_ref, os_ref, kbuf, vbuf, sem = refs
        units, heads = gather["units"], gather["heads"]
        t = qs_ref.shape[1]
        nsel = t * MOBA_TOP_K
        assert kbuf.shape[0] == 2
        j = pl.program_id(0)
        last = pl.num_programs(0) - 1
        slot = j % 2
        copies = functools.partial(_gather_copies, pt_ref, ti_ref, ck_hbm, cv_hbm, kbuf, vbuf, sem,
                                   units=units, heads=heads, nsel=nsel)

        @pl.when(j == 0)
        def _():
            for k in range(units):
                for cp in copies(0, 0, k):
                    cp.start(priority=BULK_DMA_PRIORITY)

        rows_per_unit = kbuf.shape[1] // units
        probs = {}
        ahead = [copies(jnp.minimum(j + 1, last), 1 - slot, k) for k in range(units)]
        landed = [copies(j, slot, k, wait_only=True) for k in range(units)]
        valid = [[ti_ref[(j * units + k) // heads, ((j * units + k) % heads) * nsel + c] >= 0
                  for c in range(nsel)] for k in range(units)]

        def begin(k):
            for cp in ahead[k]:
                cp.start(priority=BULK_DMA_PRIORITY)
            for cp in landed[k]:
                cp.wait()
            cols = slice(k * HEAD_DIM, (k + 1) * HEAD_DIM)
            kblk = kbuf[slot, k * rows_per_unit:(k + 1) * rows_per_unit].reshape(nsel * MOBA_BLOCK, HEAD_DIM)
            probs[k] = _sample_attend_probs(qs_ref[0][:, cols], kn_ref[0][:, cols], kblk, valid[k])

        def finish(k):
            cols = slice(k * HEAD_DIM, (k + 1) * HEAD_DIM)
            vblk = vbuf[slot, k * rows_per_unit:(k + 1) * rows_per_unit].reshape(nsel * MOBA_BLOCK, HEAD_DIM)
            os_ref[0, :, cols] = _sample_attend_values(*probs.pop(k), vn_ref[0][:, cols], vblk)

        def after_chunk(i):
            if 0 < i <= units:
                finish(i - 1)
            if i + 1 < units:
                begin(i + 1)
            if i == n_chunks - 1:
                for k in sorted(probs):
                    finish(k)
    else:
        (y_ref,) = refs
        units = 0
        begin = after_chunk = None
    d = x_ref.shape[1]
    n_chunks = wup_ref.shape[1] // d
    m = mod_ref[0]
    a = jnp.dot(o_ref[...].astype(BF16), wo_ref[...], preferred_element_type=F32)
    x1 = x_ref[...] + m[:, 2 * d:3 * d] * a
    h = _norm_mod(x1, g_ref[...], m[:, 3 * d:4 * d], m[:, 4 * d:5 * d])
    if units > 0:
        begin(0)
    y_ref[...] = x1 + m[:, 5 * d:6 * d] * _mlp(h, wup_ref, wdown_ref, after_chunk)
    if gather:
        assert not probs and units <= n_chunks, "more sample-attention units than MLP chunks"

        @pl.when(j == last)
        def _():
            for group in ahead:
                for cp in group:
                    cp.wait()


def _post_attn(x, o, mod, g, wo, wup, wdown, *, layer, tm, tiles_per_seq, sample=None):
    rows, d = x.shape
    n_tiles = rows // tm
    row_spec = pl.BlockSpec((tm, d), lambda i, *_: (i, 0))
    in_specs = [
        row_spec,
        pl.BlockSpec((tm, o.shape[1]), lambda i, *_: (i, 0)),
        pl.BlockSpec((1, mod.shape[1], mod.shape[2]), lambda i, *_: (i // tiles_per_seq, 0, 0)),
        _const_spec((1, d)),
        _const_spec(wo.shape), _layer_spec(wup.shape, layer), _layer_spec(wdown.shape, layer),
    ]
    args = [x, o, mod, g, wo, wup, wdown]
    if sample is None:
        return pl.pallas_call(
            functools.partial(_post_attn_kernel, gather=None),
            grid=(n_tiles,),
            in_specs=in_specs,
            out_specs=row_spec,
            out_shape=jax.ShapeDtypeStruct((rows, d), F32),
            compiler_params=_params("arbitrary"),
            name="post_attn",
        )(*args)

    page_table, top_idx, q_s, k_s, v_s, cache_k, cache_v = sample
    nseq, t, attn_dim = q_s.shape
    heads = attn_dim // HEAD_DIM
    page = cache_v.shape[1]
    nsel = t * MOBA_TOP_K
    ppb = MOBA_BLOCK // page
    assert (nseq * heads) % n_tiles == 0
    units = nseq * heads // n_tiles
    assert heads % units == 0 and n_tiles >= GATHER_SLOTS
    groups = heads // units
    new_spec = pl.BlockSpec((1, t, units * HEAD_DIM), lambda i, *_: (i // groups, 0, i % groups))
    gather_buf = pltpu.VMEM((GATHER_SLOTS, units * nsel * ppb, page, HEAD_DIM), F32)
    grid_spec = pltpu.PrefetchScalarGridSpec(
        num_scalar_prefetch=2,
        grid=(n_tiles,),
        in_specs=in_specs + [new_spec, new_spec, new_spec,
                             pl.BlockSpec(memory_space=pl.ANY), pl.BlockSpec(memory_space=pl.ANY)],
        out_specs=[row_spec, new_spec],
        scratch_shapes=[gather_buf, gather_buf, pltpu.SemaphoreType.DMA((GATHER_SLOTS, units))],
    )
    return pl.pallas_call(
        functools.partial(_post_attn_kernel, gather=dict(units=units, heads=heads)),
        grid_spec=grid_spec,
        out_shape=[jax.ShapeDtypeStruct((rows, d), F32), jax.ShapeDtypeStruct((nseq, t, attn_dim), F32)],
        compiler_params=_params("arbitrary"),
        name="post_attn_gather",
    )(page_table, top_idx, *args, q_s, k_s, v_s, cache_k, cache_v)


def _conv_layer_kernel(x_ref, mod_ref, g1_ref, g2_ref, gf_ref, wbcx_ref, wconv_ref, wout_ref, wup_ref, wdown_ref,
                       *rest, tiles_per_seq, period, tail):
    if period is None:
        y_ref, ut_ref, ext = rest
    else:
        e1_ref, e2_ref, y_ref, ut_ref, ext = rest
    i = pl.program_id(0)
    tm, d = x_ref.shape
    halo = 8
    sub = min(tm, CONV_SUB_ROWS)
    subs = [slice(r, r + sub) for r in range(0, tm, sub)]
    m_all = mod_ref[0]
    ms_ = [m_all if m_all.shape[0] == 1 else m_all[s] for s in subs]
    x = [x_ref[s, :] for s in subs]
    h = [_norm_mod(x[n], g1_ref[...], ms_[n][:, 0:d], ms_[n][:, d:2 * d]).astype(BF16) for n in range(len(subs))]
    bcx = [[jnp.dot(hn, wbcx_ref[:, c * d:(c + 1) * d], preferred_element_type=F32) for c in range(3)] for hn in h]
    u = [cg * xin for _, cg, xin in bcx]

    @pl.when(i % tiles_per_seq == 0)
    def _():
        ext[0:halo, :] = jnp.zeros((halo, d), F32)

    w = wconv_ref[...]
    gated = []
    for n, s in enumerate(subs):
        ext[halo + s.start:halo + s.stop, :] = u[n]
        prev1 = ext[halo - 1 + s.start:halo - 1 + s.stop, :]
        prev2 = ext[halo - 2 + s.start:halo - 2 + s.stop, :]
        if period is not None:
            tpos = (lax.broadcasted_iota(jnp.int32, (sub, d), 0) + s.start) % period
            prev1 = jnp.where(tpos < 1, e1_ref[s, :], prev1)
            prev2 = jnp.where(tpos < 2, e2_ref[s, :], prev2)
        conv = w[0:1, :] * prev2 + w[1:2, :] * prev1 + w[2:3, :] * u[n]
        gated.append((bcx[n][0] * conv).astype(BF16))
    ext[0:halo, :] = ext[tm:tm + halo, :]
    ut_ref[0] = ext[halo + tm - tail:halo + tm, :]

    y = [jnp.dot(gn, wout_ref[...], preferred_element_type=F32) for gn in gated]
    x1 = [x[n] + ms_[n][:, 2 * d:3 * d] * y[n] for n in range(len(subs))]
    h2 = [_norm_mod(x1[n], g2_ref[...], ms_[n][:, 3 * d:4 * d], ms_[n][:, 4 * d:5 * d]) for n in range(len(subs))]
    mlp = _mlp_tiles(h2, wup_ref, wdown_ref)
    for n, s in enumerate(subs):
        x2 = x1[n] + ms_[n][:, 5 * d:6 * d] * mlp[n]
        ms2 = jnp.mean(x2 * x2, axis=-1, keepdims=True)
        y_ref[s, :] = x2 * lax.rsqrt(ms2 + RMS_EPS) * gf_ref[...]


def _conv_layer(x, mod, g1, g2, gf, wbcx, wconv, wout, wup, wdown, state_rows,
                *, layer, tm, tiles_per_seq, period, tail):
    rows, d = x.shape
    n_tiles = rows // tm
    row_spec = pl.BlockSpec((tm, d), lambda i: (i, 0))
    in_specs = [
        row_spec,
        pl.BlockSpec((1, mod.shape[1], mod.shape[2]), lambda i: (i // tiles_per_seq, 0, 0)),
        _const_spec((1, d)), _const_spec((1, d)), _const_spec((1, d)),
        _const_spec(wbcx.shape), _const_spec(wconv.shape), _const_spec(wout.shape),
        _layer_spec(wup.shape, layer), _layer_spec(wdown.shape, layer),
    ]
    args = [x, mod, g1, g2, gf, wbcx, wconv, wout, wup, wdown]
    if period is not None:
        in_specs += [row_spec, row_spec]
        args += list(state_rows)
    return pl.pallas_call(
        functools.partial(_conv_layer_kernel, tiles_per_seq=tiles_per_seq, period=period, tail=tail),
        grid=(n_tiles,),
        in_specs=in_specs,
        out_specs=[row_spec, pl.BlockSpec((1, tail, d), lambda i: (i, 0, 0))],
        out_shape=[jax.ShapeDtypeStruct((rows, d), F32), jax.ShapeDtypeStruct((n_tiles, tail, d), F32)],
        scratch_shapes=[pltpu.VMEM((tm + 8, d), F32)],
        compiler_params=_params("arbitrary"),
        name="conv_layer",
    )(*args)


def kernel(x_prompt, x_sample, cache_k, cache_v, state_conv, page_table, c_prompt, c_sample, w_ada, b_ada, g_norm,
           w_qkv, w_o, w_bcx, w_conv, w_out, w_up, w_down, g_final):
    batch, seq, d = x_prompt.shape
    nseq, t, _ = x_sample.shape
    depth = w_ada.shape[0]
    n_layers_attn, pool, page, heads, head_dim = cache_k.shape
    attn_dim = heads * head_dim
    n_pages = page_table.shape[1]
    past = n_pages * page
    kw = state_conv.shape[2] + 1
    assert depth == 2 and n_layers_attn == 1 and state_conv.shape[0] == 1, "layer 0 attention, layer 1 convolution"
    assert head_dim == HEAD_DIM and kw == 3 and t >= kw - 1
    assert seq % ROW_TILE == 0 and ROW_TILE % MOBA_BLOCK == 0 and past % MOBA_BLOCK == 0 and t <= MOBA_BLOCK
    assert seq % WIDE_ROW_TILE == 0 and WIDE_ROW_TILE % CONV_SUB_ROWS == 0
    rows_p, rows_s = batch * seq, nseq * t
    assert rows_s % 8 == 0

    pad = -(rows_s + batch) % 8
    c_all = jnp.concatenate([jnp.repeat(c_sample, t, axis=0), c_prompt, jnp.zeros((pad, d), F32)], axis=0)
    mod = _ada(c_all, w_ada, b_ada)
    mod_s = mod[:, None, :rows_s]
    mod_p = mod[:, rows_s:rows_s + batch, None, :]

    bf = lambda a: a.astype(BF16)
    wqkv, wo, wbcx, wout = bf(w_qkv[0]), bf(w_o[0]), bf(w_bcx[0]), bf(w_out[0])
    wup, wdown = bf(w_up), bf(w_down)
    g = g_norm.reshape(depth, 2, 1, d)
    gf = g_final.reshape(1, d)

    xp = x_prompt.reshape(rows_p, d)
    xs = x_sample.reshape(rows_s, d)
    tps = seq // ROW_TILE
    tab_p = _rope_tables(np.arange(seq))
    tab_s = _rope_tables(past + (np.arange(rows_s) % t))

    q_p, k_p, v_p, kb_p, vb_p, ksum = _qkv(xp, mod_p[0], g[0, 0], wqkv, tab_p, tm=WIDE_ROW_TILE,
                                           tiles_per_seq=seq // WIDE_ROW_TILE,
                                           with_ksum=True)
    q_s, k_s, v_s, kb_s, vb_s = _qkv(xs, mod_s[0], g[0, 0], wqkv, tab_s, tm=rows_s, tiles_per_seq=1,
                                     with_ksum=False)
    q_s3, kn_s3, vn_s3 = (a.astype(F32).reshape(nseq, t, attn_dim) for a in (q_s, kb_s, vb_s))
    o_p, top = _attn_scan(page_table, q_p, kb_p, vb_p, ksum, q_s3,
                          cache_k.reshape(pool * page * heads, head_dim), batch=batch, seq=seq, page=page)
    top_idx = top[:, :, :t, :MOBA_TOP_K].reshape(nseq, heads * t * MOBA_TOP_K)
    sample = (page_table, top_idx, q_s3, kn_s3, vn_s3, cache_k.reshape(pool, page, heads, head_dim),
              cache_v.reshape(pool, page, heads, head_dim))
    x1_p, o_s = _post_attn(xp, o_p, mod_p[0], g[0, 1], wo, wup, wdown, layer=0, tm=ROW_TILE, tiles_per_seq=tps,
                           sample=sample)
    x1_s = _post_attn(xs, o_s.reshape(rows_s, attn_dim), mod_s[0], g[0, 1], wo, wup, wdown, layer=0,
                      tm=rows_s, tiles_per_seq=1)

    y_p, ut_p = _conv_layer(x1_p, mod_p[1], g[1, 0], g[1, 1], gf, wbcx, w_conv[0], wout, wup, wdown, None, layer=1,
                            tm=WIDE_ROW_TILE, tiles_per_seq=seq // WIDE_ROW_TILE, period=None, tail=8)
    st = state_conv[0]
    zeros = jnp.zeros((nseq, t - 1, d), F32)
    e1 = jnp.concatenate([st[:, 1:2], zeros], axis=1).reshape(rows_s, d)
    e2 = jnp.concatenate([st[:, 0:2], zeros[:, 1:]], axis=1).reshape(rows_s, d)
    y_s, ut_s = _conv_layer(x1_s, mod_s[1], g[1, 0], g[1, 1], gf, wbcx, w_conv[0], wout, wup, wdown, (e1, e2),
                            layer=1, tm=rows_s, tiles_per_seq=1, period=t, tail=rows_s)

    conv_p = ut_p.reshape(batch, seq // WIDE_ROW_TILE, 8, d)[:, -1, 8 - (kw - 1):, :]
    conv_s = ut_s.reshape(nseq, t, d)[:, t - (kw - 1):, :]
    return (
        y_p.reshape(batch, seq, d),
        y_s.reshape(nseq, t, d),
        k_p.reshape(1, batch, seq, heads, head_dim),
        v_p.reshape(1, batch, seq, heads, head_dim),
        k_s.reshape(1, nseq, t, heads, head_dim),
        v_s.reshape(1, nseq, t, heads, head_dim),
        conv_p[None],
        conv_s[None],
    )
```

```python
import functools

import numpy as np
import jax
import jax.numpy as jnp
from jax import lax
from jax.experimental import pallas as pl
from jax.experimental.pallas import tpu as pltpu

F32 = jnp.float32
BF16 = jnp.bfloat16

HEAD_DIM = 128
MOBA_BLOCK = 256
MOBA_TOP_K = 3
ROPE_DIM = HEAD_DIM // 4
ROPE_THETA = 500000.0
RMS_EPS = 1e-6
SOFTMAX_SCALE = HEAD_DIM ** -0.5
MASKED = -1e30

LANES = 128
VMEM_LIMIT = 56 * 1024 * 1024
ROW_TILE = 256
WIDE_ROW_TILE = 512
CONV_SUB_ROWS = 256
ADA_COL_TILE = 1536
SCAN_SLOTS = 3
GATHER_SLOTS = 2
ATTN_HEADS_PER_STEP = 2
ATTN_CHUNK_BLOCKS = 2

NT_DIMS = (((1,), (1,)), ((), ()))


def _nt_dot(a, b):
    return lax.dot_general(a, b, NT_DIMS, preferred_element_type=F32)


def _split_bf16(x):
    hi = x.astype(BF16)
    lo = (x - hi.astype(F32)).astype(BF16)
    return hi, lo


def _const_spec(shape):
    n = len(shape)
    return pl.BlockSpec(shape, lambda *_: (0,) * n, pipeline_mode=pl.Buffered(1))


def _layer_spec(shape, layer):
    n = len(shape) - 1
    return pl.BlockSpec((None,) + tuple(shape[1:]), lambda *_: (layer,) + (0,) * n, pipeline_mode=pl.Buffered(1))


def _params(*sem):
    return pltpu.CompilerParams(dimension_semantics=sem, vmem_limit_bytes=VMEM_LIMIT)


def _ada_kernel(c_ref, w_ref, b_ref, o_ref):
    rows = c_ref.shape[0]
    c_hi, c_lo = _split_bf16(c_ref[...])
    w_hi, w_lo = _split_bf16(w_ref[0])
    both = jnp.dot(jnp.concatenate([c_hi, c_lo], axis=0), w_hi, preferred_element_type=F32)
    acc = both[0:rows] + both[rows:] + jnp.dot(c_hi, w_lo, preferred_element_type=F32)
    o_ref[0] = acc + b_ref[0]


def _ada(c_all, w_ada, b_ada):
    depth, d, n6 = w_ada.shape
    rows = c_all.shape[0]
    tn = ADA_COL_TILE
    return pl.pallas_call(
        _ada_kernel,
        grid=(depth, n6 // tn),
        in_specs=[
            pl.BlockSpec((rows, d), lambda l, n: (0, 0)),
            pl.BlockSpec((1, d, tn), lambda l, n: (l, 0, n)),
            pl.BlockSpec((1, 1, tn), lambda l, n: (l, 0, n)),
        ],
        out_specs=pl.BlockSpec((1, rows, tn), lambda l, n: (l, 0, n)),
        out_shape=jax.ShapeDtypeStruct((depth, rows, n6), F32),
        compiler_params=_params("arbitrary", "arbitrary"),
        name="ada",
    )(c_all, w_ada, b_ada.reshape(depth, 1, n6))


def _norm_mod(x, g, shift, scale):
    ms = jnp.mean(x * x, axis=-1, keepdims=True)
    y = x * lax.rsqrt(ms + RMS_EPS) * g
    return y * (1.0 + scale) + shift


def _mlp(h, wup_ref, wdown_ref, after_chunk=None):
    return _mlp_tiles([h], wup_ref, wdown_ref, after_chunk)[0]


def _mlp_tiles(hs, wup_ref, wdown_ref, after_chunk=None):
    d, dff = wup_ref.shape
    hb = [h.astype(BF16) for h in hs]
    acc = [jnp.zeros((h.shape[0], d), F32) for h in hs]
    for i, c in enumerate(range(0, dff, d)):
        for n in range(len(hs)):
            u = jnp.dot(hb[n], wup_ref[:, c:c + d], preferred_element_type=F32)
            u = jnp.square(jnp.maximum(u, 0.0)).astype(BF16)
            acc[n] += jnp.dot(u, wdown_ref[c:c + d, :], preferred_element_type=F32)
        if after_chunk is not None:
            after_chunk(i)
    return acc


def _rope_tables(pos):
    half = ROPE_DIM // 2
    inv = np.float32(ROPE_THETA) ** (-np.arange(0, ROPE_DIM, 2, dtype=np.float32) / np.float32(ROPE_DIM))
    ang = (pos.astype(np.float32)[:, None] * inv.astype(np.float32)[None, :]).astype(np.float32)
    cos = np.cos(ang.astype(np.float64))
    sin = np.sin(ang.astype(np.float64))
    n = pos.shape[0]
    c = np.ones((n, HEAD_DIM), np.float64)
    s1 = np.zeros((n, HEAD_DIM), np.float64)
    s2 = np.zeros((n, HEAD_DIM), np.float64)
    c[:, :half] = cos
    c[:, half:ROPE_DIM] = cos
    s1[:, :half] = -sin
    s2[:, half:ROPE_DIM] = sin
    return tuple(jnp.asarray(t.astype(np.float32)) for t in (c, s1, s2))


def _qkv_kernel(x_ref, mod_ref, g_ref, w_ref, cos_ref, s1_ref, s2_ref, q_ref, k5_ref, v5_ref, kb_ref, vb_ref,
                *ks_ref, attn_dim):
    d = x_ref.shape[1]
    m = mod_ref[0]
    h = _norm_mod(x_ref[...], g_ref[...], m[:, 0:d], m[:, d:2 * d]).astype(BF16)
    cos, s1, s2 = cos_ref[...], s1_ref[...], s2_ref[...]
    tm = h.shape[0]
    heads = attn_dim // HEAD_DIM
    cw = 2 * HEAD_DIM
    for c in range(0, 3 * attn_dim, cw):
        r = jnp.dot(h, w_ref[:, c:c + cw], preferred_element_type=F32)
        if c < 2 * attn_dim:
            rot = []
            for u in range(0, cw, HEAD_DIM):
                z = r[:, u:u + HEAD_DIM]
                z = (z * cos + pltpu.roll(z, HEAD_DIM - ROPE_DIM // 2, 1) * s1
                     + pltpu.roll(z, ROPE_DIM // 2, 1) * s2)
                rot.append(z)
            r = jnp.concatenate(rot, axis=1)
        if c < attn_dim:
            q_ref[:, c:c + cw] = r.astype(q_ref.dtype)
            continue
        is_k = c < 2 * attn_dim
        c0 = c - (attn_dim if is_k else 2 * attn_dim)
        (kb_ref if is_k else vb_ref)[:, c0:c0 + cw] = r.astype(BF16)
        for u in range(0, cw, HEAD_DIM):
            head = (c0 + u) // HEAD_DIM
            (k5_ref if is_k else v5_ref)[pl.ds(head, tm, stride=heads), :] = r[:, u:u + HEAD_DIM]
        if is_k and ks_ref:
            nblk = tm // MOBA_BLOCK
            ks_ref[0][0, :, c0:c0 + cw] = jnp.sum(r.reshape(nblk, MOBA_BLOCK, cw), axis=1)


def _qkv(x, mod, g, w, tables, *, tm, tiles_per_seq, with_ksum, q_dtype):
    rows, d = x.shape
    attn_dim = w.shape[1] // 3
    heads = attn_dim // HEAD_DIM
    n_tiles = rows // tm
    mod_rows = mod.shape[1]
    row_spec = pl.BlockSpec((tm, attn_dim), lambda i: (i, 0))
    tok_head_spec = pl.BlockSpec((tm * heads, HEAD_DIM), lambda i: (i, 0))
    tab_spec = pl.BlockSpec((tm, HEAD_DIM), lambda i: (i % tiles_per_seq, 0))
    out_specs = [row_spec, tok_head_spec, tok_head_spec, row_spec, row_spec]
    out_shape = ([jax.ShapeDtypeStruct((rows, attn_dim), q_dtype)]
                 + [jax.ShapeDtypeStruct((rows * heads, HEAD_DIM), F32)] * 2
                 + [jax.ShapeDtypeStruct((rows, attn_dim), BF16)] * 2)
    if with_ksum:
        nblk = tm // MOBA_BLOCK
        out_specs.append(pl.BlockSpec((1, nblk, attn_dim), lambda i: (i, 0, 0)))
        out_shape.append(jax.ShapeDtypeStruct((n_tiles, nblk, attn_dim), F32))
    return pl.pallas_call(
        functools.partial(_qkv_kernel, attn_dim=attn_dim),
        grid=(n_tiles,),
        in_specs=[
            pl.BlockSpec((tm, d), lambda i: (i, 0)),
            pl.BlockSpec((1, mod_rows, mod.shape[2]), lambda i: (i // tiles_per_seq, 0, 0)),
            _const_spec((1, d)),
            _const_spec(w.shape),
            tab_spec, tab_spec, tab_spec,
        ],
        out_specs=out_specs,
        out_shape=out_shape,
        compiler_params=_params("arbitrary"),
        name="qkv",
    )(x, mod, g, w, *tables)


def _attn_prompt_step(first, i, q_ref, k_ref, v_ref, ks_ref, o_ref, kaug, vaug, eye, s_sc,
                      *, hps, chunk_blocks, side_work):
    tq = q_ref.shape[0]
    seq = k_ref.shape[0]
    nb = seq // MOBA_BLOCK
    ck = chunk_blocks * MOBA_BLOCK
    n_chunk_max = nb // chunk_blocks

    @pl.when(first)
    def _():
        r_id = lax.broadcasted_iota(jnp.int32, (tq, tq), 0)
        c_id = lax.broadcasted_iota(jnp.int32, (tq, tq), 1)
        eye[...] = jnp.where(r_id == c_id, 1.0, 0.0).astype(BF16)
        lane = lax.broadcasted_iota(jnp.int32, (MOBA_BLOCK, LANES), 1)
        ones_col = jnp.where(lane == 0, 1.0, 0.0).astype(BF16)
        for hh in range(hps):
            cols = slice(hh * HEAD_DIM, (hh + 1) * HEAD_DIM)

            def fill(j, carry):
                rows = pl.ds(pl.multiple_of(j * MOBA_BLOCK, MOBA_BLOCK), MOBA_BLOCK)
                kaug[hh, rows, 0:HEAD_DIM] = k_ref[rows, cols]
                kaug[hh, rows, HEAD_DIM:] = jnp.where(lane == j, 1.0, 0.0).astype(BF16)
                vaug[hh, rows, 0:HEAD_DIM] = v_ref[rows, cols]
                vaug[hh, rows, HEAD_DIM:] = ones_col
                return carry
            lax.fori_loop(0, nb, fill, 0)

    own = pl.ds(pl.multiple_of(i * MOBA_BLOCK, MOBA_BLOCK), MOBA_BLOCK)
    head_cols = [slice(hh * HEAD_DIM, (hh + 1) * HEAD_DIM) for hh in range(hps)]

    def select(hh):
        q = q_ref[:, head_cols[hh]]
        qs = (q.astype(F32) * (SOFTMAX_SCALE * np.log2(np.e))).astype(BF16)
        kmean = (ks_ref[0][:, head_cols[hh]] * (1.0 / MOBA_BLOCK)).astype(BF16)
        gate = _nt_dot(kmean, q)
        blk = lax.broadcasted_iota(jnp.int32, (nb, tq), 0)
        blk_f = blk.astype(F32)
        gate = jnp.where(blk < i, gate, -jnp.inf)
        sel = jnp.zeros((nb, tq), F32)
        for _ in range(MOBA_TOP_K):
            mx = jnp.max(gate, axis=0, keepdims=True)
            first = jnp.min(jnp.where(gate == mx, blk_f, float(nb)), axis=0, keepdims=True)
            hit = blk_f == first
            sel = jnp.where(hit & (mx > -jnp.inf), 1.0, sel)
            gate = jnp.where(hit, -jnp.inf, gate)
        sel_pad = jnp.concatenate([sel, jnp.zeros((LANES - nb, tq), F32)], axis=0).astype(BF16)
        sel_q = _nt_dot(eye[...], sel_pad)
        bias = jnp.where(sel_q > 0.5, 0.0, MASKED).astype(BF16)
        return jnp.concatenate([qs, bias], axis=1)

    def variant(n_chunks):
        heads_ = range(hps)
        qa = [select(hh) for hh in heads_]
        r_id = lax.broadcasted_iota(jnp.int32, (tq, MOBA_BLOCK), 0)
        c_id = lax.broadcasted_iota(jnp.int32, (tq, MOBA_BLOCK), 1)
        s_own = [jnp.where(c_id <= r_id, _nt_dot(qa[hh][:, 0:HEAD_DIM], kaug[hh, own, 0:HEAD_DIM]), MASKED)
                 for hh in heads_]
        run = [jnp.maximum(s[:, 0:LANES], s[:, LANES:]) for s in s_own]
        for c in range(n_chunks):
            for hh in heads_:
                s_c = _nt_dot(qa[hh], kaug[hh, c * ck:(c + 1) * ck, :])
                s_sc[hh, :, c * ck:(c + 1) * ck] = s_c
                for u in range(0, ck, LANES):
                    run[hh] = jnp.maximum(run[hh], s_c[:, u:u + LANES])
        m = [jnp.max(r, axis=1, keepdims=True) for r in run]
        acc = [jnp.dot(jnp.exp2(s_own[hh] - m[hh]).astype(BF16), vaug[hh, own, :], preferred_element_type=F32)
               for hh in heads_]
        for c in range(n_chunks):
            for hh in heads_:
                p_c = jnp.exp2(s_sc[hh, :, c * ck:(c + 1) * ck] - m[hh]).astype(BF16)
                acc[hh] += jnp.dot(p_c, vaug[hh, c * ck:(c + 1) * ck, :], preferred_element_type=F32)
        for hh in heads_:
            o_ref[:, head_cols[hh]] = (acc[hh][:, 0:HEAD_DIM] / acc[hh][:, HEAD_DIM:HEAD_DIM + 1]).astype(o_ref.dtype)

    n_needed = (i + chunk_blocks - 1) // chunk_blocks
    for n in range(n_chunk_max + 1):
        @pl.when(n_needed == n)
        def _(n=n):
            side_work()
            variant(n)


def _scan_queries(q_ref):
    t = q_ref.shape[1]
    qs = q_ref[0] * SOFTMAX_SCALE
    q_hi = qs.astype(BF16).astype(F32)
    q_lo = (qs - q_hi).astype(BF16).astype(F32)
    return jnp.concatenate([q_hi, q_lo, jnp.zeros((16 - 2 * t, qs.shape[1]), F32)], axis=0).astype(BF16)


def _cache_scan_pages(p, page_refs, ksum_sc, *, heads, page):
    pps = len(page_refs)
    ppb = MOBA_BLOCK // page
    bps = pps // ppb
    sums = [jnp.sum(ref[...].reshape(page, heads, HEAD_DIM), axis=0) for ref in page_refs]
    for b in range(bps):
        blk_sum = functools.reduce(lambda x, y: x + y, sums[b * ppb:(b + 1) * ppb])
        ksum_sc[pl.ds(pl.multiple_of((p * bps + b) * heads, heads), heads), :] = blk_sum


def _cache_scan_gate(q_ref, ti_ref, ksum_sc, *, heads):
    t = q_ref.shape[1]
    q16 = _scan_queries(q_ref)
    nb = ksum_sc.shape[0] // heads
    lane = lax.broadcasted_iota(jnp.int32, (16, LANES), 1)
    lane_f = lane.astype(F32)
    for h in range(heads):
        kmean = ksum_sc[pl.ds(h, nb, stride=heads), :] * (1.0 / MOBA_BLOCK)
        km = jnp.concatenate([kmean, jnp.zeros((LANES - nb, HEAD_DIM), F32)], axis=0)
        km_hi, km_lo = _split_bf16(km)
        q_h = q16[:, h * HEAD_DIM:(h + 1) * HEAD_DIM]
        r1 = _nt_dot(q_h, km_hi)
        r2 = _nt_dot(q_h, km_lo)
        gate = r1 + pltpu.roll(r1, 16 - t, 0) + r2
        gate = jnp.where(lane < nb, gate, -jnp.inf)
        out = jnp.zeros((16, LANES), jnp.int32)
        for kk in range(MOBA_TOP_K):
            mx = jnp.max(gate, axis=1, keepdims=True)
            first = jnp.min(jnp.where(gate == mx, lane_f, float(LANES)), axis=1, keepdims=True)
            idx = jnp.where(mx > -jnp.inf, first, -1.0).astype(jnp.int32)
            out = jnp.where(lane == kk, idx, out)
            gate = jnp.where(lane_f == first, -jnp.inf, gate)
        ti_ref[0, h] = out[0:8]


def _query_block_order(i, n):
    return jnp.where(i % 2 == 0, i // 2, n - 1 - i // 2)


def _attn_scan_kernel(pt_ref, q_ref, k_ref, v_ref, ks_ref, qs_ref, cache_hbm, o_ref, ti_ref,
                      kaug, vaug, eye, s_sc, ksum_sc, pbuf, psem, *, hps, heads, page, scan_steps):
    n_slots, pps = pbuf.shape[0], pbuf.shape[1]
    page_rows = page * heads
    i = pl.program_id(2)
    flat = (pl.program_id(0) * pl.num_programs(1) + pl.program_id(1)) * pl.num_programs(2) + i
    total = pl.num_programs(0) * pl.num_programs(1) * pl.num_programs(2)
    p = flat % scan_steps
    slot = flat % n_slots

    def page_copies(step):
        seq_id, first = step // scan_steps, (step % scan_steps) * pps
        return [pltpu.make_async_copy(
            cache_hbm.at[pl.ds(pl.multiple_of(pt_ref[seq_id, first + u] * page_rows, page_rows), page_rows), :],
            pbuf.at[step % n_slots, u], psem.at[step % n_slots]) for u in range(pps)]

    @pl.when(flat == 0)
    def _():
        for ahead in range(n_slots - 1):
            for cp in page_copies(ahead):
                cp.start()

    @pl.when(flat + n_slots - 1 < total)
    def _():
        for cp in page_copies(flat + n_slots - 1):
            cp.start()

    for cp in page_copies(flat):
        cp.wait()
    page_refs = [pbuf.at[slot, u] for u in range(pps)]

    def scan_pages():
        _cache_scan_pages(p, page_refs, ksum_sc, heads=heads, page=page)

    _attn_prompt_step(i == 0, _query_block_order(i, pl.num_programs(2)), q_ref, k_ref, v_ref, ks_ref, o_ref,
                      kaug, vaug, eye, s_sc, hps=hps, chunk_blocks=ATTN_CHUNK_BLOCKS, side_work=scan_pages)

    @pl.when(p == scan_steps - 1)
    def _():
        _cache_scan_gate(qs_ref, ti_ref, ksum_sc, heads=heads)


def _attn_scan(page_table, q, k, v, ksum, q_s, cache, *, batch, seq, page):
    rows, attn_dim = q.shape
    heads = attn_dim // HEAD_DIM
    tq = MOBA_BLOCK
    nq = seq // tq
    nb = seq // MOBA_BLOCK
    hps = ATTN_HEADS_PER_STEP
    n_hp = heads // hps
    assert nb <= LANES and nb % ATTN_CHUNK_BLOCKS == 0 and heads % hps == 0

    nseq, n_pages = page_table.shape
    t = q_s.shape[1]
    past = n_pages * page
    nb_past = past // MOBA_BLOCK
    n_steps = batch * n_hp * nq
    assert n_steps % nseq == 0
    scan_steps = n_steps // nseq
    assert n_pages % scan_steps == 0
    pps = n_pages // scan_steps
    assert (pps * page) % MOBA_BLOCK == 0 and MOBA_BLOCK % page == 0
    assert nb_past <= LANES and 2 * t <= 8 and (pps * page // MOBA_BLOCK) % 8 == 0

    def flat(b, h, i):
        return (b * n_hp + h) * nq + i

    q_spec = pl.BlockSpec((tq, hps * HEAD_DIM), lambda b, h, i, pt: (b * nq + _query_block_order(i, nq), h))
    kv_spec = pl.BlockSpec((seq, hps * HEAD_DIM), lambda b, h, i, pt: (b, h))
    assert n_steps >= SCAN_SLOTS
    grid_spec = pltpu.PrefetchScalarGridSpec(
        num_scalar_prefetch=1,
        grid=(batch, n_hp, nq),
        in_specs=[q_spec, kv_spec, kv_spec,
                  pl.BlockSpec((1, nb, hps * HEAD_DIM), lambda b, h, i, pt: (b, 0, h)),
                  pl.BlockSpec((1, t, attn_dim), lambda b, h, i, pt: (flat(b, h, i) // scan_steps, 0, 0)),
                  pl.BlockSpec(memory_space=pl.ANY)],
        out_specs=[
            q_spec,
            pl.BlockSpec((1, heads, 8, LANES), lambda b, h, i, pt: (flat(b, h, i) // scan_steps, 0, 0, 0)),
        ],
        scratch_shapes=[
            pltpu.VMEM((hps, seq, 2 * HEAD_DIM), BF16),
            pltpu.VMEM((hps, seq, 2 * HEAD_DIM), BF16),
            pltpu.VMEM((tq, tq), BF16),
            pltpu.VMEM((hps, tq, seq), F32),
            pltpu.VMEM((nb_past * heads, HEAD_DIM), F32),
            pltpu.VMEM((SCAN_SLOTS, pps, page * heads, HEAD_DIM), F32),
            pltpu.SemaphoreType.DMA((SCAN_SLOTS,)),
        ],
    )
    return pl.pallas_call(
        functools.partial(_attn_scan_kernel, hps=hps, heads=heads, page=page, scan_steps=scan_steps),
        grid_spec=grid_spec,
        out_shape=[
            jax.ShapeDtypeStruct((rows, attn_dim), BF16),
            jax.ShapeDtypeStruct((nseq, heads, 8, LANES), jnp.int32),
        ],
        compiler_params=_params("arbitrary", "arbitrary", "arbitrary"),
        name="attn_scan",
    )(page_table, q, k, v, ksum.reshape(batch, nb, attn_dim), q_s, cache)


def _gather_copies(pt_ref, ti_ref, ck_hbm, cv_hbm, kbuf, vbuf, sem, step, slot, k, *, units, heads, nsel,
                   wait_only=False):
    page = cv_hbm.shape[1]
    ppb = MOBA_BLOCK // page
    unit = step * units + k
    s_, h_ = unit // heads, unit % heads
    out = []
    for c in range(nsel):
        blk = None if wait_only else jnp.maximum(ti_ref[s_, h_ * nsel + c], 0)
        for e in range(ppb):
            pg = 0 if wait_only else pt_ref[s_, blk * ppb + e]
            for src, dst in ((ck_hbm, kbuf), (cv_hbm, vbuf)):
                out.append(pltpu.make_async_copy(
                    src.at[pg, :, 0 if wait_only else h_, :], dst.at[slot, (k * nsel + c) * ppb + e],
                    sem.at[slot, k]))
    return out


def _sample_attend_probs(q, kn, kblk, oks):
    t = q.shape[0]
    nsel = len(oks)
    qs = q * SOFTMAX_SCALE
    width = nsel * MOBA_BLOCK
    seg_w = MOBA_TOP_K * MOBA_BLOCK
    q16 = jnp.concatenate([qs, jnp.zeros((16 - t, HEAD_DIM), F32)], axis=0).astype(BF16)
    lt = _nt_dot(q16, kblk.astype(BF16))[0:t]
    segs = [jnp.where(oks[c], lt[:, c * MOBA_BLOCK:(c + 1) * MOBA_BLOCK], MASKED) for c in range(nsel)]
    lm = jnp.concatenate(segs, axis=1)
    col = lax.broadcasted_iota(jnp.int32, (t, width), 1)
    row = lax.broadcasted_iota(jnp.int32, (t, width), 0)
    lm = jnp.where((col >= row * seg_w) & (col < (row + 1) * seg_w), lm, MASKED)
    row_o = lax.broadcasted_iota(jnp.int32, (t, LANES), 0)
    col_o = lax.broadcasted_iota(jnp.int32, (t, LANES), 1)
    own = jnp.full((t, LANES), MASKED, F32)
    for j in range(t):
        dj = jnp.sum(qs * kn[j:j + 1, :], axis=1, keepdims=True)
        own = jnp.where((col_o == j) & (j <= row_o), dj, own)
    full = jnp.concatenate([lm, own], axis=1)
    m = jnp.max(full, axis=1, keepdims=True)
    p = jnp.exp(full - m)
    return p, jnp.sum(p, axis=1, keepdims=True)


def _sample_attend_values(p, l, vn, vblk):
    t = vn.shape[0]
    width = vblk.shape[0]
    p_pad = jnp.concatenate([p[:, :width], jnp.zeros((8 - t, width), F32)], axis=0).astype(BF16)
    pv = jnp.dot(p_pad, vblk.astype(BF16), preferred_element_type=F32)[0:t]
    for j in range(t):
        pv += p[:, width + j:width + j + 1] * vn[j:j + 1, :]
    return pv / l


def _post_attn_kernel(*refs, gather):
    if gather:
        pt_ref, ti_ref, *refs = refs
    x_ref, o_ref, mod_ref, g_ref, wo_ref, wup_ref, wdown_ref, *refs = refs
    if gather:
        qs_ref, kn_ref, vn_ref, ck_hbm, cv_hbm, y_ref, os_ref, kbuf, vbuf, sem = refs
        units, heads = gather["units"], gather["heads"]
        t = qs_ref.shape[1]
        nsel = t * MOBA_TOP_K
        assert kbuf.shape[0] == 2
        j = pl.program_id(0)
        last = pl.num_programs(0) - 1
        slot = j % 2
        copies = functools.partial(_gather_copies, pt_ref, ti_ref, ck_hbm, cv_hbm, kbuf, vbuf, sem,
                                   units=units, heads=heads, nsel=nsel)

        @pl.when(j == 0)
        def _():
            for k in range(units):
                for cp in copies(0, 0, k):
                    cp.start()

        rows_per_unit = kbuf.shape[1] // units
        probs = {}
        ahead = [copies(jnp.minimum(j + 1, last), 1 - slot, k) for k in range(units)]
        landed = [copies(j, slot, k, wait_only=True) for k in range(units)]
        valid = [[ti_ref[(j * units + k) // heads, ((j * units + k) % heads) * nsel + c] >= 0
                  for c in range(nsel)] for k in range(units)]

        def begin(k):
            for cp in ahead[k]:
                cp.start()
            for cp in landed[k]:
                cp.wait()
            cols = slice(k * HEAD_DIM, (k + 1) * HEAD_DIM)
            kblk = kbuf[slot, k * rows_per_unit:(k + 1) * rows_per_unit].reshape(nsel * MOBA_BLOCK, HEAD_DIM)
            probs[k] = _sample_attend_probs(qs_ref[0][:, cols], kn_ref[0][:, cols], kblk, valid[k])

        def finish(k):
            cols = slice(k * HEAD_DIM, (k + 1) * HEAD_DIM)
            vblk = vbuf[slot, k * rows_per_unit:(k + 1) * rows_per_unit].reshape(nsel * MOBA_BLOCK, HEAD_DIM)
            os_ref[0, :, cols] = _sample_attend_values(*probs.pop(k), vn_ref[0][:, cols], vblk)

        def after_chunk(i):
            if 0 < i <= units:
                finish(i - 1)
            if i + 1 < units:
                begin(i + 1)
            if i == n_chunks - 1:
                for k in sorted(probs):
                    finish(k)
    else:
        (y_ref,) = refs
        units = 0
        begin = after_chunk = None
    d = x_ref.shape[1]
    n_chunks = wup_ref.shape[1] // d
    m = mod_ref[0]
    a = jnp.dot(o_ref[...].astype(BF16), wo_ref[...], preferred_element_type=F32)
    x1 = x_ref[...] + m[:, 2 * d:3 * d] * a
    h = _norm_mod(x1, g_ref[...], m[:, 3 * d:4 * d], m[:, 4 * d:5 * d])
    if units > 0:
        begin(0)
    y_ref[...] = x1 + m[:, 5 * d:6 * d] * _mlp(h, wup_ref, wdown_ref, after_chunk)
    if gather:
        assert not probs and units <= n_chunks, "more sample-attention units than MLP chunks"

        @pl.when(j == last)
        def _():
            for group in ahead:
                for cp in group:
                    cp.wait()


def _post_attn(x, o, mod, g, wo, wup, wdown, *, layer, tm, tiles_per_seq, sample=None):
    rows, d = x.shape
    n_tiles = rows // tm
    row_spec = pl.BlockSpec((tm, d), lambda i, *_: (i, 0))
    in_specs = [
        row_spec,
        pl.BlockSpec((tm, o.shape[1]), lambda i, *_: (i, 0)),
        pl.BlockSpec((1, mod.shape[1], mod.shape[2]), lambda i, *_: (i // tiles_per_seq, 0, 0)),
        _const_spec((1, d)),
        _const_spec(wo.shape), _layer_spec(wup.shape, layer), _layer_spec(wdown.shape, layer),
    ]
    args = [x, o, mod, g, wo, wup, wdown]
    if sample is None:
        return pl.pallas_call(
            functools.partial(_post_attn_kernel, gather=None),
            grid=(n_tiles,),
            in_specs=in_specs,
            out_specs=row_spec,
            out_shape=jax.ShapeDtypeStruct((rows, d), F32),
            compiler_params=_params("arbitrary"),
            name="post_attn",
        )(*args)

    page_table, top_idx, q_s, k_s, v_s, cache_k, cache_v = sample
    nseq, t, attn_dim = q_s.shape
    heads = attn_dim // HEAD_DIM
    page = cache_v.shape[1]
    nsel = t * MOBA_TOP_K
    ppb = MOBA_BLOCK // page
    assert (nseq * heads) % n_tiles == 0
    units = nseq * heads // n_tiles
    assert heads % units == 0 and n_tiles >= GATHER_SLOTS
    groups = heads // units
    new_spec = pl.BlockSpec((1, t, units * HEAD_DIM), lambda i, *_: (i // groups, 0, i % groups))
    gather_buf = pltpu.VMEM((GATHER_SLOTS, units * nsel * ppb, page, HEAD_DIM), F32)
    grid_spec = pltpu.PrefetchScalarGridSpec(
        num_scalar_prefetch=2,
        grid=(n_tiles,),
        in_specs=in_specs + [new_spec, new_spec, new_spec,
                             pl.BlockSpec(memory_space=pl.ANY), pl.BlockSpec(memory_space=pl.ANY)],
        out_specs=[row_spec, new_spec],
        scratch_shapes=[gather_buf, gather_buf, pltpu.SemaphoreType.DMA((GATHER_SLOTS, units))],
    )
    return pl.pallas_call(
        functools.partial(_post_attn_kernel, gather=dict(units=units, heads=heads)),
        grid_spec=grid_spec,
        out_shape=[jax.ShapeDtypeStruct((rows, d), F32), jax.ShapeDtypeStruct((nseq, t, attn_dim), F32)],
        compiler_params=_params("arbitrary"),
        name="post_attn_gather",
    )(page_table, top_idx, *args, q_s, k_s, v_s, cache_k, cache_v)


def _conv_layer_kernel(x_ref, mod_ref, g1_ref, g2_ref, gf_ref, wbcx_ref, wconv_ref, wout_ref, wup_ref, wdown_ref,
                       *rest, tiles_per_seq, period, tail):
    if period is None:
        y_ref, ut_ref, ext = rest
    else:
        e1_ref, e2_ref, y_ref, ut_ref, ext = rest
    i = pl.program_id(0)
    tm, d = x_ref.shape
    halo = 8
    sub = min(tm, CONV_SUB_ROWS)
    subs = [slice(r, r + sub) for r in range(0, tm, sub)]
    m_all = mod_ref[0]
    ms_ = [m_all if m_all.shape[0] == 1 else m_all[s] for s in subs]
    x = [x_ref[s, :] for s in subs]
    h = [_norm_mod(x[n], g1_ref[...], ms_[n][:, 0:d], ms_[n][:, d:2 * d]).astype(BF16) for n in range(len(subs))]
    bcx = [[jnp.dot(hn, wbcx_ref[:, c * d:(c + 1) * d], preferred_element_type=F32) for c in range(3)] for hn in h]
    u = [cg * xin for _, cg, xin in bcx]

    @pl.when(i % tiles_per_seq == 0)
    def _():
        ext[0:halo, :] = jnp.zeros((halo, d), F32)

    w = wconv_ref[...]
    gated = []
    for n, s in enumerate(subs):
        ext[halo + s.start:halo + s.stop, :] = u[n]
        prev1 = ext[halo - 1 + s.start:halo - 1 + s.stop, :]
        prev2 = ext[halo - 2 + s.start:halo - 2 + s.stop, :]
        if period is not None:
            tpos = (lax.broadcasted_iota(jnp.int32, (sub, d), 0) + s.start) % period
            prev1 = jnp.where(tpos < 1, e1_ref[s, :], prev1)
            prev2 = jnp.where(tpos < 2, e2_ref[s, :], prev2)
        conv = w[0:1, :] * prev2 + w[1:2, :] * prev1 + w[2:3, :] * u[n]
        gated.append((bcx[n][0] * conv).astype(BF16))
    ext[0:halo, :] = ext[tm:tm + halo, :]
    ut_ref[0] = ext[halo + tm - tail:halo + tm, :]

    y = [jnp.dot(gn, wout_ref[...], preferred_element_type=F32) for gn in gated]
    x1 = [x[n] + ms_[n][:, 2 * d:3 * d] * y[n] for n in range(len(subs))]
    h2 = [_norm_mod(x1[n], g2_ref[...], ms_[n][:, 3 * d:4 * d], ms_[n][:, 4 * d:5 * d]) for n in range(len(subs))]
    mlp = _mlp_tiles(h2, wup_ref, wdown_ref)
    for n, s in enumerate(subs):
        x2 = x1[n] + ms_[n][:, 5 * d:6 * d] * mlp[n]
        ms2 = jnp.mean(x2 * x2, axis=-1, keepdims=True)
        y_ref[s, :] = x2 * lax.rsqrt(ms2 + RMS_EPS) * gf_ref[...]


def _conv_layer(x, mod, g1, g2, gf, wbcx, wconv, wout, wup, wdown, state_rows,
                *, layer, tm, tiles_per_seq, period, tail):
    rows, d = x.shape
    n_tiles = rows // tm
    row_spec = pl.BlockSpec((tm, d), lambda i: (i, 0))
    in_specs = [
        row_spec,
        pl.BlockSpec((1, mod.shape[1], mod.shape[2]), lambda i: (i // tiles_per_seq, 0, 0)),
        _const_spec((1, d)), _const_spec((1, d)), _const_spec((1, d)),
        _const_spec(wbcx.shape), _const_spec(wconv.shape), _const_spec(wout.shape),
        _layer_spec(wup.shape, layer), _layer_spec(wdown.shape, layer),
    ]
    args = [x, mod, g1, g2, gf, wbcx, wconv, wout, wup, wdown]
    if period is not None:
        in_specs += [row_spec, row_spec]
        args += list(state_rows)
    return pl.pallas_call(
        functools.partial(_conv_layer_kernel, tiles_per_seq=tiles_per_seq, period=period, tail=tail),
        grid=(n_tiles,),
        in_specs=in_specs,
        out_specs=[row_spec, pl.BlockSpec((1, tail, d), lambda i: (i, 0, 0))],
        out_shape=[jax.ShapeDtypeStruct((rows, d), F32), jax.ShapeDtypeStruct((n_tiles, tail, d), F32)],
        scratch_shapes=[pltpu.VMEM((tm + 8, d), F32)],
        compiler_params=_params("arbitrary"),
        name="conv_layer",
    )(*args)


def kernel(x_prompt, x_sample, cache_k, cache_v, state_conv, page_table, c_prompt, c_sample, w_ada, b_ada, g_norm,
           w_qkv, w_o, w_bcx, w_conv, w_out, w_up, w_down, g_final):
    batch, seq, d = x_prompt.shape
    nseq, t, _ = x_sample.shape
    depth = w_ada.shape[0]
    n_layers_attn, pool, page, heads, head_dim = cache_k.shape
    attn_dim = heads * head_dim
    n_pages = page_table.shape[1]
    past = n_pages * page
    kw = state_conv.shape[2] + 1
    assert depth == 2 and n_layers_attn == 1 and state_conv.shape[0] == 1, "layer 0 attention, layer 1 convolution"
    assert head_dim == HEAD_DIM and kw == 3 and t >= kw - 1
    assert seq % ROW_TILE == 0 and ROW_TILE % MOBA_BLOCK == 0 and past % MOBA_BLOCK == 0 and t <= MOBA_BLOCK
    assert seq % WIDE_ROW_TILE == 0 and WIDE_ROW_TILE % CONV_SUB_ROWS == 0 and WIDE_ROW_TILE % MOBA_BLOCK == 0
    rows_p, rows_s = batch * seq, nseq * t
    assert rows_s % 8 == 0

    pad = -(rows_s + batch) % 16
    c_all = jnp.concatenate([jnp.repeat(c_sample, t, axis=0), c_prompt, jnp.zeros((pad, d), F32)], axis=0)
    mod = _ada(c_all, w_ada, b_ada)
    mod_s = mod[:, None, :rows_s]
    mod_p = mod[:, rows_s:rows_s + batch, None, :]

    bf = lambda a: a.astype(BF16)
    wqkv, wo, wbcx, wout = bf(w_qkv[0]), bf(w_o[0]), bf(w_bcx[0]), bf(w_out[0])
    wup, wdown = bf(w_up), bf(w_down)
    g = g_norm.reshape(depth, 2, 1, d)
    gf = g_final.reshape(1, d)

    xp = x_prompt.reshape(rows_p, d)
    xs = x_sample.reshape(rows_s, d)
    tps = seq // ROW_TILE
    tab_p = _rope_tables(np.arange(seq))
    tab_s = _rope_tables(past + (np.arange(rows_s) % t))

    q_p, k_p, v_p, kb_p, vb_p, ksum = _qkv(xp, mod_p[0], g[0, 0], wqkv, tab_p, tm=WIDE_ROW_TILE,
                                           tiles_per_seq=seq // WIDE_ROW_TILE, with_ksum=True, q_dtype=BF16)
    q_s, k_s, v_s, kb_s, vb_s = _qkv(xs, mod_s[0], g[0, 0], wqkv, tab_s, tm=rows_s, tiles_per_seq=1,
                                     with_ksum=False, q_dtype=F32)
    q_s3, kn_s3, vn_s3 = (a.astype(F32).reshape(nseq, t, attn_dim) for a in (q_s, kb_s, vb_s))
    o_p, top = _attn_scan(page_table, q_p, kb_p, vb_p, ksum, q_s3,
                          cache_k.reshape(pool * page * heads, head_dim), batch=batch, seq=seq, page=page)
    top_idx = top[:, :, :t, :MOBA_TOP_K].reshape(nseq, heads * t * MOBA_TOP_K)
    sample = (page_table, top_idx, q_s3, kn_s3, vn_s3, cache_k.reshape(pool, page, heads, head_dim),
              cache_v.reshape(pool, page, heads, head_dim))
    x1_p, o_s = _post_attn(xp, o_p, mod_p[0], g[0, 1], wo, wup, wdown, layer=0, tm=ROW_TILE, tiles_per_seq=tps,
                           sample=sample)
    x1_s = _post_attn(xs, o_s.reshape(rows_s, attn_dim), mod_s[0], g[0, 1], wo, wup, wdown, layer=0,
                      tm=rows_s, tiles_per_seq=1)

    y_p, ut_p = _conv_layer(x1_p, mod_p[1], g[1, 0], g[1, 1], gf, wbcx, w_conv[0], wout, wup, wdown, None, layer=1,
                            tm=WIDE_ROW_TILE, tiles_per_seq=seq // WIDE_ROW_TILE, period=None, tail=8)
    st = state_conv[0]
    zeros = jnp.zeros((nseq, t - 1, d), F32)
    e1 = jnp.concatenate([st[:, 1:2], zeros], axis=1).reshape(rows_s, d)
    e2 = jnp.concatenate([st[:, 0:2], zeros[:, 1:]], axis=1).reshape(rows_s, d)
    y_s, ut_s = _conv_layer(x1_s, mod_s[1], g[1, 0], g[1, 1], gf, wbcx, w_conv[0], wout, wup, wdown, (e1, e2),
                            layer=1, tm=rows_s, tiles_per_seq=1, period=t, tail=rows_s)

    conv_p = ut_p.reshape(batch, seq // WIDE_ROW_TILE, 8, d)[:, -1, 8 - (kw - 1):, :]
    conv_s = ut_s.reshape(nseq, t, d)[:, t - (kw - 1):, :]
    return (
        y_p.reshape(batch, seq, d),
        y_s.reshape(nseq, t, d),
        k_p.reshape(1, batch, seq, heads, head_dim),
        v_p.reshape(1, batch, seq, heads, head_dim),
        k_s.reshape(1, nseq, t, heads, head_dim),
        v_s.reshape(1, nseq, t, heads, head_dim),
        conv_p[None],
        conv_s[None],
    )
```

```python
import functools

import numpy as np
import jax
import jax.numpy as jnp
from jax import lax
from jax.experimental import pallas as pl
from jax.experimental.pallas import tpu as pltpu

F32 = jnp.float32
BF16 = jnp.bfloat16

HEAD_DIM = 128
MOBA_BLOCK = 256
MOBA_TOP_K = 3
ROPE_DIM = HEAD_DIM // 4
ROPE_THETA = 500000.0
RMS_EPS = 1e-6
SOFTMAX_SCALE = HEAD_DIM ** -0.5
MASKED = -1e30

LANES = 128
VMEM_LIMIT = 56 * 1024 * 1024
ROW_TILE = 256
WIDE_ROW_TILE = 512
CONV_SUB_ROWS = 256
ADA_COL_TILE = 1536
SCAN_SLOTS = 3
GATHER_SLOTS = 2
ATTN_HEADS_PER_STEP = 2
ATTN_CHUNK_BLOCKS = 2

NT_DIMS = (((1,), (1,)), ((), ()))


def _nt_dot(a, b):
    return lax.dot_general(a, b, NT_DIMS, preferred_element_type=F32)


def _split_bf16(x):
    hi = x.astype(BF16)
    lo = (x - hi.astype(F32)).astype(BF16)
    return hi, lo


def _const_spec(shape):
    n = len(shape)
    return pl.BlockSpec(shape, lambda *_: (0,) * n, pipeline_mode=pl.Buffered(1))


def _layer_spec(shape, layer):
    n = len(shape) - 1
    return pl.BlockSpec((None,) + tuple(shape[1:]), lambda *_: (layer,) + (0,) * n, pipeline_mode=pl.Buffered(1))


def _params(*sem):
    return pltpu.CompilerParams(dimension_semantics=sem, vmem_limit_bytes=VMEM_LIMIT)


def _ada_kernel(c_ref, w_ref, b_ref, o_ref):
    rows = c_ref.shape[0]
    c_hi, c_lo = _split_bf16(c_ref[...])
    w_hi, w_lo = _split_bf16(w_ref[0])
    both = jnp.dot(jnp.concatenate([c_hi, c_lo], axis=0), w_hi, preferred_element_type=F32)
    acc = both[0:rows] + both[rows:] + jnp.dot(c_hi, w_lo, preferred_element_type=F32)
    o_ref[0] = acc + b_ref[0]


def _ada(c_all, w_ada, b_ada):
    depth, d, n6 = w_ada.shape
    rows = c_all.shape[0]
    tn = ADA_COL_TILE
    return pl.pallas_call(
        _ada_kernel,
        grid=(depth, n6 // tn),
        in_specs=[
            pl.BlockSpec((rows, d), lambda l, n: (0, 0)),
            pl.BlockSpec((1, d, tn), lambda l, n: (l, 0, n)),
            pl.BlockSpec((1, 1, tn), lambda l, n: (l, 0, n)),
        ],
        out_specs=pl.BlockSpec((1, rows, tn), lambda l, n: (l, 0, n)),
        out_shape=jax.ShapeDtypeStruct((depth, rows, n6), F32),
        compiler_params=_params("arbitrary", "arbitrary"),
        name="ada",
    )(c_all, w_ada, b_ada.reshape(depth, 1, n6))


def _norm_mod(x, g, shift, scale):
    ms = jnp.mean(x * x, axis=-1, keepdims=True)
    y = x * lax.rsqrt(ms + RMS_EPS) * g
    return y * (1.0 + scale) + shift


def _mlp(h, wup_ref, wdown_ref, after_chunk=None):
    return _mlp_tiles([h], wup_ref, wdown_ref, after_chunk)[0]


def _mlp_tiles(hs, wup_ref, wdown_ref, after_chunk=None):
    d, dff = wup_ref.shape
    hb = [h.astype(BF16) for h in hs]
    acc = [jnp.zeros((h.shape[0], d), F32) for h in hs]
    for i, c in enumerate(range(0, dff, d)):
        for n in range(len(hs)):
            u = jnp.dot(hb[n], wup_ref[:, c:c + d], preferred_element_type=F32)
            u = jnp.square(jnp.maximum(u, 0.0)).astype(BF16)
            acc[n] += jnp.dot(u, wdown_ref[c:c + d, :], preferred_element_type=F32)
        if after_chunk is not None:
            after_chunk(i)
    return acc


def _rope_tables(pos):
    half = ROPE_DIM // 2
    inv = np.float32(ROPE_THETA) ** (-np.arange(0, ROPE_DIM, 2, dtype=np.float32) / np.float32(ROPE_DIM))
    ang = (pos.astype(np.float32)[:, None] * inv.astype(np.float32)[None, :]).astype(np.float32)
    cos = np.cos(ang.astype(np.float64))
    sin = np.sin(ang.astype(np.float64))
    n = pos.shape[0]
    c = np.ones((n, HEAD_DIM), np.float64)
    s1 = np.zeros((n, HEAD_DIM), np.float64)
    s2 = np.zeros((n, HEAD_DIM), np.float64)
    c[:, :half] = cos
    c[:, half:ROPE_DIM] = cos
    s1[:, :half] = -sin
    s2[:, half:ROPE_DIM] = sin
    return tuple(jnp.asarray(t.astype(np.float32)) for t in (c, s1, s2))


def _qkv_kernel(x_ref, mod_ref, g_ref, w_ref, cos_ref, s1_ref, s2_ref, q_ref, k5_ref, v5_ref, kb_ref, vb_ref,
                *ks_ref, attn_dim):
    d = x_ref.shape[1]
    m = mod_ref[0]
    h = _norm_mod(x_ref[...], g_ref[...], m[:, 0:d], m[:, d:2 * d]).astype(BF16)
    cos, s1, s2 = cos_ref[...], s1_ref[...], s2_ref[...]
    tm = h.shape[0]
    heads = attn_dim // HEAD_DIM
    cw = 2 * HEAD_DIM
    for c in range(0, 3 * attn_dim, cw):
        r = jnp.dot(h, w_ref[:, c:c + cw], preferred_element_type=F32)
        if c < 2 * attn_dim:
            rot = []
            for u in range(0, cw, HEAD_DIM):
                z = r[:, u:u + HEAD_DIM]
                z = (z * cos + pltpu.roll(z, HEAD_DIM - ROPE_DIM // 2, 1) * s1
                     + pltpu.roll(z, ROPE_DIM // 2, 1) * s2)
                rot.append(z)
            r = jnp.concatenate(rot, axis=1)
        if c < attn_dim:
            q_ref[:, c:c + cw] = r.astype(q_ref.dtype)
            continue
        is_k = c < 2 * attn_dim
        c0 = c - (attn_dim if is_k else 2 * attn_dim)
        (kb_ref if is_k else vb_ref)[:, c0:c0 + cw] = r.astype(BF16)
        for u in range(0, cw, HEAD_DIM):
            head = (c0 + u) // HEAD_DIM
            (k5_ref if is_k else v5_ref)[pl.ds(head, tm, stride=heads), :] = r[:, u:u + HEAD_DIM]
        if is_k and ks_ref:
            nblk = tm // MOBA_BLOCK
            ks_ref[0][0, :, c0:c0 + cw] = jnp.sum(r.reshape(nblk, MOBA_BLOCK, cw), axis=1)


def _qkv(x, mod, g, w, tables, *, tm, tiles_per_seq, with_ksum, q_dtype):
    rows, d = x.shape
    attn_dim = w.shape[1] // 3
    heads = attn_dim // HEAD_DIM
    n_tiles = rows // tm
    mod_rows = mod.shape[1]
    row_spec = pl.BlockSpec((tm, attn_dim), lambda i: (i, 0))
    tok_head_spec = pl.BlockSpec((tm * heads, HEAD_DIM), lambda i: (i, 0))
    tab_spec = pl.BlockSpec((tm, HEAD_DIM), lambda i: (i % tiles_per_seq, 0))
    out_specs = [row_spec, tok_head_spec, tok_head_spec, row_spec, row_spec]
    out_shape = ([jax.ShapeDtypeStruct((rows, attn_dim), q_dtype)]
                 + [jax.ShapeDtypeStruct((rows * heads, HEAD_DIM), F32)] * 2
                 + [jax.ShapeDtypeStruct((rows, attn_dim), BF16)] * 2)
    if with_ksum:
        nblk = tm // MOBA_BLOCK
        out_specs.append(pl.BlockSpec((1, nblk, attn_dim), lambda i: (i, 0, 0)))
        out_shape.append(jax.ShapeDtypeStruct((n_tiles, nblk, attn_dim), F32))
    return pl.pallas_call(
        functools.partial(_qkv_kernel, attn_dim=attn_dim),
        grid=(n_tiles,),
        in_specs=[
            pl.BlockSpec((tm, d), lambda i: (i, 0)),
            pl.BlockSpec((1, mod_rows, mod.shape[2]), lambda i: (i // tiles_per_seq, 0, 0)),
            _const_spec((1, d)),
            _const_spec(w.shape),
            tab_spec, tab_spec, tab_spec,
        ],
        out_specs=out_specs,
        out_shape=out_shape,
        compiler_params=_params("arbitrary"),
        name="qkv",
    )(x, mod, g, w, *tables)


def _attn_prompt_step(first, i, q_ref, k_ref, v_ref, ks_ref, o_ref, kaug, vaug, eye, s_sc,
                      *, hps, chunk_blocks, side_work):
    tq = q_ref.shape[0]
    seq = k_ref.shape[0]
    nb = seq // MOBA_BLOCK
    ck = chunk_blocks * MOBA_BLOCK
    n_chunk_max = nb // chunk_blocks

    @pl.when(first)
    def _():
        r_id = lax.broadcasted_iota(jnp.int32, (tq, tq), 0)
        c_id = lax.broadcasted_iota(jnp.int32, (tq, tq), 1)
        eye[...] = jnp.where(r_id == c_id, 1.0, 0.0).astype(BF16)
        lane = lax.broadcasted_iota(jnp.int32, (MOBA_BLOCK, LANES), 1)
        ones_col = jnp.where(lane == 0, 1.0, 0.0).astype(BF16)
        for hh in range(hps):
            cols = slice(hh * HEAD_DIM, (hh + 1) * HEAD_DIM)

            def fill(j, carry):
                rows = pl.ds(pl.multiple_of(j * MOBA_BLOCK, MOBA_BLOCK), MOBA_BLOCK)
                kaug[hh, rows, 0:HEAD_DIM] = k_ref[rows, cols]
                kaug[hh, rows, HEAD_DIM:] = jnp.where(lane == j, 1.0, 0.0).astype(BF16)
                vaug[hh, rows, 0:HEAD_DIM] = v_ref[rows, cols]
                vaug[hh, rows, HEAD_DIM:] = ones_col
                return carry
            lax.fori_loop(0, nb, fill, 0)

    own = pl.ds(pl.multiple_of(i * MOBA_BLOCK, MOBA_BLOCK), MOBA_BLOCK)
    head_cols = [slice(hh * HEAD_DIM, (hh + 1) * HEAD_DIM) for hh in range(hps)]

    def select(hh):
        q = q_ref[:, head_cols[hh]]
        qs = (q.astype(F32) * (SOFTMAX_SCALE * np.log2(np.e))).astype(BF16)
        kmean = (ks_ref[0][:, head_cols[hh]] * (1.0 / MOBA_BLOCK)).astype(BF16)
        gate = _nt_dot(kmean, q)
        blk = lax.broadcasted_iota(jnp.int32, (nb, tq), 0)
        blk_f = blk.astype(F32)
        gate = jnp.where(blk < i, gate, -jnp.inf)
        sel = jnp.zeros((nb, tq), F32)
        for _ in range(MOBA_TOP_K):
            mx = jnp.max(gate, axis=0, keepdims=True)
            first = jnp.min(jnp.where(gate == mx, blk_f, float(nb)), axis=0, keepdims=True)
            hit = blk_f == first
            sel = jnp.where(hit & (mx > -jnp.inf), 1.0, sel)
            gate = jnp.where(hit, -jnp.inf, gate)
        sel_pad = jnp.concatenate([sel, jnp.zeros((LANES - nb, tq), F32)], axis=0).astype(BF16)
        sel_q = _nt_dot(eye[...], sel_pad)
        bias = jnp.where(sel_q > 0.5, 0.0, MASKED).astype(BF16)
        return jnp.concatenate([qs, bias], axis=1)

    def variant(n_chunks):
        heads_ = range(hps)
        qa = [select(hh) for hh in heads_]
        r_id = lax.broadcasted_iota(jnp.int32, (tq, MOBA_BLOCK), 0)
        c_id = lax.broadcasted_iota(jnp.int32, (tq, MOBA_BLOCK), 1)
        s_own = [jnp.where(c_id <= r_id, _nt_dot(qa[hh][:, 0:HEAD_DIM], kaug[hh, own, 0:HEAD_DIM]), MASKED)
                 for hh in heads_]
        run = [jnp.maximum(s[:, 0:LANES], s[:, LANES:]) for s in s_own]
        for c in range(n_chunks):
            for hh in heads_:
                s_c = _nt_dot(qa[hh], kaug[hh, c * ck:(c + 1) * ck, :])
                s_sc[hh, :, c * ck:(c + 1) * ck] = s_c
                for u in range(0, ck, LANES):
                    run[hh] = jnp.maximum(run[hh], s_c[:, u:u + LANES])
        m = [jnp.max(r, axis=1, keepdims=True) for r in run]
        acc = [jnp.dot(jnp.exp2(s_own[hh] - m[hh]).astype(BF16), vaug[hh, own, :], preferred_element_type=F32)
               for hh in heads_]
        for c in range(n_chunks):
            for hh in heads_:
                p_c = jnp.exp2(s_sc[hh, :, c * ck:(c + 1) * ck] - m[hh]).astype(BF16)
                acc[hh] += jnp.dot(p_c, vaug[hh, c * ck:(c + 1) * ck, :], preferred_element_type=F32)
        for hh in heads_:
            o_ref[:, head_cols[hh]] = (acc[hh][:, 0:HEAD_DIM] / acc[hh][:, HEAD_DIM:HEAD_DIM + 1]).astype(o_ref.dtype)

    n_needed = (i + chunk_blocks - 1) // chunk_blocks
    for n in range(n_chunk_max + 1):
        @pl.when(n_needed == n)
        def _(n=n):
            side_work()
            variant(n)


def _scan_queries(q_ref):
    t = q_ref.shape[1]
    qs = q_ref[0] * SOFTMAX_SCALE
    q_hi = qs.astype(BF16).astype(F32)
    q_lo = (qs - q_hi).astype(BF16).astype(F32)
    return jnp.concatenate([q_hi, q_lo, jnp.zeros((16 - 2 * t, qs.shape[1]), F32)], axis=0).astype(BF16)


def _cache_scan_pages(p, page_refs, ksum_sc, *, heads, page):
    pps = len(page_refs)
    ppb = MOBA_BLOCK // page
    bps = pps // ppb
    sums = [jnp.sum(ref[...].reshape(page, heads, HEAD_DIM), axis=0) for ref in page_refs]
    for b in range(bps):
        blk_sum = functools.reduce(lambda x, y: x + y, sums[b * ppb:(b + 1) * ppb])
        ksum_sc[pl.ds(pl.multiple_of((p * bps + b) * heads, heads), heads), :] = blk_sum


def _cache_scan_gate(q_ref, ti_ref, ksum_sc, *, heads):
    t = q_ref.shape[1]
    q16 = _scan_queries(q_ref)
    nb = ksum_sc.shape[0] // heads
    lane = lax.broadcasted_iota(jnp.int32, (16, LANES), 1)
    lane_f = lane.astype(F32)
    for h in range(heads):
        kmean = ksum_sc[pl.ds(h, nb, stride=heads), :] * (1.0 / MOBA_BLOCK)
        km = jnp.concatenate([kmean, jnp.zeros((LANES - nb, HEAD_DIM), F32)], axis=0)
        km_hi, km_lo = _split_bf16(km)
        q_h = q16[:, h * HEAD_DIM:(h + 1) * HEAD_DIM]
        r1 = _nt_dot(q_h, km_hi)
        r2 = _nt_dot(q_h, km_lo)
        gate = r1 + pltpu.roll(r1, 16 - t, 0) + r2
        gate = jnp.where(lane < nb, gate, -jnp.inf)
        out = jnp.zeros((16, LANES), jnp.int32)
        for kk in range(MOBA_TOP_K):
            mx = jnp.max(gate, axis=1, keepdims=True)
            first = jnp.min(jnp.where(gate == mx, lane_f, float(LANES)), axis=1, keepdims=True)
            idx = jnp.where(mx > -jnp.inf, first, -1.0).astype(jnp.int32)
            out = jnp.where(lane == kk, idx, out)
            gate = jnp.where(lane_f == first, -jnp.inf, gate)
        ti_ref[0, h] = out[0:8]


def _query_block_order(i, n):
    return jnp.where(i % 2 == 0, i // 2, n - 1 - i // 2)


def _attn_scan_kernel(pt_ref, q_ref, k_ref, v_ref, ks_ref, qs_ref, cache_hbm, o_ref, ti_ref,
                      kaug, vaug, eye, s_sc, ksum_sc, pbuf, psem, *, hps, heads, page, scan_steps):
    n_slots, pps = pbuf.shape[0], pbuf.shape[1]
    page_rows = page * heads
    i = pl.program_id(2)
    flat = (pl.program_id(0) * pl.num_programs(1) + pl.program_id(1)) * pl.num_programs(2) + i
    total = pl.num_programs(0) * pl.num_programs(1) * pl.num_programs(2)
    p = flat % scan_steps
    slot = flat % n_slots

    def page_copies(step):
        seq_id, first = step // scan_steps, (step % scan_steps) * pps
        return [pltpu.make_async_copy(
            cache_hbm.at[pl.ds(pl.multiple_of(pt_ref[seq_id, first + u] * page_rows, page_rows), page_rows), :],
            pbuf.at[step % n_slots, u], psem.at[step % n_slots]) for u in range(pps)]

    def start_pages(step):
        for u, cp in enumerate(page_copies(step)):
            cp.start(priority=u % 2)

    @pl.when(flat == 0)
    def _():
        for ahead in range(n_slots - 1):
            start_pages(ahead)

    @pl.when(flat + n_slots - 1 < total)
    def _():
        start_pages(flat + n_slots - 1)

    for cp in page_copies(flat):
        cp.wait()
    page_refs = [pbuf.at[slot, u] for u in range(pps)]

    def scan_pages():
        _cache_scan_pages(p, page_refs, ksum_sc, heads=heads, page=page)

    _attn_prompt_step(i == 0, _query_block_order(i, pl.num_programs(2)), q_ref, k_ref, v_ref, ks_ref, o_ref,
                      kaug, vaug, eye, s_sc, hps=hps, chunk_blocks=ATTN_CHUNK_BLOCKS, side_work=scan_pages)

    @pl.when(p == scan_steps - 1)
    def _():
        _cache_scan_gate(qs_ref, ti_ref, ksum_sc, heads=heads)


def _attn_scan(page_table, q, k, v, ksum, q_s, cache, *, batch, seq, page):
    rows, attn_dim = q.shape
    heads = attn_dim // HEAD_DIM
    tq = MOBA_BLOCK
    nq = seq // tq
    nb = seq // MOBA_BLOCK
    hps = ATTN_HEADS_PER_STEP
    n_hp = heads // hps
    assert nb <= LANES and nb % ATTN_CHUNK_BLOCKS == 0 and heads % hps == 0

    nseq, n_pages = page_table.shape
    t = q_s.shape[1]
    past = n_pages * page
    nb_past = past // MOBA_BLOCK
    n_steps = batch * n_hp * nq
    assert n_steps % nseq == 0
    scan_steps = n_steps // nseq
    assert n_pages % scan_steps == 0
    pps = n_pages // scan_steps
    assert (pps * page) % MOBA_BLOCK == 0 and MOBA_BLOCK % page == 0
    assert nb_past <= LANES and 2 * t <= 8 and (pps * page // MOBA_BLOCK) % 8 == 0

    def flat(b, h, i):
        return (b * n_hp + h) * nq + i

    q_spec = pl.BlockSpec((tq, hps * HEAD_DIM), lambda b, h, i, pt: (b * nq + _query_block_order(i, nq), h))
    kv_spec = pl.BlockSpec((seq, hps * HEAD_DIM), lambda b, h, i, pt: (b, h))
    assert n_steps >= SCAN_SLOTS
    grid_spec = pltpu.PrefetchScalarGridSpec(
        num_scalar_prefetch=1,
        grid=(batch, n_hp, nq),
        in_specs=[q_spec, kv_spec, kv_spec,
                  pl.BlockSpec((1, nb, hps * HEAD_DIM), lambda b, h, i, pt: (b, 0, h)),
                  pl.BlockSpec((1, t, attn_dim), lambda b, h, i, pt: (flat(b, h, i) // scan_steps, 0, 0)),
                  pl.BlockSpec(memory_space=pl.ANY)],
        out_specs=[
            q_spec,
            pl.BlockSpec((1, heads, 8, LANES), lambda b, h, i, pt: (flat(b, h, i) // scan_steps, 0, 0, 0)),
        ],
        scratch_shapes=[
            pltpu.VMEM((hps, seq, 2 * HEAD_DIM), BF16),
            pltpu.VMEM((hps, seq, 2 * HEAD_DIM), BF16),
            pltpu.VMEM((tq, tq), BF16),
            pltpu.VMEM((hps, tq, seq), F32),
            pltpu.VMEM((nb_past * heads, HEAD_DIM), F32),
            pltpu.VMEM((SCAN_SLOTS, pps, page * heads, HEAD_DIM), F32),
            pltpu.SemaphoreType.DMA((SCAN_SLOTS,)),
        ],
    )
    return pl.pallas_call(
        functools.partial(_attn_scan_kernel, hps=hps, heads=heads, page=page, scan_steps=scan_steps),
        grid_spec=grid_spec,
        out_shape=[
            jax.ShapeDtypeStruct((rows, attn_dim), BF16),
            jax.ShapeDtypeStruct((nseq, heads, 8, LANES), jnp.int32),
        ],
        compiler_params=_params("arbitrary", "arbitrary", "arbitrary"),
        name="attn_scan",
    )(page_table, q, k, v, ksum.reshape(batch, nb, attn_dim), q_s, cache)


def _gather_copies(pt_ref, ti_ref, ck_hbm, cv_hbm, kbuf, vbuf, sem, step, slot, k, *, units, heads, nsel,
                   wait_only=False):
    page = cv_hbm.shape[1]
    ppb = MOBA_BLOCK // page
    unit = step * units + k
    s_, h_ = unit // heads, unit % heads
    out = []
    for c in range(nsel):
        blk = None if wait_only else jnp.maximum(ti_ref[s_, h_ * nsel + c], 0)
        for e in range(ppb):
            pg = 0 if wait_only else pt_ref[s_, blk * ppb + e]
            for src, dst in ((ck_hbm, kbuf), (cv_hbm, vbuf)):
                out.append(pltpu.make_async_copy(
                    src.at[pg, :, 0 if wait_only else h_, :], dst.at[slot, (k * nsel + c) * ppb + e],
                    sem.at[slot, k]))
    return out


def _sample_attend_probs(q, kn, kblk, oks):
    t = q.shape[0]
    nsel = len(oks)
    qs = q * SOFTMAX_SCALE
    width = nsel * MOBA_BLOCK
    seg_w = MOBA_TOP_K * MOBA_BLOCK
    q16 = jnp.concatenate([qs, jnp.zeros((16 - t, HEAD_DIM), F32)], axis=0).astype(BF16)
    lt = _nt_dot(q16, kblk.astype(BF16))[0:t]
    segs = [jnp.where(oks[c], lt[:, c * MOBA_BLOCK:(c + 1) * MOBA_BLOCK], MASKED) for c in range(nsel)]
    lm = jnp.concatenate(segs, axis=1)
    col = lax.broadcasted_iota(jnp.int32, (t, width), 1)
    row = lax.broadcasted_iota(jnp.int32, (t, width), 0)
    lm = jnp.where((col >= row * seg_w) & (col < (row + 1) * seg_w), lm, MASKED)
    row_o = lax.broadcasted_iota(jnp.int32, (t, LANES), 0)
    col_o = lax.broadcasted_iota(jnp.int32, (t, LANES), 1)
    own = jnp.full((t, LANES), MASKED, F32)
    for j in range(t):
        dj = jnp.sum(qs * kn[j:j + 1, :], axis=1, keepdims=True)
        own = jnp.where((col_o == j) & (j <= row_o), dj, own)
    full = jnp.concatenate([lm, own], axis=1)
    m = jnp.max(full, axis=1, keepdims=True)
    p = jnp.exp(full - m)
    return p, jnp.sum(p, axis=1, keepdims=True)


def _sample_attend_values(p, l, vn, vblk):
    t = vn.shape[0]
    width = vblk.shape[0]
    p_pad = jnp.concatenate([p[:, :width], jnp.zeros((8 - t, width), F32)], axis=0).astype(BF16)
    pv = jnp.dot(p_pad, vblk.astype(BF16), preferred_element_type=F32)[0:t]
    for j in range(t):
        pv += p[:, width + j:width + j + 1] * vn[j:j + 1, :]
    return pv / l


def _post_attn_kernel(*refs, gather):
    if gather:
        pt_ref, ti_ref, *refs = refs
    x_ref, o_ref, mod_ref, g_ref, wo_ref, wup_ref, wdown_ref, *refs = refs
    if gather:
        qs_ref, kn_ref, vn_ref, ck_hbm, cv_hbm, y_ref, os_ref, kbuf, vbuf, sem = refs
        units, heads = gather["units"], gather["heads"]
        t = qs_ref.shape[1]
        nsel = t * MOBA_TOP_K
        assert kbuf.shape[0] == 2
        j = pl.program_id(0)
        last = pl.num_programs(0) - 1
        slot = j % 2
        copies = functools.partial(_gather_copies, pt_ref, ti_ref, ck_hbm, cv_hbm, kbuf, vbuf, sem,
                                   units=units, heads=heads, nsel=nsel)

        @pl.when(j == 0)
        def _():
            for k in range(units):
                for cp in copies(0, 0, k):
                    cp.start()

        rows_per_unit = kbuf.shape[1] // units
        probs = {}
        ahead = [copies(jnp.minimum(j + 1, last), 1 - slot, k) for k in range(units)]
        landed = [copies(j, slot, k, wait_only=True) for k in range(units)]
        valid = [[ti_ref[(j * units + k) // heads, ((j * units + k) % heads) * nsel + c] >= 0
                  for c in range(nsel)] for k in range(units)]

        def begin(k):
            for cp in ahead[k]:
                cp.start()
            for cp in landed[k]:
                cp.wait()
            cols = slice(k * HEAD_DIM, (k + 1) * HEAD_DIM)
            kblk = kbuf[slot, k * rows_per_unit:(k + 1) * rows_per_unit].reshape(nsel * MOBA_BLOCK, HEAD_DIM)
            probs[k] = _sample_attend_probs(qs_ref[0][:, cols], kn_ref[0][:, cols], kblk, valid[k])

        def finish(k):
            cols = slice(k * HEAD_DIM, (k + 1) * HEAD_DIM)
            vblk = vbuf[slot, k * rows_per_unit:(k + 1) * rows_per_unit].reshape(nsel * MOBA_BLOCK, HEAD_DIM)
            os_ref[0, :, cols] = _sample_attend_values(*probs.pop(k), vn_ref[0][:, cols], vblk)

        def after_chunk(i):
            if 0 < i <= units:
                finish(i - 1)
            if i + 1 < units:
                begin(i + 1)
            if i == n_chunks - 1:
                for k in sorted(probs):
                    finish(k)
    else:
        (y_ref,) = refs
        units = 0
        begin = after_chunk = None
    d = x_ref.shape[1]
    n_chunks = wup_ref.shape[1] // d
    m = mod_ref[0]
    a = jnp.dot(o_ref[...].astype(BF16), wo_ref[...], preferred_element_type=F32)
    x1 = x_ref[...] + m[:, 2 * d:3 * d] * a
    h = _norm_mod(x1, g_ref[...], m[:, 3 * d:4 * d], m[:, 4 * d:5 * d])
    if units > 0:
        begin(0)
    y_ref[...] = x1 + m[:, 5 * d:6 * d] * _mlp(h, wup_ref, wdown_ref, after_chunk)
    if gather:
        assert not probs and units <= n_chunks, "more sample-attention units than MLP chunks"

        @pl.when(j == last)
        def _():
            for group in ahead:
                for cp in group:
                    cp.wait()


def _post_attn(x, o, mod, g, wo, wup, wdown, *, layer, tm, tiles_per_seq, sample=None):
    rows, d = x.shape
    n_tiles = rows // tm
    row_spec = pl.BlockSpec((tm, d), lambda i, *_: (i, 0))
    in_specs = [
        row_spec,
        pl.BlockSpec((tm, o.shape[1]), lambda i, *_: (i, 0)),
        pl.BlockSpec((1, mod.shape[1], mod.shape[2]), lambda i, *_: (i // tiles_per_seq, 0, 0)),
        _const_spec((1, d)),
        _const_spec(wo.shape), _layer_spec(wup.shape, layer), _layer_spec(wdown.shape, layer),
    ]
    args = [x, o, mod, g, wo, wup, wdown]
    if sample is None:
        return pl.pallas_call(
            functools.partial(_post_attn_kernel, gather=None),
            grid=(n_tiles,),
            in_specs=in_specs,
            out_specs=row_spec,
            out_shape=jax.ShapeDtypeStruct((rows, d), F32),
            compiler_params=_params("arbitrary"),
            name="post_attn",
        )(*args)

    page_table, top_idx, q_s, k_s, v_s, cache_k, cache_v = sample
    nseq, t, attn_dim = q_s.shape
    heads = attn_dim // HEAD_DIM
    page = cache_v.shape[1]
    nsel = t * MOBA_TOP_K
    ppb = MOBA_BLOCK // page
    assert (nseq * heads) % n_tiles == 0
    units = nseq * heads // n_tiles
    assert heads % units == 0 and n_tiles >= GATHER_SLOTS
    groups = heads // units
    new_spec = pl.BlockSpec((1, t, units * HEAD_DIM), lambda i, *_: (i // groups, 0, i % groups))
    gather_buf = pltpu.VMEM((GATHER_SLOTS, units * nsel * ppb, page, HEAD_DIM), F32)
    grid_spec = pltpu.PrefetchScalarGridSpec(
        num_scalar_prefetch=2,
        grid=(n_tiles,),
        in_specs=in_specs + [new_spec, new_spec, new_spec,
                             pl.BlockSpec(memory_space=pl.ANY), pl.BlockSpec(memory_space=pl.ANY)],
        out_specs=[row_spec, new_spec],
        scratch_shapes=[gather_buf, gather_buf, pltpu.SemaphoreType.DMA((GATHER_SLOTS, units))],
    )
    return pl.pallas_call(
        functools.partial(_post_attn_kernel, gather=dict(units=units, heads=heads)),
        grid_spec=grid_spec,
        out_shape=[jax.ShapeDtypeStruct((rows, d), F32), jax.ShapeDtypeStruct((nseq, t, attn_dim), F32)],
        compiler_params=_params("arbitrary"),
        name="post_attn_gather",
    )(page_table, top_idx, *args, q_s, k_s, v_s, cache_k, cache_v)


def _conv_layer_kernel(x_ref, mod_ref, g1_ref, g2_ref, gf_ref, wbcx_ref, wconv_ref, wout_ref, wup_ref, wdown_ref,
                       *rest, tiles_per_seq, period, tail):
    if period is None:
        y_ref, ut_ref, ext = rest
    else:
        e1_ref, e2_ref, y_ref, ut_ref, ext = rest
    i = pl.program_id(0)
    tm, d = x_ref.shape
    halo = 8
    sub = min(tm, CONV_SUB_ROWS)
    subs = [slice(r, r + sub) for r in range(0, tm, sub)]
    m_all = mod_ref[0]
    ms_ = [m_all if m_all.shape[0] == 1 else m_all[s] for s in subs]
    x = [x_ref[s, :] for s in subs]
    h = [_norm_mod(x[n], g1_ref[...], ms_[n][:, 0:d], ms_[n][:, d:2 * d]).astype(BF16) for n in range(len(subs))]
    bcx = [[jnp.dot(hn, wbcx_ref[:, c * d:(c + 1) * d], preferred_element_type=F32) for c in range(3)] for hn in h]
    u = [cg * xin for _, cg, xin in bcx]

    @pl.when(i % tiles_per_seq == 0)
    def _():
        ext[0:halo, :] = jnp.zeros((halo, d), F32)

    w = wconv_ref[...]
    gated = []
    for n, s in enumerate(subs):
        ext[halo + s.start:halo + s.stop, :] = u[n]
        prev1 = ext[halo - 1 + s.start:halo - 1 + s.stop, :]
        prev2 = ext[halo - 2 + s.start:halo - 2 + s.stop, :]
        if period is not None:
            tpos = (lax.broadcasted_iota(jnp.int32, (sub, d), 0) + s.start) % period
            prev1 = jnp.where(tpos < 1, e1_ref[s, :], prev1)
            prev2 = jnp.where(tpos < 2, e2_ref[s, :], prev2)
        conv = w[0:1, :] * prev2 + w[1:2, :] * prev1 + w[2:3, :] * u[n]
        gated.append((bcx[n][0] * conv).astype(BF16))
    ext[0:halo, :] = ext[tm:tm + halo, :]
    ut_ref[0] = ext[halo + tm - tail:halo + tm, :]

    y = [jnp.dot(gn, wout_ref[...], preferred_element_type=F32) for gn in gated]
    x1 = [x[n] + ms_[n][:, 2 * d:3 * d] * y[n] for n in range(len(subs))]
    h2 = [_norm_mod(x1[n], g2_ref[...], ms_[n][:, 3 * d:4 * d], ms_[n][:, 4 * d:5 * d]) for n in range(len(subs))]
    mlp = _mlp_tiles(h2, wup_ref, wdown_ref)
    for n, s in enumerate(subs):
        x2 = x1[n] + ms_[n][:, 5 * d:6 * d] * mlp[n]
        ms2 = jnp.mean(x2 * x2, axis=-1, keepdims=True)
        y_ref[s, :] = x2 * lax.rsqrt(ms2 + RMS_EPS) * gf_ref[...]


def _conv_layer(x, mod, g1, g2, gf, wbcx, wconv, wout, wup, wdown, state_rows,
                *, layer, tm, tiles_per_seq, period, tail):
    rows, d = x.shape
    n_tiles = rows // tm
    row_spec = pl.BlockSpec((tm, d), lambda i: (i, 0))
    in_specs = [
        row_spec,
        pl.BlockSpec((1, mod.shape[1], mod.shape[2]), lambda i: (i // tiles_per_seq, 0, 0)),
        _const_spec((1, d)), _const_spec((1, d)), _const_spec((1, d)),
        _const_spec(wbcx.shape), _const_spec(wconv.shape), _const_spec(wout.shape),
        _layer_spec(wup.shape, layer), _layer_spec(wdown.shape, layer),
    ]
    args = [x, mod, g1, g2, gf, wbcx, wconv, wout, wup, wdown]
    if period is not None:
        in_specs += [row_spec, row_spec]
        args += list(state_rows)
    return pl.pallas_call(
        functools.partial(_conv_layer_kernel, tiles_per_seq=tiles_per_seq, period=period, tail=tail),
        grid=(n_tiles,),
        in_specs=in_specs,
        out_specs=[row_spec, pl.BlockSpec((1, tail, d), lambda i: (i, 0, 0))],
        out_shape=[jax.ShapeDtypeStruct((rows, d), F32), jax.ShapeDtypeStruct((n_tiles, tail, d), F32)],
        scratch_shapes=[pltpu.VMEM((tm + 8, d), F32)],
        compiler_params=_params("arbitrary"),
        name="conv_layer",
    )(*args)


def kernel(x_prompt, x_sample, cache_k, cache_v, state_conv, page_table, c_prompt, c_sample, w_ada, b_ada, g_norm,
           w_qkv, w_o, w_bcx, w_conv, w_out, w_up, w_down, g_final):
    batch, seq, d = x_prompt.shape
    nseq, t, _ = x_sample.shape
    depth = w_ada.shape[0]
    n_layers_attn, pool, page, heads, head_dim = cache_k.shape
    attn_dim = heads * head_dim
    n_pages = page_table.shape[1]
    past = n_pages * page
    kw = state_conv.shape[2] + 1
    assert depth == 2 and n_layers_attn == 1 and state_conv.shape[0] == 1, "layer 0 attention, layer 1 convolution"
    assert head_dim == HEAD_DIM and kw == 3 and t >= kw - 1
    assert seq % ROW_TILE == 0 and ROW_TILE % MOBA_BLOCK == 0 and past % MOBA_BLOCK == 0 and t <= MOBA_BLOCK
    assert seq % WIDE_ROW_TILE == 0 and WIDE_ROW_TILE % CONV_SUB_ROWS == 0 and WIDE_ROW_TILE % MOBA_BLOCK == 0
    rows_p, rows_s = batch * seq, nseq * t
    assert rows_s % 8 == 0

    pad = -(rows_s + batch) % 16
    c_all = jnp.concatenate([jnp.repeat(c_sample, t, axis=0), c_prompt, jnp.zeros((pad, d), F32)], axis=0)
    mod = _ada(c_all, w_ada, b_ada)
    mod_s = mod[:, None, :rows_s]
    mod_p = mod[:, rows_s:rows_s + batch, None, :]

    bf = lambda a: a.astype(BF16)
    wqkv, wo, wbcx, wout = bf(w_qkv[0]), bf(w_o[0]), bf(w_bcx[0]), bf(w_out[0])
    wup, wdown = bf(w_up), bf(w_down)
    g = g_norm.reshape(depth, 2, 1, d)
    gf = g_final.reshape(1, d)

    xp = x_prompt.reshape(rows_p, d)
    xs = x_sample.reshape(rows_s, d)
    tps = seq // ROW_TILE
    tab_p = _rope_tables(np.arange(seq))
    tab_s = _rope_tables(past + (np.arange(rows_s) % t))

    q_p, k_p, v_p, kb_p, vb_p, ksum = _qkv(xp, mod_p[0], g[0, 0], wqkv, tab_p, tm=WIDE_ROW_TILE,
                                           tiles_per_seq=seq // WIDE_ROW_TILE, with_ksum=True, q_dtype=BF16)
    q_s, k_s, v_s, kb_s, vb_s = _qkv(xs, mod_s[0], g[0, 0], wqkv, tab_s, tm=rows_s, tiles_per_seq=1,
                                     with_ksum=False, q_dtype=F32)
    q_s3, kn_s3, vn_s3 = (a.astype(F32).reshape(nseq, t, attn_dim) for a in (q_s, kb_s, vb_s))
    o_p, top = _attn_scan(page_table, q_p, kb_p, vb_p, ksum, q_s3,
                          cache_k.reshape(pool * page * heads, head_dim), batch=batch, seq=seq, page=page)
    top_idx = top[:, :, :t, :MOBA_TOP_K].reshape(nseq, heads * t * MOBA_TOP_K)
    sample = (page_table, top_idx, q_s3, kn_s3, vn_s3, cache_k.reshape(pool, page, heads, head_dim),
              cache_v.reshape(pool, page, heads, head_dim))
    x1_p, o_s = _post_attn(xp, o_p, mod_p[0], g[0, 1], wo, wup, wdown, layer=0, tm=ROW_TILE, tiles_per_seq=tps,
                           sample=sample)
    x1_s = _post_attn(xs, o_s.reshape(rows_s, attn_dim), mod_s[0], g[0, 1], wo, wup, wdown, layer=0,
                      tm=rows_s, tiles_per_seq=1)

    y_p, ut_p = _conv_layer(x1_p, mod_p[1], g[1, 0], g[1, 1], gf, wbcx, w_conv[0], wout, wup, wdown, None, layer=1,
                            tm=WIDE_ROW_TILE, tiles_per_seq=seq // WIDE_ROW_TILE, period=None, tail=8)
    st = state_conv[0]
    zeros = jnp.zeros((nseq, t - 1, d), F32)
    e1 = jnp.concatenate([st[:, 1:2], zeros], axis=1).reshape(rows_s, d)
    e2 = jnp.concatenate([st[:, 0:2], zeros[:, 1:]], axis=1).reshape(rows_s, d)
    y_s, ut_s = _conv_layer(x1_s, mod_s[1], g[1, 0], g[1, 1], gf, wbcx, w_conv[0], wout, wup, wdown, (e1, e2),
                            layer=1, tm=rows_s, tiles_per_seq=1, period=t, tail=rows_s)

    conv_p = ut_p.reshape(batch, seq // WIDE_ROW_TILE, 8, d)[:, -1, 8 - (kw - 1):, :]
    conv_s = ut_s.reshape(nseq, t, d)[:, t - (kw - 1):, :]
    return (
        y_p.reshape(batch, seq, d),
        y_s.reshape(nseq, t, d),
        k_p.reshape(1, batch, seq, heads, head_dim),
        v_p.reshape(1, batch, seq, heads, head_dim),
        k_s.reshape(1, nseq, t, heads, head_dim),
        v_s.reshape(1, nseq, t, heads, head_dim),
        conv_p[None],
        conv_s[None],
    )
```

```python
import functools

import numpy as np
import jax
import jax.numpy as jnp
from jax import lax
from jax.experimental import pallas as pl
from jax.experimental.pallas import tpu as pltpu

F32 = jnp.float32
BF16 = jnp.bfloat16

HEAD_DIM = 128
MOBA_BLOCK = 256
MOBA_TOP_K = 3
ROPE_DIM = HEAD_DIM // 4
ROPE_THETA = 500000.0
RMS_EPS = 1e-6
SOFTMAX_SCALE = HEAD_DIM ** -0.5
MASKED = -1e30

LANES = 128
VMEM_LIMIT = 56 * 1024 * 1024
ROW_TILE = 256
WIDE_ROW_TILE = 512
CONV_SUB_ROWS = 256
ADA_COL_TILE = 1536
SCAN_SLOTS = 3
GATHER_SLOTS = 2
ATTN_HEADS_PER_STEP = 2
ATTN_CHUNK_BLOCKS = 1

NT_DIMS = (((1,), (1,)), ((), ()))


def _nt_dot(a, b):
    return lax.dot_general(a, b, NT_DIMS, preferred_element_type=F32)


def _split_bf16(x):
    hi = x.astype(BF16)
    lo = (x - hi.astype(F32)).astype(BF16)
    return hi, lo


def _const_spec(shape):
    n = len(shape)
    return pl.BlockSpec(shape, lambda *_: (0,) * n, pipeline_mode=pl.Buffered(1))


def _layer_spec(shape, layer):
    n = len(shape) - 1
    return pl.BlockSpec((None,) + tuple(shape[1:]), lambda *_: (layer,) + (0,) * n, pipeline_mode=pl.Buffered(1))


def _params(*sem):
    return pltpu.CompilerParams(dimension_semantics=sem, vmem_limit_bytes=VMEM_LIMIT)


def _ada_kernel(c_ref, w_ref, b_ref, o_ref):
    rows = c_ref.shape[0]
    c_hi, c_lo = _split_bf16(c_ref[...])
    w_hi, w_lo = _split_bf16(w_ref[0])
    both = jnp.dot(jnp.concatenate([c_hi, c_lo], axis=0), w_hi, preferred_element_type=F32)
    acc = both[0:rows] + both[rows:] + jnp.dot(c_hi, w_lo, preferred_element_type=F32)
    o_ref[0] = acc + b_ref[0]


def _ada(c_all, w_ada, b_ada):
    depth, d, n6 = w_ada.shape
    rows = c_all.shape[0]
    tn = ADA_COL_TILE
    return pl.pallas_call(
        _ada_kernel,
        grid=(depth, n6 // tn),
        in_specs=[
            pl.BlockSpec((rows, d), lambda l, n: (0, 0)),
            pl.BlockSpec((1, d, tn), lambda l, n: (l, 0, n)),
            pl.BlockSpec((1, 1, tn), lambda l, n: (l, 0, n)),
        ],
        out_specs=pl.BlockSpec((1, rows, tn), lambda l, n: (l, 0, n)),
        out_shape=jax.ShapeDtypeStruct((depth, rows, n6), F32),
        compiler_params=_params("arbitrary", "arbitrary"),
        name="ada",
    )(c_all, w_ada, b_ada.reshape(depth, 1, n6))


def _norm_mod(x, g, shift, scale):
    ms = jnp.mean(x * x, axis=-1, keepdims=True)
    y = x * lax.rsqrt(ms + RMS_EPS) * g
    return y * (1.0 + scale) + shift


def _mlp(h, wup_ref, wdown_ref, after_chunk=None):
    return _mlp_tiles([h], wup_ref, wdown_ref, after_chunk)[0]


def _mlp_tiles(hs, wup_ref, wdown_ref, after_chunk=None):
    d, dff = wup_ref.shape
    hb = [h.astype(BF16) for h in hs]
    acc = [jnp.zeros((h.shape[0], d), F32) for h in hs]
    for i, c in enumerate(range(0, dff, d)):
        for n in range(len(hs)):
            u = jnp.dot(hb[n], wup_ref[:, c:c + d], preferred_element_type=F32)
            u = jnp.square(jnp.maximum(u, 0.0)).astype(BF16)
            acc[n] += jnp.dot(u, wdown_ref[c:c + d, :], preferred_element_type=F32)
        if after_chunk is not None:
            after_chunk(i)
    return acc


def _rope_tables(pos):
    half = ROPE_DIM // 2
    inv = np.float32(ROPE_THETA) ** (-np.arange(0, ROPE_DIM, 2, dtype=np.float32) / np.float32(ROPE_DIM))
    ang = (pos.astype(np.float32)[:, None] * inv.astype(np.float32)[None, :]).astype(np.float32)
    cos = np.cos(ang.astype(np.float64))
    sin = np.sin(ang.astype(np.float64))
    n = pos.shape[0]
    c = np.ones((n, HEAD_DIM), np.float64)
    s1 = np.zeros((n, HEAD_DIM), np.float64)
    s2 = np.zeros((n, HEAD_DIM), np.float64)
    c[:, :half] = cos
    c[:, half:ROPE_DIM] = cos
    s1[:, :half] = -sin
    s2[:, half:ROPE_DIM] = sin
    return tuple(jnp.asarray(t.astype(np.float32)) for t in (c, s1, s2))


def _qkv_kernel(x_ref, mod_ref, g_ref, w_ref, cos_ref, s1_ref, s2_ref, q_ref, k5_ref, v5_ref, kb_ref, vb_ref,
                *ks_ref, attn_dim):
    d = x_ref.shape[1]
    m = mod_ref[0]
    h = _norm_mod(x_ref[...], g_ref[...], m[:, 0:d], m[:, d:2 * d]).astype(BF16)
    cos, s1, s2 = cos_ref[...], s1_ref[...], s2_ref[...]
    tm = h.shape[0]
    heads = attn_dim // HEAD_DIM
    cw = 2 * HEAD_DIM
    for c in range(0, 3 * attn_dim, cw):
        r = jnp.dot(h, w_ref[:, c:c + cw], preferred_element_type=F32)
        if c < 2 * attn_dim:
            rot = []
            for u in range(0, cw, HEAD_DIM):
                z = r[:, u:u + HEAD_DIM]
                z = (z * cos + pltpu.roll(z, HEAD_DIM - ROPE_DIM // 2, 1) * s1
                     + pltpu.roll(z, ROPE_DIM // 2, 1) * s2)
                rot.append(z)
            r = jnp.concatenate(rot, axis=1)
        if c < attn_dim:
            q_ref[:, c:c + cw] = r.astype(q_ref.dtype)
            continue
        is_k = c < 2 * attn_dim
        c0 = c - (attn_dim if is_k else 2 * attn_dim)
        (kb_ref if is_k else vb_ref)[:, c0:c0 + cw] = r.astype(BF16)
        for u in range(0, cw, HEAD_DIM):
            head = (c0 + u) // HEAD_DIM
            (k5_ref if is_k else v5_ref)[pl.ds(head, tm, stride=heads), :] = r[:, u:u + HEAD_DIM]
        if is_k and ks_ref:
            nblk = tm // MOBA_BLOCK
            ks_ref[0][0, :, c0:c0 + cw] = jnp.sum(r.reshape(nblk, MOBA_BLOCK, cw), axis=1)


def _qkv(x, mod, g, w, tables, *, tm, tiles_per_seq, with_ksum, q_dtype):
    rows, d = x.shape
    attn_dim = w.shape[1] // 3
    heads = attn_dim // HEAD_DIM
    n_tiles = rows // tm
    mod_rows = mod.shape[1]
    row_spec = pl.BlockSpec((tm, attn_dim), lambda i: (i, 0))
    tok_head_spec = pl.BlockSpec((tm * heads, HEAD_DIM), lambda i: (i, 0))
    tab_spec = pl.BlockSpec((tm, HEAD_DIM), lambda i: (i % tiles_per_seq, 0))
    out_specs = [row_spec, tok_head_spec, tok_head_spec, row_spec, row_spec]
    out_shape = ([jax.ShapeDtypeStruct((rows, attn_dim), q_dtype)]
                 + [jax.ShapeDtypeStruct((rows * heads, HEAD_DIM), F32)] * 2
                 + [jax.ShapeDtypeStruct((rows, attn_dim), BF16)] * 2)
    if with_ksum:
        nblk = tm // MOBA_BLOCK
        out_specs.append(pl.BlockSpec((1, nblk, attn_dim), lambda i: (i, 0, 0)))
        out_shape.append(jax.ShapeDtypeStruct((n_tiles, nblk, attn_dim), F32))
    return pl.pallas_call(
        functools.partial(_qkv_kernel, attn_dim=attn_dim),
        grid=(n_tiles,),
        in_specs=[
            pl.BlockSpec((tm, d), lambda i: (i, 0)),
            pl.BlockSpec((1, mod_rows, mod.shape[2]), lambda i: (i // tiles_per_seq, 0, 0)),
            _const_spec((1, d)),
            _const_spec(w.shape),
            tab_spec, tab_spec, tab_spec,
        ],
        out_specs=out_specs,
        out_shape=out_shape,
        compiler_params=_params("arbitrary"),
        name="qkv",
    )(x, mod, g, w, *tables)


def _attn_prompt_step(first, i, q_ref, k_ref, v_ref, ks_ref, o_ref, kaug, vaug, eye, s_sc,
                      *, hps, chunk_blocks, side_work):
    tq = q_ref.shape[0]
    seq = k_ref.shape[0]
    nb = seq // MOBA_BLOCK
    ck = chunk_blocks * MOBA_BLOCK
    n_chunk_max = nb // chunk_blocks

    @pl.when(first)
    def _():
        r_id = lax.broadcasted_iota(jnp.int32, (tq, tq), 0)
        c_id = lax.broadcasted_iota(jnp.int32, (tq, tq), 1)
        eye[...] = jnp.where(r_id == c_id, 1.0, 0.0).astype(BF16)
        lane = lax.broadcasted_iota(jnp.int32, (MOBA_BLOCK, LANES), 1)
        ones_col = jnp.where(lane == 0, 1.0, 0.0).astype(BF16)
        for hh in range(hps):
            cols = slice(hh * HEAD_DIM, (hh + 1) * HEAD_DIM)

            def fill(j, carry):
                rows = pl.ds(pl.multiple_of(j * MOBA_BLOCK, MOBA_BLOCK), MOBA_BLOCK)
                kaug[hh, rows, 0:HEAD_DIM] = k_ref[rows, cols]
                kaug[hh, rows, HEAD_DIM:] = jnp.where(lane == j, 1.0, 0.0).astype(BF16)
                vaug[hh, rows, 0:HEAD_DIM] = v_ref[rows, cols]
                vaug[hh, rows, HEAD_DIM:] = ones_col
                return carry
            lax.fori_loop(0, nb, fill, 0)

    own = pl.ds(pl.multiple_of(i * MOBA_BLOCK, MOBA_BLOCK), MOBA_BLOCK)
    head_cols = [slice(hh * HEAD_DIM, (hh + 1) * HEAD_DIM) for hh in range(hps)]

    def select(hh):
        q = q_ref[:, head_cols[hh]]
        qs = (q.astype(F32) * (SOFTMAX_SCALE * np.log2(np.e))).astype(BF16)
        kmean = (ks_ref[0][:, head_cols[hh]] * (1.0 / MOBA_BLOCK)).astype(BF16)
        gate = _nt_dot(kmean, q)
        blk = lax.broadcasted_iota(jnp.int32, (nb, tq), 0)
        blk_f = blk.astype(F32)
        gate = jnp.where(blk < i, gate, -jnp.inf)
        sel = jnp.zeros((nb, tq), F32)
        for _ in range(MOBA_TOP_K):
            mx = jnp.max(gate, axis=0, keepdims=True)
            first = jnp.min(jnp.where(gate == mx, blk_f, float(nb)), axis=0, keepdims=True)
            hit = blk_f == first
            sel = jnp.where(hit & (mx > -jnp.inf), 1.0, sel)
            gate = jnp.where(hit, -jnp.inf, gate)
        sel_pad = jnp.concatenate([sel, jnp.zeros((LANES - nb, tq), F32)], axis=0).astype(BF16)
        sel_q = _nt_dot(eye[...], sel_pad)
        bias = jnp.where(sel_q > 0.5, 0.0, MASKED).astype(BF16)
        return jnp.concatenate([qs, bias], axis=1)

    def variant(n_chunks):
        heads_ = range(hps)
        qa = [select(hh) for hh in heads_]
        r_id = lax.broadcasted_iota(jnp.int32, (tq, MOBA_BLOCK), 0)
        c_id = lax.broadcasted_iota(jnp.int32, (tq, MOBA_BLOCK), 1)
        s_own = [jnp.where(c_id <= r_id, _nt_dot(qa[hh][:, 0:HEAD_DIM], kaug[hh, own, 0:HEAD_DIM]), MASKED)
                 for hh in heads_]
        run = [jnp.maximum(s[:, 0:LANES], s[:, LANES:]) for s in s_own]
        for c in range(n_chunks):
            for hh in heads_:
                s_c = _nt_dot(qa[hh], kaug[hh, c * ck:(c + 1) * ck, :])
                s_sc[hh, :, c * ck:(c + 1) * ck] = s_c
                for u in range(0, ck, LANES):
                    run[hh] = jnp.maximum(run[hh], s_c[:, u:u + LANES])
        m = [jnp.max(r, axis=1, keepdims=True) for r in run]
        acc = [jnp.dot(jnp.exp2(s_own[hh] - m[hh]).astype(BF16), vaug[hh, own, :], preferred_element_type=F32)
               for hh in heads_]
        for c in range(n_chunks):
            for hh in heads_:
                p_c = jnp.exp2(s_sc[hh, :, c * ck:(c + 1) * ck] - m[hh]).astype(BF16)
                acc[hh] += jnp.dot(p_c, vaug[hh, c * ck:(c + 1) * ck, :], preferred_element_type=F32)
        for hh in heads_:
            o_ref[:, head_cols[hh]] = (acc[hh][:, 0:HEAD_DIM] / acc[hh][:, HEAD_DIM:HEAD_DIM + 1]).astype(o_ref.dtype)

    n_needed = (i + chunk_blocks - 1) // chunk_blocks
    for n in range(n_chunk_max + 1):
        @pl.when(n_needed == n)
        def _(n=n):
            side_work()
            variant(n)


def _scan_queries(q_ref):
    t = q_ref.shape[1]
    qs = q_ref[0] * SOFTMAX_SCALE
    q_hi = qs.astype(BF16).astype(F32)
    q_lo = (qs - q_hi).astype(BF16).astype(F32)
    return jnp.concatenate([q_hi, q_lo, jnp.zeros((16 - 2 * t, qs.shape[1]), F32)], axis=0).astype(BF16)


def _cache_scan_pages(p, page_refs, ksum_sc, *, heads, page):
    pps = len(page_refs)
    ppb = MOBA_BLOCK // page
    bps = pps // ppb
    sums = [jnp.sum(ref[...].reshape(page, heads, HEAD_DIM), axis=0) for ref in page_refs]
    for b in range(bps):
        blk_sum = functools.reduce(lambda x, y: x + y, sums[b * ppb:(b + 1) * ppb])
        ksum_sc[pl.ds(pl.multiple_of((p * bps + b) * heads, heads), heads), :] = blk_sum


def _cache_scan_gate(q_ref, ti_ref, ksum_sc, *, heads):
    t = q_ref.shape[1]
    q16 = _scan_queries(q_ref)
    nb = ksum_sc.shape[0] // heads
    lane = lax.broadcasted_iota(jnp.int32, (16, LANES), 1)
    lane_f = lane.astype(F32)
    for h in range(heads):
        kmean = ksum_sc[pl.ds(h, nb, stride=heads), :] * (1.0 / MOBA_BLOCK)
        km = jnp.concatenate([kmean, jnp.zeros((LANES - nb, HEAD_DIM), F32)], axis=0)
        km_hi, km_lo = _split_bf16(km)
        q_h = q16[:, h * HEAD_DIM:(h + 1) * HEAD_DIM]
        r1 = _nt_dot(q_h, km_hi)
        r2 = _nt_dot(q_h, km_lo)
        gate = r1 + pltpu.roll(r1, 16 - t, 0) + r2
        gate = jnp.where(lane < nb, gate, -jnp.inf)
        out = jnp.zeros((16, LANES), jnp.int32)
        for kk in range(MOBA_TOP_K):
            mx = jnp.max(gate, axis=1, keepdims=True)
            first = jnp.min(jnp.where(gate == mx, lane_f, float(LANES)), axis=1, keepdims=True)
            idx = jnp.where(mx > -jnp.inf, first, -1.0).astype(jnp.int32)
            out = jnp.where(lane == kk, idx, out)
            gate = jnp.where(lane_f == first, -jnp.inf, gate)
        ti_ref[0, h] = out[0:8]


def _query_block_order(i, n):
    return jnp.where(i % 2 == 0, i // 2, n - 1 - i // 2)


def _attn_scan_kernel(pt_ref, q_ref, k_ref, v_ref, ks_ref, qs_ref, cache_hbm, o_ref, ti_ref,
                      kaug, vaug, eye, s_sc, ksum_sc, pbuf, psem, *, hps, heads, page, scan_steps):
    n_slots, pps = pbuf.shape[0], pbuf.shape[1]
    page_rows = page * heads
    i = pl.program_id(2)
    flat = (pl.program_id(0) * pl.num_programs(1) + pl.program_id(1)) * pl.num_programs(2) + i
    total = pl.num_programs(0) * pl.num_programs(1) * pl.num_programs(2)
    p = flat % scan_steps
    slot = flat % n_slots

    def page_copies(step):
        seq_id, first = step // scan_steps, (step % scan_steps) * pps
        return [pltpu.make_async_copy(
            cache_hbm.at[pl.ds(pl.multiple_of(pt_ref[seq_id, first + u] * page_rows, page_rows), page_rows), :],
            pbuf.at[step % n_slots, u], psem.at[step % n_slots]) for u in range(pps)]

    @pl.when(flat == 0)
    def _():
        for ahead in range(n_slots - 1):
            for cp in page_copies(ahead):
                cp.start()

    @pl.when(flat + n_slots - 1 < total)
    def _():
        for cp in page_copies(flat + n_slots - 1):
            cp.start()

    for cp in page_copies(flat):
        cp.wait()
    page_refs = [pbuf.at[slot, u] for u in range(pps)]

    def scan_pages():
        _cache_scan_pages(p, page_refs, ksum_sc, heads=heads, page=page)

    _attn_prompt_step(i == 0, _query_block_order(i, pl.num_programs(2)), q_ref, k_ref, v_ref, ks_ref, o_ref,
                      kaug, vaug, eye, s_sc, hps=hps, chunk_blocks=ATTN_CHUNK_BLOCKS, side_work=scan_pages)

    @pl.when(p == scan_steps - 1)
    def _():
        _cache_scan_gate(qs_ref, ti_ref, ksum_sc, heads=heads)


def _attn_scan(page_table, q, k, v, ksum, q_s, cache, *, batch, seq, page):
    rows, attn_dim = q.shape
    heads = attn_dim // HEAD_DIM
    tq = MOBA_BLOCK
    nq = seq // tq
    nb = seq // MOBA_BLOCK
    hps = ATTN_HEADS_PER_STEP
    n_hp = heads // hps
    assert nb <= LANES and nb % ATTN_CHUNK_BLOCKS == 0 and heads % hps == 0

    nseq, n_pages = page_table.shape
    t = q_s.shape[1]
    past = n_pages * page
    nb_past = past // MOBA_BLOCK
    n_steps = batch * n_hp * nq
    assert n_steps % nseq == 0
    scan_steps = n_steps // nseq
    assert n_pages % scan_steps == 0
    pps = n_pages // scan_steps
    assert (pps * page) % MOBA_BLOCK == 0 and MOBA_BLOCK % page == 0
    assert nb_past <= LANES and 2 * t <= 8 and (pps * page // MOBA_BLOCK) % 8 == 0

    def flat(b, h, i):
        return (b * n_hp + h) * nq + i

    q_spec = pl.BlockSpec((tq, hps * HEAD_DIM), lambda b, h, i, pt: (b * nq + _query_block_order(i, nq), h))
    kv_spec = pl.BlockSpec((seq, hps * HEAD_DIM), lambda b, h, i, pt: (b, h))
    assert n_steps >= SCAN_SLOTS
    grid_spec = pltpu.PrefetchScalarGridSpec(
        num_scalar_prefetch=1,
        grid=(batch, n_hp, nq),
        in_specs=[q_spec, kv_spec, kv_spec,
                  pl.BlockSpec((1, nb, hps * HEAD_DIM), lambda b, h, i, pt: (b, 0, h)),
                  pl.BlockSpec((1, t, attn_dim), lambda b, h, i, pt: (flat(b, h, i) // scan_steps, 0, 0)),
                  pl.BlockSpec(memory_space=pl.ANY)],
        out_specs=[
            q_spec,
            pl.BlockSpec((1, heads, 8, LANES), lambda b, h, i, pt: (flat(b, h, i) // scan_steps, 0, 0, 0)),
        ],
        scratch_shapes=[
            pltpu.VMEM((hps, seq, 2 * HEAD_DIM), BF16),
            pltpu.VMEM((hps, seq, 2 * HEAD_DIM), BF16),
            pltpu.VMEM((tq, tq), BF16),
            pltpu.VMEM((hps, tq, seq), F32),
            pltpu.VMEM((nb_past * heads, HEAD_DIM), F32),
            pltpu.VMEM((SCAN_SLOTS, pps, page * heads, HEAD_DIM), F32),
            pltpu.SemaphoreType.DMA((SCAN_SLOTS,)),
        ],
    )
    return pl.pallas_call(
        functools.partial(_attn_scan_kernel, hps=hps, heads=heads, page=page, scan_steps=scan_steps),
        grid_spec=grid_spec,
        out_shape=[
            jax.ShapeDtypeStruct((rows, attn_dim), BF16),
            jax.ShapeDtypeStruct((nseq, heads, 8, LANES), jnp.int32),
        ],
        compiler_params=_params("arbitrary", "arbitrary", "arbitrary"),
        name="attn_scan",
    )(page_table, q, k, v, ksum.reshape(batch, nb, attn_dim), q_s, cache)


def _gather_copies(pt_ref, ti_ref, ck_hbm, cv_hbm, kbuf, vbuf, sem, step, slot, k, *, units, heads, nsel,
                   wait_only=False):
    page = cv_hbm.shape[1]
    ppb = MOBA_BLOCK // page
    unit = step * units + k
    s_, h_ = unit // heads, unit % heads
    out = []
    for c in range(nsel):
        blk = None if wait_only else jnp.maximum(ti_ref[s_, h_ * nsel + c], 0)
        for e in range(ppb):
            pg = 0 if wait_only else pt_ref[s_, blk * ppb + e]
            for src, dst in ((ck_hbm, kbuf), (cv_hbm, vbuf)):
                out.append(pltpu.make_async_copy(
                    src.at[pg, :, 0 if wait_only else h_, :], dst.at[slot, (k * nsel + c) * ppb + e],
                    sem.at[slot, k]))
    return out


def _sample_attend_probs(q, kn, kblk, oks):
    t = q.shape[0]
    nsel = len(oks)
    qs = q * SOFTMAX_SCALE
    width = nsel * MOBA_BLOCK
    seg_w = MOBA_TOP_K * MOBA_BLOCK
    q16 = jnp.concatenate([qs, jnp.zeros((16 - t, HEAD_DIM), F32)], axis=0).astype(BF16)
    lt = _nt_dot(q16, kblk.astype(BF16))[0:t]
    segs = [jnp.where(oks[c], lt[:, c * MOBA_BLOCK:(c + 1) * MOBA_BLOCK], MASKED) for c in range(nsel)]
    lm = jnp.concatenate(segs, axis=1)
    col = lax.broadcasted_iota(jnp.int32, (t, width), 1)
    row = lax.broadcasted_iota(jnp.int32, (t, width), 0)
    lm = jnp.where((col >= row * seg_w) & (col < (row + 1) * seg_w), lm, MASKED)
    row_o = lax.broadcasted_iota(jnp.int32, (t, LANES), 0)
    col_o = lax.broadcasted_iota(jnp.int32, (t, LANES), 1)
    own = jnp.full((t, LANES), MASKED, F32)
    for j in range(t):
        dj = jnp.sum(qs * kn[j:j + 1, :], axis=1, keepdims=True)
        own = jnp.where((col_o == j) & (j <= row_o), dj, own)
    full = jnp.concatenate([lm, own], axis=1)
    m = jnp.max(full, axis=1, keepdims=True)
    p = jnp.exp(full - m)
    return p, jnp.sum(p, axis=1, keepdims=True)


def _sample_attend_values(p, l, vn, vblk):
    t = vn.shape[0]
    width = vblk.shape[0]
    p_pad = jnp.concatenate([p[:, :width], jnp.zeros((8 - t, width), F32)], axis=0).astype(BF16)
    pv = jnp.dot(p_pad, vblk.astype(BF16), preferred_element_type=F32)[0:t]
    for j in range(t):
        pv += p[:, width + j:width + j + 1] * vn[j:j + 1, :]
    return pv / l


def _post_attn_kernel(*refs, gather):
    if gather:
        pt_ref, ti_ref, *refs = refs
    x_ref, o_ref, mod_ref, g_ref, wo_ref, wup_ref, wdown_ref, *refs = refs
    if gather:
        qs_ref, kn_ref, vn_ref, ck_hbm, cv_hbm, y_ref, os_ref, kbuf, vbuf, sem = refs
        units, heads = gather["units"], gather["heads"]
        t = qs_ref.shape[1]
        nsel = t * MOBA_TOP_K
        assert kbuf.shape[0] == 2
        j = pl.program_id(0)
        last = pl.num_programs(0) - 1
        slot = j % 2
        copies = functools.partial(_gather_copies, pt_ref, ti_ref, ck_hbm, cv_hbm, kbuf, vbuf, sem,
                                   units=units, heads=heads, nsel=nsel)

        @pl.when(j == 0)
        def _():
            for k in range(units):
                for cp in copies(0, 0, k):
                    cp.start()

        rows_per_unit = kbuf.shape[1] // units
        probs = {}
        ahead = [copies(jnp.minimum(j + 1, last), 1 - slot, k) for k in range(units)]
        landed = [copies(j, slot, k, wait_only=True) for k in range(units)]
        valid = [[ti_ref[(j * units + k) // heads, ((j * units + k) % heads) * nsel + c] >= 0
                  for c in range(nsel)] for k in range(units)]

        def begin(k):
            for cp in ahead[k]:
                cp.start()
            for cp in landed[k]:
                cp.wait()
            cols = slice(k * HEAD_DIM, (k + 1) * HEAD_DIM)
            kblk = kbuf[slot, k * rows_per_unit:(k + 1) * rows_per_unit].reshape(nsel * MOBA_BLOCK, HEAD_DIM)
            probs[k] = _sample_attend_probs(qs_ref[0][:, cols], kn_ref[0][:, cols], kblk, valid[k])

        def finish(k):
            cols = slice(k * HEAD_DIM, (k + 1) * HEAD_DIM)
            vblk = vbuf[slot, k * rows_per_unit:(k + 1) * rows_per_unit].reshape(nsel * MOBA_BLOCK, HEAD_DIM)
            os_ref[0, :, cols] = _sample_attend_values(*probs.pop(k), vn_ref[0][:, cols], vblk)

        def after_chunk(i):
            if 0 < i <= units:
                finish(i - 1)
            if i + 1 < units:
                begin(i + 1)
            if i == n_chunks - 1:
                for k in sorted(probs):
                    finish(k)
    else:
        (y_ref,) = refs
        units = 0
        begin = after_chunk = None
    d = x_ref.shape[1]
    n_chunks = wup_ref.shape[1] // d
    m = mod_ref[0]
    a = jnp.dot(o_ref[...].astype(BF16), wo_ref[...], preferred_element_type=F32)
    x1 = x_ref[...] + m[:, 2 * d:3 * d] * a
    h = _norm_mod(x1, g_ref[...], m[:, 3 * d:4 * d], m[:, 4 * d:5 * d])
    if units > 0:
        begin(0)
    y_ref[...] = x1 + m[:, 5 * d:6 * d] * _mlp(h, wup_ref, wdown_ref, after_chunk)
    if gather:
        assert not probs and units <= n_chunks, "more sample-attention units than MLP chunks"

        @pl.when(j == last)
        def _():
            for group in ahead:
                for cp in group:
                    cp.wait()


def _post_attn(x, o, mod, g, wo, wup, wdown, *, layer, tm, tiles_per_seq, sample=None):
    rows, d = x.shape
    n_tiles = rows // tm
    row_spec = pl.BlockSpec((tm, d), lambda i, *_: (i, 0))
    in_specs = [
        row_spec,
        pl.BlockSpec((tm, o.shape[1]), lambda i, *_: (i, 0)),
        pl.BlockSpec((1, mod.shape[1], mod.shape[2]), lambda i, *_: (i // tiles_per_seq, 0, 0)),
        _const_spec((1, d)),
        _const_spec(wo.shape), _layer_spec(wup.shape, layer), _layer_spec(wdown.shape, layer),
    ]
    args = [x, o, mod, g, wo, wup, wdown]
    if sample is None:
        return pl.pallas_call(
            functools.partial(_post_attn_kernel, gather=None),
            grid=(n_tiles,),
            in_specs=in_specs,
            out_specs=row_spec,
            out_shape=jax.ShapeDtypeStruct((rows, d), F32),
            compiler_params=_params("arbitrary"),
            name="post_attn",
        )(*args)

    page_table, top_idx, q_s, k_s, v_s, cache_k, cache_v = sample
    nseq, t, attn_dim = q_s.shape
    heads = attn_dim // HEAD_DIM
    page = cache_v.shape[1]
    nsel = t * MOBA_TOP_K
    ppb = MOBA_BLOCK // page
    assert (nseq * heads) % n_tiles == 0
    units = nseq * heads // n_tiles
    assert heads % units == 0 and n_tiles >= GATHER_SLOTS
    groups = heads // units
    new_spec = pl.BlockSpec((1, t, units * HEAD_DIM), lambda i, *_: (i // groups, 0, i % groups))
    gather_buf = pltpu.VMEM((GATHER_SLOTS, units * nsel * ppb, page, HEAD_DIM), F32)
    grid_spec = pltpu.PrefetchScalarGridSpec(
        num_scalar_prefetch=2,
        grid=(n_tiles,),
        in_specs=in_specs + [new_spec, new_spec, new_spec,
                             pl.BlockSpec(memory_space=pl.ANY), pl.BlockSpec(memory_space=pl.ANY)],
        out_specs=[row_spec, new_spec],
        scratch_shapes=[gather_buf, gather_buf, pltpu.SemaphoreType.DMA((GATHER_SLOTS, units))],
    )
    return pl.pallas_call(
        functools.partial(_post_attn_kernel, gather=dict(units=units, heads=heads)),
        grid_spec=grid_spec,
        out_shape=[jax.ShapeDtypeStruct((rows, d), F32), jax.ShapeDtypeStruct((nseq, t, attn_dim), F32)],
        compiler_params=_params("arbitrary"),
        name="post_attn_gather",
    )(page_table, top_idx, *args, q_s, k_s, v_s, cache_k, cache_v)


def _conv_layer_kernel(x_ref, mod_ref, g1_ref, g2_ref, gf_ref, wbcx_ref, wconv_ref, wout_ref, wup_ref, wdown_ref,
                       *rest, tiles_per_seq, period, tail):
    if period is None:
        y_ref, ut_ref, ext = rest
    else:
        e1_ref, e2_ref, y_ref, ut_ref, ext = rest
    i = pl.program_id(0)
    tm, d = x_ref.shape
    halo = 8
    sub = min(tm, CONV_SUB_ROWS)
    subs = [slice(r, r + sub) for r in range(0, tm, sub)]
    m_all = mod_ref[0]
    ms_ = [m_all if m_all.shape[0] == 1 else m_all[s] for s in subs]
    x = [x_ref[s, :] for s in subs]
    h = [_norm_mod(x[n], g1_ref[...], ms_[n][:, 0:d], ms_[n][:, d:2 * d]).astype(BF16) for n in range(len(subs))]
    bcx = [[jnp.dot(hn, wbcx_ref[:, c * d:(c + 1) * d], preferred_element_type=F32) for c in range(3)] for hn in h]
    u = [cg * xin for _, cg, xin in bcx]

    @pl.when(i % tiles_per_seq == 0)
    def _():
        ext[0:halo, :] = jnp.zeros((halo, d), F32)

    w = wconv_ref[...]
    gated = []
    for n, s in enumerate(subs):
        ext[halo + s.start:halo + s.stop, :] = u[n]
        prev1 = ext[halo - 1 + s.start:halo - 1 + s.stop, :]
        prev2 = ext[halo - 2 + s.start:halo - 2 + s.stop, :]
        if period is not None:
            tpos = (lax.broadcasted_iota(jnp.int32, (sub, d), 0) + s.start) % period
            prev1 = jnp.where(tpos < 1, e1_ref[s, :], prev1)
            prev2 = jnp.where(tpos < 2, e2_ref[s, :], prev2)
        conv = w[0:1, :] * prev2 + w[1:2, :] * prev1 + w[2:3, :] * u[n]
        gated.append((bcx[n][0] * conv).astype(BF16))
    ext[0:halo, :] = ext[tm:tm + halo, :]
    ut_ref[0] = ext[halo + tm - tail:halo + tm, :]

    y = [jnp.dot(gn, wout_ref[...], preferred_element_type=F32) for gn in gated]
    x1 = [x[n] + ms_[n][:, 2 * d:3 * d] * y[n] for n in range(len(subs))]
    h2 = [_norm_mod(x1[n], g2_ref[...], ms_[n][:, 3 * d:4 * d], ms_[n][:, 4 * d:5 * d]) for n in range(len(subs))]
    mlp = _mlp_tiles(h2, wup_ref, wdown_ref)
    for n, s in enumerate(subs):
        x2 = x1[n] + ms_[n][:, 5 * d:6 * d] * mlp[n]
        ms2 = jnp.mean(x2 * x2, axis=-1, keepdims=True)
        y_ref[s, :] = x2 * lax.rsqrt(ms2 + RMS_EPS) * gf_ref[...]


def _conv_layer(x, mod, g1, g2, gf, wbcx, wconv, wout, wup, wdown, state_rows,
                *, layer, tm, tiles_per_seq, period, tail):
    rows, d = x.shape
    n_tiles = rows // tm
    row_spec = pl.BlockSpec((tm, d), lambda i: (i, 0))
    in_specs = [
        row_spec,
        pl.BlockSpec((1, mod.shape[1], mod.shape[2]), lambda i: (i // tiles_per_seq, 0, 0)),
        _const_spec((1, d)), _const_spec((1, d)), _const_spec((1, d)),
        _const_spec(wbcx.shape), _const_spec(wconv.shape), _const_spec(wout.shape),
        _layer_spec(wup.shape, layer), _layer_spec(wdown.shape, layer),
    ]
    args = [x, mod, g1, g2, gf, wbcx, wconv, wout, wup, wdown]
    if period is not None:
        in_specs += [row_spec, row_spec]
        args += list(state_rows)
    return pl.pallas_call(
        functools.partial(_conv_layer_kernel, tiles_per_seq=tiles_per_seq, period=period, tail=tail),
        grid=(n_tiles,),
        in_specs=in_specs,
        out_specs=[row_spec, pl.BlockSpec((1, tail, d), lambda i: (i, 0, 0))],
        out_shape=[jax.ShapeDtypeStruct((rows, d), F32), jax.ShapeDtypeStruct((n_tiles, tail, d), F32)],
        scratch_shapes=[pltpu.VMEM((tm + 8, d), F32)],
        compiler_params=_params("arbitrary"),
        name="conv_layer",
    )(*args)


def kernel(x_prompt, x_sample, cache_k, cache_v, state_conv, page_table, c_prompt, c_sample, w_ada, b_ada, g_norm,
           w_qkv, w_o, w_bcx, w_conv, w_out, w_up, w_down, g_final):
    batch, seq, d = x_prompt.shape
    nseq, t, _ = x_sample.shape
    depth = w_ada.shape[0]
    n_layers_attn, pool, page, heads, head_dim = cache_k.shape
    attn_dim = heads * head_dim
    n_pages = page_table.shape[1]
    past = n_pages * page
    kw = state_conv.shape[2] + 1
    assert depth == 2 and n_layers_attn == 1 and state_conv.shape[0] == 1, "layer 0 attention, layer 1 convolution"
    assert head_dim == HEAD_DIM and kw == 3 and t >= kw - 1
    assert seq % ROW_TILE == 0 and ROW_TILE % MOBA_BLOCK == 0 and past % MOBA_BLOCK == 0 and t <= MOBA_BLOCK
    assert seq % WIDE_ROW_TILE == 0 and WIDE_ROW_TILE % CONV_SUB_ROWS == 0 and WIDE_ROW_TILE % MOBA_BLOCK == 0
    rows_p, rows_s = batch * seq, nseq * t
    assert rows_s % 8 == 0

    pad = -(rows_s + batch) % 16
    c_all = jnp.concatenate([jnp.repeat(c_sample, t, axis=0), c_prompt, jnp.zeros((pad, d), F32)], axis=0)
    mod = _ada(c_all, w_ada, b_ada)
    mod_s = mod[:, None, :rows_s]
    mod_p = mod[:, rows_s:rows_s + batch, None, :]

    bf = lambda a: a.astype(BF16)
    wqkv, wo, wbcx, wout = bf(w_qkv[0]), bf(w_o[0]), bf(w_bcx[0]), bf(w_out[0])
    wup, wdown = bf(w_up), bf(w_down)
    g = g_norm.reshape(depth, 2, 1, d)
    gf = g_final.reshape(1, d)

    xp = x_prompt.reshape(rows_p, d)
    xs = x_sample.reshape(rows_s, d)
    tps = seq // ROW_TILE
    tab_p = _rope_tables(np.arange(seq))
    tab_s = _rope_tables(past + (np.arange(rows_s) % t))

    q_p, k_p, v_p, kb_p, vb_p, ksum = _qkv(xp, mod_p[0], g[0, 0], wqkv, tab_p, tm=WIDE_ROW_TILE,
                                           tiles_per_seq=seq // WIDE_ROW_TILE, with_ksum=True, q_dtype=BF16)
    q_s, k_s, v_s, kb_s, vb_s = _qkv(xs, mod_s[0], g[0, 0], wqkv, tab_s, tm=rows_s, tiles_per_seq=1,
                                     with_ksum=False, q_dtype=F32)
    q_s3, kn_s3, vn_s3 = (a.astype(F32).reshape(nseq, t, attn_dim) for a in (q_s, kb_s, vb_s))
    o_p, top = _attn_scan(page_table, q_p, kb_p, vb_p, ksum, q_s3,
                          cache_k.reshape(pool * page * heads, head_dim), batch=batch, seq=seq, page=page)
    top_idx = top[:, :, :t, :MOBA_TOP_K].reshape(nseq, heads * t * MOBA_TOP_K)
    sample = (page_table, top_idx, q_s3, kn_s3, vn_s3, cache_k.reshape(pool, page, heads, head_dim),
              cache_v.reshape(pool, page, heads, head_dim))
    x1_p, o_s = _post_attn(xp, o_p, mod_p[0], g[0, 1], wo, wup, wdown, layer=0, tm=ROW_TILE, tiles_per_seq=tps,
                           sample=sample)
    x1_s = _post_attn(xs, o_s.reshape(rows_s, attn_dim), mod_s[0], g[0, 1], wo, wup, wdown, layer=0,
                      tm=rows_s, tiles_per_seq=1)

    y_p, ut_p = _conv_layer(x1_p, mod_p[1], g[1, 0], g[1, 1], gf, wbcx, w_conv[0], wout, wup, wdown, None, layer=1,
                            tm=WIDE_ROW_TILE, tiles_per_seq=seq // WIDE_ROW_TILE, period=None, tail=8)
    st = state_conv[0]
    zeros = jnp.zeros((nseq, t - 1, d), F32)
    e1 = jnp.concatenate([st[:, 1:2], zeros], axis=1).reshape(rows_s, d)
    e2 = jnp.concatenate([st[:, 0:2], zeros[:, 1:]], axis=1).reshape(rows_s, d)
    y_s, ut_s = _conv_layer(x1_s, mod_s[1], g[1, 0], g[1, 1], gf, wbcx, w_conv[0], wout, wup, wdown, (e1, e2),
                            layer=1, tm=rows_s, tiles_per_seq=1, period=t, tail=rows_s)

    conv_p = ut_p.reshape(batch, seq // WIDE_ROW_TILE, 8, d)[:, -1, 8 - (kw - 1):, :]
    conv_s = ut_s.reshape(nseq, t, d)[:, t - (kw - 1):, :]
    return (
        y_p.reshape(batch, seq, d),
        y_s.reshape(nseq, t, d),
        k_p.reshape(1, batch, seq, heads, head_dim),
        v_p.reshape(1, batch, seq, heads, head_dim),
        k_s.reshape(1, nseq, t, heads, head_dim),
        v_s.reshape(1, nseq, t, heads, head_dim),
        conv_p[None],
        conv_s[None],
    )
```

```python
import functools

import numpy as np
import jax
import jax.numpy as jnp
from jax import lax
from jax.experimental import pallas as pl
from jax.experimental.pallas import tpu as pltpu

F32 = jnp.float32
BF16 = jnp.bfloat16

HEAD_DIM = 128
MOBA_BLOCK = 256
MOBA_TOP_K = 3
ROPE_DIM = HEAD_DIM // 4
ROPE_THETA = 500000.0
RMS_EPS = 1e-6
SOFTMAX_SCALE = HEAD_DIM ** -0.5
MASKED = -1e30

LANES = 128
VMEM_LIMIT = 56 * 1024 * 1024
ROW_TILE = 256
WIDE_ROW_TILE = 512
CONV_SUB_ROWS = 256
ADA_COL_TILE = 1536
SCAN_SLOTS = 3
GATHER_SLOTS = 2
ATTN_HEADS_PER_STEP = 2
ATTN_CHUNK_BLOCKS = 2

NT_DIMS = (((1,), (1,)), ((), ()))


def _nt_dot(a, b):
    return lax.dot_general(a, b, NT_DIMS, preferred_element_type=F32)


def _split_bf16(x):
    hi = x.astype(BF16)
    lo = (x - hi.astype(F32)).astype(BF16)
    return hi, lo


def _const_spec(shape):
    n = len(shape)
    return pl.BlockSpec(shape, lambda *_: (0,) * n, pipeline_mode=pl.Buffered(1))


def _layer_spec(shape, layer):
    n = len(shape) - 1
    return pl.BlockSpec((None,) + tuple(shape[1:]), lambda *_: (layer,) + (0,) * n, pipeline_mode=pl.Buffered(1))


def _params(*sem, cast_inputs=(), n_inputs=0):
    fusion = [i in cast_inputs for i in range(n_inputs)] if cast_inputs else None
    return pltpu.CompilerParams(dimension_semantics=sem, vmem_limit_bytes=VMEM_LIMIT, allow_input_fusion=fusion)


def _ada_kernel(c_ref, w_ref, b_ref, o_ref):
    rows = c_ref.shape[0]
    c_hi, c_lo = _split_bf16(c_ref[...])
    w_hi, w_lo = _split_bf16(w_ref[0])
    both = jnp.dot(jnp.concatenate([c_hi, c_lo], axis=0), w_hi, preferred_element_type=F32)
    acc = both[0:rows] + both[rows:] + jnp.dot(c_hi, w_lo, preferred_element_type=F32)
    o_ref[0] = acc + b_ref[0]


def _ada(c_all, w_ada, b_ada):
    depth, d, n6 = w_ada.shape
    rows = c_all.shape[0]
    tn = ADA_COL_TILE
    return pl.pallas_call(
        _ada_kernel,
        grid=(depth, n6 // tn),
        in_specs=[
            pl.BlockSpec((rows, d), lambda l, n: (0, 0)),
            pl.BlockSpec((1, d, tn), lambda l, n: (l, 0, n)),
            pl.BlockSpec((1, 1, tn), lambda l, n: (l, 0, n)),
        ],
        out_specs=pl.BlockSpec((1, rows, tn), lambda l, n: (l, 0, n)),
        out_shape=jax.ShapeDtypeStruct((depth, rows, n6), F32),
        compiler_params=_params("arbitrary", "arbitrary"),
        name="ada",
    )(c_all, w_ada, b_ada.reshape(depth, 1, n6))


def _norm_mod(x, g, shift, scale):
    ms = jnp.mean(x * x, axis=-1, keepdims=True)
    y = x * lax.rsqrt(ms + RMS_EPS) * g
    return y * (1.0 + scale) + shift


def _mlp(h, wup_ref, wdown_ref, after_chunk=None):
    return _mlp_tiles([h], wup_ref, wdown_ref, after_chunk)[0]


def _mlp_tiles(hs, wup_ref, wdown_ref, after_chunk=None):
    d, dff = wup_ref.shape
    hb = [h.astype(BF16) for h in hs]
    acc = [jnp.zeros((h.shape[0], d), F32) for h in hs]
    for i, c in enumerate(range(0, dff, d)):
        for n in range(len(hs)):
            u = jnp.dot(hb[n], wup_ref[:, c:c + d], preferred_element_type=F32)
            u = jnp.square(jnp.maximum(u, 0.0)).astype(BF16)
            acc[n] += jnp.dot(u, wdown_ref[c:c + d, :], preferred_element_type=F32)
        if after_chunk is not None:
            after_chunk(i)
    return acc


def _rope_tables(pos):
    half = ROPE_DIM // 2
    inv = np.float32(ROPE_THETA) ** (-np.arange(0, ROPE_DIM, 2, dtype=np.float32) / np.float32(ROPE_DIM))
    ang = (pos.astype(np.float32)[:, None] * inv.astype(np.float32)[None, :]).astype(np.float32)
    cos = np.cos(ang.astype(np.float64))
    sin = np.sin(ang.astype(np.float64))
    n = pos.shape[0]
    c = np.ones((n, HEAD_DIM), np.float64)
    s1 = np.zeros((n, HEAD_DIM), np.float64)
    s2 = np.zeros((n, HEAD_DIM), np.float64)
    c[:, :half] = cos
    c[:, half:ROPE_DIM] = cos
    s1[:, :half] = -sin
    s2[:, half:ROPE_DIM] = sin
    return tuple(jnp.asarray(t.astype(np.float32)) for t in (c, s1, s2))


def _qkv_kernel(x_ref, mod_ref, g_ref, w_ref, cos_ref, s1_ref, s2_ref, q_ref, k5_ref, v5_ref, kb_ref, vb_ref,
                *ks_ref, attn_dim):
    d = x_ref.shape[1]
    m = mod_ref[0]
    h = _norm_mod(x_ref[...], g_ref[...], m[:, 0:d], m[:, d:2 * d]).astype(BF16)
    cos, s1, s2 = cos_ref[...], s1_ref[...], s2_ref[...]
    tm = h.shape[0]
    heads = attn_dim // HEAD_DIM
    cw = 2 * HEAD_DIM
    for c in range(0, 3 * attn_dim, cw):
        r = jnp.dot(h, w_ref[:, c:c + cw], preferred_element_type=F32)
        if c < 2 * attn_dim:
            rot = []
            for u in range(0, cw, HEAD_DIM):
                z = r[:, u:u + HEAD_DIM]
                z = (z * cos + pltpu.roll(z, HEAD_DIM - ROPE_DIM // 2, 1) * s1
                     + pltpu.roll(z, ROPE_DIM // 2, 1) * s2)
                rot.append(z)
            r = jnp.concatenate(rot, axis=1)
        if c < attn_dim:
            q_ref[:, c:c + cw] = r.astype(q_ref.dtype)
            continue
        is_k = c < 2 * attn_dim
        c0 = c - (attn_dim if is_k else 2 * attn_dim)
        (kb_ref if is_k else vb_ref)[:, c0:c0 + cw] = r.astype(BF16)
        for u in range(0, cw, HEAD_DIM):
            head = (c0 + u) // HEAD_DIM
            (k5_ref if is_k else v5_ref)[pl.ds(head, tm, stride=heads), :] = r[:, u:u + HEAD_DIM]
        if is_k and ks_ref:
            nblk = tm // MOBA_BLOCK
            ks_ref[0][0, :, c0:c0 + cw] = jnp.sum(r.reshape(nblk, MOBA_BLOCK, cw), axis=1)


def _qkv(x, mod, g, w, tables, *, tm, tiles_per_seq, with_ksum, q_dtype):
    rows, d = x.shape
    attn_dim = w.shape[1] // 3
    heads = attn_dim // HEAD_DIM
    n_tiles = rows // tm
    mod_rows = mod.shape[1]
    row_spec = pl.BlockSpec((tm, attn_dim), lambda i: (i, 0))
    tok_head_spec = pl.BlockSpec((tm * heads, HEAD_DIM), lambda i: (i, 0))
    tab_spec = pl.BlockSpec((tm, HEAD_DIM), lambda i: (i % tiles_per_seq, 0))
    out_specs = [row_spec, tok_head_spec, tok_head_spec, row_spec, row_spec]
    out_shape = ([jax.ShapeDtypeStruct((rows, attn_dim), q_dtype)]
                 + [jax.ShapeDtypeStruct((rows * heads, HEAD_DIM), F32)] * 2
                 + [jax.ShapeDtypeStruct((rows, attn_dim), BF16)] * 2)
    if with_ksum:
        nblk = tm // MOBA_BLOCK
        out_specs.append(pl.BlockSpec((1, nblk, attn_dim), lambda i: (i, 0, 0)))
        out_shape.append(jax.ShapeDtypeStruct((n_tiles, nblk, attn_dim), F32))
    return pl.pallas_call(
        functools.partial(_qkv_kernel, attn_dim=attn_dim),
        grid=(n_tiles,),
        in_specs=[
            pl.BlockSpec((tm, d), lambda i: (i, 0)),
            pl.BlockSpec((1, mod_rows, mod.shape[2]), lambda i: (i // tiles_per_seq, 0, 0)),
            _const_spec((1, d)),
            _const_spec(w.shape),
            tab_spec, tab_spec, tab_spec,
        ],
        out_specs=out_specs,
        out_shape=out_shape,
        compiler_params=_params("arbitrary", cast_inputs=(3,), n_inputs=7),
        name="qkv",
    )(x, mod, g, w, *tables)


def _attn_prompt_step(first, i, q_ref, k_ref, v_ref, ks_ref, o_ref, kaug, vaug, eye, s_sc,
                      *, hps, chunk_blocks, side_work):
    tq = q_ref.shape[0]
    seq = k_ref.shape[0]
    nb = seq // MOBA_BLOCK
    ck = chunk_blocks * MOBA_BLOCK
    n_chunk_max = nb // chunk_blocks

    @pl.when(first)
    def _():
        r_id = lax.broadcasted_iota(jnp.int32, (tq, tq), 0)
        c_id = lax.broadcasted_iota(jnp.int32, (tq, tq), 1)
        eye[...] = jnp.where(r_id == c_id, 1.0, 0.0).astype(BF16)
        lane = lax.broadcasted_iota(jnp.int32, (MOBA_BLOCK, LANES), 1)
        ones_col = jnp.where(lane == 0, 1.0, 0.0).astype(BF16)
        for hh in range(hps):
            cols = slice(hh * HEAD_DIM, (hh + 1) * HEAD_DIM)

            def fill(j, carry):
                rows = pl.ds(pl.multiple_of(j * MOBA_BLOCK, MOBA_BLOCK), MOBA_BLOCK)
                kaug[hh, rows, 0:HEAD_DIM] = k_ref[rows, cols]
                kaug[hh, rows, HEAD_DIM:] = jnp.where(lane == j, 1.0, 0.0).astype(BF16)
                vaug[hh, rows, 0:HEAD_DIM] = v_ref[rows, cols]
                vaug[hh, rows, HEAD_DIM:] = ones_col
                return carry
            lax.fori_loop(0, nb, fill, 0)

    own = pl.ds(pl.multiple_of(i * MOBA_BLOCK, MOBA_BLOCK), MOBA_BLOCK)
    head_cols = [slice(hh * HEAD_DIM, (hh + 1) * HEAD_DIM) for hh in range(hps)]

    def select(hh):
        q = q_ref[:, head_cols[hh]]
        qs = (q.astype(F32) * (SOFTMAX_SCALE * np.log2(np.e))).astype(BF16)
        kmean = (ks_ref[0][:, head_cols[hh]] * (1.0 / MOBA_BLOCK)).astype(BF16)
        gate = _nt_dot(kmean, q)
        blk = lax.broadcasted_iota(jnp.int32, (nb, tq), 0)
        blk_f = blk.astype(F32)
        gate = jnp.where(blk < i, gate, -jnp.inf)
        sel = jnp.zeros((nb, tq), F32)
        for _ in range(MOBA_TOP_K):
            mx = jnp.max(gate, axis=0, keepdims=True)
            first = jnp.min(jnp.where(gate == mx, blk_f, float(nb)), axis=0, keepdims=True)
            hit = blk_f == first
            sel = jnp.where(hit & (mx > -jnp.inf), 1.0, sel)
            gate = jnp.where(hit, -jnp.inf, gate)
        sel_pad = jnp.concatenate([sel, jnp.zeros((LANES - nb, tq), F32)], axis=0).astype(BF16)
        sel_q = _nt_dot(eye[...], sel_pad)
        bias = jnp.where(sel_q > 0.5, 0.0, MASKED).astype(BF16)
        return jnp.concatenate([qs, bias], axis=1)

    def variant(n_chunks):
        heads_ = range(hps)
        qa = [select(hh) for hh in heads_]
        r_id = lax.broadcasted_iota(jnp.int32, (tq, MOBA_BLOCK), 0)
        c_id = lax.broadcasted_iota(jnp.int32, (tq, MOBA_BLOCK), 1)
        s_own = [jnp.where(c_id <= r_id, _nt_dot(qa[hh][:, 0:HEAD_DIM], kaug[hh, own, 0:HEAD_DIM]), MASKED)
                 for hh in heads_]
        run = [jnp.maximum(s[:, 0:LANES], s[:, LANES:]) for s in s_own]
        for c in range(n_chunks):
            for hh in heads_:
                s_c = _nt_dot(qa[hh], kaug[hh, c * ck:(c + 1) * ck, :])
                s_sc[hh, :, c * ck:(c + 1) * ck] = s_c
                for u in range(0, ck, LANES):
                    run[hh] = jnp.maximum(run[hh], s_c[:, u:u + LANES])
        m = [jnp.max(r, axis=1, keepdims=True) for r in run]
        acc = [jnp.dot(jnp.exp2(s_own[hh] - m[hh]).astype(BF16), vaug[hh, own, :], preferred_element_type=F32)
               for hh in heads_]
        for c in range(n_chunks):
            for hh in heads_:
                p_c = jnp.exp2(s_sc[hh, :, c * ck:(c + 1) * ck] - m[hh]).astype(BF16)
                acc[hh] += jnp.dot(p_c, vaug[hh, c * ck:(c + 1) * ck, :], preferred_element_type=F32)
        for hh in heads_:
            o_ref[:, head_cols[hh]] = (acc[hh][:, 0:HEAD_DIM] / acc[hh][:, HEAD_DIM:HEAD_DIM + 1]).astype(o_ref.dtype)

    n_needed = (i + chunk_blocks - 1) // chunk_blocks
    for n in range(n_chunk_max + 1):
        @pl.when(n_needed == n)
        def _(n=n):
            side_work()
            variant(n)


def _scan_queries(q_ref):
    t = q_ref.shape[1]
    qs = q_ref[0] * SOFTMAX_SCALE
    q_hi = qs.astype(BF16).astype(F32)
    q_lo = (qs - q_hi).astype(BF16).astype(F32)
    return jnp.concatenate([q_hi, q_lo, jnp.zeros((16 - 2 * t, qs.shape[1]), F32)], axis=0).astype(BF16)


def _cache_scan_pages(p, page_refs, ksum_sc, *, heads, page):
    pps = len(page_refs)
    ppb = MOBA_BLOCK // page
    bps = pps // ppb
    sums = [jnp.sum(ref[...].reshape(page, heads, HEAD_DIM), axis=0) for ref in page_refs]
    for b in range(bps):
        blk_sum = functools.reduce(lambda x, y: x + y, sums[b * ppb:(b + 1) * ppb])
        ksum_sc[pl.ds(pl.multiple_of((p * bps + b) * heads, heads), heads), :] = blk_sum


def _cache_scan_gate(q_ref, ti_ref, ksum_sc, *, heads):
    t = q_ref.shape[1]
    q16 = _scan_queries(q_ref)
    nb = ksum_sc.shape[0] // heads
    lane = lax.broadcasted_iota(jnp.int32, (16, LANES), 1)
    lane_f = lane.astype(F32)
    for h in range(heads):
        kmean = ksum_sc[pl.ds(h, nb, stride=heads), :] * (1.0 / MOBA_BLOCK)
        km = jnp.concatenate([kmean, jnp.zeros((LANES - nb, HEAD_DIM), F32)], axis=0)
        km_hi, km_lo = _split_bf16(km)
        q_h = q16[:, h * HEAD_DIM:(h + 1) * HEAD_DIM]
        r1 = _nt_dot(q_h, km_hi)
        r2 = _nt_dot(q_h, km_lo)
        gate = r1 + pltpu.roll(r1, 16 - t, 0) + r2
        gate = jnp.where(lane < nb, gate, -jnp.inf)
        out = jnp.zeros((16, LANES), jnp.int32)
        for kk in range(MOBA_TOP_K):
            mx = jnp.max(gate, axis=1, keepdims=True)
            first = jnp.min(jnp.where(gate == mx, lane_f, float(LANES)), axis=1, keepdims=True)
            idx = jnp.where(mx > -jnp.inf, first, -1.0).astype(jnp.int32)
            out = jnp.where(lane == kk, idx, out)
            gate = jnp.where(lane_f == first, -jnp.inf, gate)
        ti_ref[0, h] = out[0:8]


def _query_block_order(i, n):
    return jnp.where(i % 2 == 0, i // 2, n - 1 - i // 2)


def _attn_scan_kernel(pt_ref, q_ref, k_ref, v_ref, ks_ref, qs_ref, cache_hbm, o_ref, ti_ref,
                      kaug, vaug, eye, s_sc, ksum_sc, pbuf, psem, *, hps, heads, page, scan_steps):
    n_slots, pps = pbuf.shape[0], pbuf.shape[1]
    page_rows = page * heads
    i = pl.program_id(2)
    flat = (pl.program_id(0) * pl.num_programs(1) + pl.program_id(1)) * pl.num_programs(2) + i
    total = pl.num_programs(0) * pl.num_programs(1) * pl.num_programs(2)
    p = flat % scan_steps
    slot = flat % n_slots

    def page_copies(step):
        seq_id, first = step // scan_steps, (step % scan_steps) * pps
        return [pltpu.make_async_copy(
            cache_hbm.at[pl.ds(pl.multiple_of(pt_ref[seq_id, first + u] * page_rows, page_rows), page_rows), :],
            pbuf.at[step % n_slots, u], psem.at[step % n_slots]) for u in range(pps)]

    @pl.when(flat == 0)
    def _():
        for ahead in range(n_slots - 1):
            for cp in page_copies(ahead):
                cp.start()

    @pl.when(flat + n_slots - 1 < total)
    def _():
        for cp in page_copies(flat + n_slots - 1):
            cp.start()

    for cp in page_copies(flat):
        cp.wait()
    page_refs = [pbuf.at[slot, u] for u in range(pps)]

    def scan_pages():
        _cache_scan_pages(p, page_refs, ksum_sc, heads=heads, page=page)

    _attn_prompt_step(i == 0, _query_block_order(i, pl.num_programs(2)), q_ref, k_ref, v_ref, ks_ref, o_ref,
                      kaug, vaug, eye, s_sc, hps=hps, chunk_blocks=ATTN_CHUNK_BLOCKS, side_work=scan_pages)

    @pl.when(p == scan_steps - 1)
    def _():
        _cache_scan_gate(qs_ref, ti_ref, ksum_sc, heads=heads)


def _attn_scan(page_table, q, k, v, ksum, q_s, cache, *, batch, seq, page):
    rows, attn_dim = q.shape
    heads = attn_dim // HEAD_DIM
    tq = MOBA_BLOCK
    nq = seq // tq
    nb = seq // MOBA_BLOCK
    hps = ATTN_HEADS_PER_STEP
    n_hp = heads // hps
    assert nb <= LANES and nb % ATTN_CHUNK_BLOCKS == 0 and heads % hps == 0

    nseq, n_pages = page_table.shape
    t = q_s.shape[1]
    past = n_pages * page
    nb_past = past // MOBA_BLOCK
    n_steps = batch * n_hp * nq
    assert n_steps % nseq == 0
    scan_steps = n_steps // nseq
    assert n_pages % scan_steps == 0
    pps = n_pages // scan_steps
    assert (pps * page) % MOBA_BLOCK == 0 and MOBA_BLOCK % page == 0
    assert nb_past <= LANES and 2 * t <= 8 and (pps * page // MOBA_BLOCK) % 8 == 0

    def flat(b, h, i):
        return (b * n_hp + h) * nq + i

    q_spec = pl.BlockSpec((tq, hps * HEAD_DIM), lambda b, h, i, pt: (b * nq + _query_block_order(i, nq), h))
    kv_spec = pl.BlockSpec((seq, hps * HEAD_DIM), lambda b, h, i, pt: (b, h))
    assert n_steps >= SCAN_SLOTS
    grid_spec = pltpu.PrefetchScalarGridSpec(
        num_scalar_prefetch=1,
        grid=(batch, n_hp, nq),
        in_specs=[q_spec, kv_spec, kv_spec,
                  pl.BlockSpec((1, nb, hps * HEAD_DIM), lambda b, h, i, pt: (b, 0, h)),
                  pl.BlockSpec((1, t, attn_dim), lambda b, h, i, pt: (flat(b, h, i) // scan_steps, 0, 0)),
                  pl.BlockSpec(memory_space=pl.ANY)],
        out_specs=[
            q_spec,
            pl.BlockSpec((1, heads, 8, LANES), lambda b, h, i, pt: (flat(b, h, i) // scan_steps, 0, 0, 0)),
        ],
        scratch_shapes=[
            pltpu.VMEM((hps, seq, 2 * HEAD_DIM), BF16),
            pltpu.VMEM((hps, seq, 2 * HEAD_DIM), BF16),
            pltpu.VMEM((tq, tq), BF16),
            pltpu.VMEM((hps, tq, seq), F32),
            pltpu.VMEM((nb_past * heads, HEAD_DIM), F32),
            pltpu.VMEM((SCAN_SLOTS, pps, page * heads, HEAD_DIM), F32),
            pltpu.SemaphoreType.DMA((SCAN_SLOTS,)),
        ],
    )
    return pl.pallas_call(
        functools.partial(_attn_scan_kernel, hps=hps, heads=heads, page=page, scan_steps=scan_steps),
        grid_spec=grid_spec,
        out_shape=[
            jax.ShapeDtypeStruct((rows, attn_dim), BF16),
            jax.ShapeDtypeStruct((nseq, heads, 8, LANES), jnp.int32),
        ],
        compiler_params=_params("arbitrary", "arbitrary", "arbitrary"),
        name="attn_scan",
    )(page_table, q, k, v, ksum.reshape(batch, nb, attn_dim), q_s, cache)


def _gather_copies(pt_ref, ti_ref, ck_hbm, cv_hbm, kbuf, vbuf, sem, step, slot, k, *, units, heads, nsel,
                   wait_only=False):
    page = cv_hbm.shape[1]
    ppb = MOBA_BLOCK // page
    unit = step * units + k
    s_, h_ = unit // heads, unit % heads
    out = []
    for c in range(nsel):
        blk = None if wait_only else jnp.maximum(ti_ref[s_, h_ * nsel + c], 0)
        for e in range(ppb):
            pg = 0 if wait_only else pt_ref[s_, blk * ppb + e]
            for src, dst in ((ck_hbm, kbuf), (cv_hbm, vbuf)):
                out.append(pltpu.make_async_copy(
                    src.at[pg, :, 0 if wait_only else h_, :], dst.at[slot, (k * nsel + c) * ppb + e],
                    sem.at[slot, k]))
    return out


def _sample_attend_probs(q, kn, kblk, oks):
    t = q.shape[0]
    nsel = len(oks)
    qs = q * SOFTMAX_SCALE
    width = nsel * MOBA_BLOCK
    seg_w = MOBA_TOP_K * MOBA_BLOCK
    q16 = jnp.concatenate([qs, jnp.zeros((16 - t, HEAD_DIM), F32)], axis=0).astype(BF16)
    lt = _nt_dot(q16, kblk.astype(BF16))[0:t]
    segs = [jnp.where(oks[c], lt[:, c * MOBA_BLOCK:(c + 1) * MOBA_BLOCK], MASKED) for c in range(nsel)]
    lm = jnp.concatenate(segs, axis=1)
    col = lax.broadcasted_iota(jnp.int32, (t, width), 1)
    row = lax.broadcasted_iota(jnp.int32, (t, width), 0)
    lm = jnp.where((col >= row * seg_w) & (col < (row + 1) * seg_w), lm, MASKED)
    row_o = lax.broadcasted_iota(jnp.int32, (t, LANES), 0)
    col_o = lax.broadcasted_iota(jnp.int32, (t, LANES), 1)
    own = jnp.full((t, LANES), MASKED, F32)
    for j in range(t):
        dj = jnp.sum(qs * kn[j:j + 1, :], axis=1, keepdims=True)
        own = jnp.where((col_o == j) & (j <= row_o), dj, own)
    full = jnp.concatenate([lm, own], axis=1)
    m = jnp.max(full, axis=1, keepdims=True)
    p = jnp.exp(full - m)
    return p, jnp.sum(p, axis=1, keepdims=True)


def _sample_attend_values(p, l, vn, vblk):
    t = vn.shape[0]
    width = vblk.shape[0]
    p_pad = jnp.concatenate([p[:, :width], jnp.zeros((8 - t, width), F32)], axis=0).astype(BF16)
    pv = jnp.dot(p_pad, vblk.astype(BF16), preferred_element_type=F32)[0:t]
    for j in range(t):
        pv += p[:, width + j:width + j + 1] * vn[j:j + 1, :]
    return pv / l


def _post_attn_kernel(*refs, gather):
    if gather:
        pt_ref, ti_ref, *refs = refs
    x_ref, o_ref, mod_ref, g_ref, wo_ref, wup_ref, wdown_ref, *refs = refs
    if gather:
        qs_ref, kn_ref, vn_ref, ck_hbm, cv_hbm, y_ref, os_ref, kbuf, vbuf, sem = refs
        units, heads = gather["units"], gather["heads"]
        t = qs_ref.shape[1]
        nsel = t * MOBA_TOP_K
        assert kbuf.shape[0] == 2
        j = pl.program_id(0)
        last = pl.num_programs(0) - 1
        slot = j % 2
        copies = functools.partial(_gather_copies, pt_ref, ti_ref, ck_hbm, cv_hbm, kbuf, vbuf, sem,
                                   units=units, heads=heads, nsel=nsel)

        @pl.when(j == 0)
        def _():
            for k in range(units):
                for cp in copies(0, 0, k):
                    cp.start()

        rows_per_unit = kbuf.shape[1] // units
        probs = {}
        ahead = [copies(jnp.minimum(j + 1, last), 1 - slot, k) for k in range(units)]
        landed = [copies(j, slot, k, wait_only=True) for k in range(units)]
        valid = [[ti_ref[(j * units + k) // heads, ((j * units + k) % heads) * nsel + c] >= 0
                  for c in range(nsel)] for k in range(units)]

        def begin(k):
            for cp in ahead[k]:
                cp.start()
            for cp in landed[k]:
                cp.wait()
            cols = slice(k * HEAD_DIM, (k + 1) * HEAD_DIM)
            kblk = kbuf[slot, k * rows_per_unit:(k + 1) * rows_per_unit].reshape(nsel * MOBA_BLOCK, HEAD_DIM)
            probs[k] = _sample_attend_probs(qs_ref[0][:, cols], kn_ref[0][:, cols], kblk, valid[k])

        def finish(k):
            cols = slice(k * HEAD_DIM, (k + 1) * HEAD_DIM)
            vblk = vbuf[slot, k * rows_per_unit:(k + 1) * rows_per_unit].reshape(nsel * MOBA_BLOCK, HEAD_DIM)
            os_ref[0, :, cols] = _sample_attend_values(*probs.pop(k), vn_ref[0][:, cols], vblk)

        def after_chunk(i):
            if 0 < i <= units:
                finish(i - 1)
            if i + 1 < units:
                begin(i + 1)
            if i == n_chunks - 1:
                for k in sorted(probs):
                    finish(k)
    else:
        (y_ref,) = refs
        units = 0
        begin = after_chunk = None
    d = x_ref.shape[1]
    n_chunks = wup_ref.shape[1] // d
    m = mod_ref[0]
    a = jnp.dot(o_ref[...].astype(BF16), wo_ref[...], preferred_element_type=F32)
    x1 = x_ref[...] + m[:, 2 * d:3 * d] * a
    h = _norm_mod(x1, g_ref[...], m[:, 3 * d:4 * d], m[:, 4 * d:5 * d])
    if units > 0:
        begin(0)
    y_ref[...] = x1 + m[:, 5 * d:6 * d] * _mlp(h, wup_ref, wdown_ref, after_chunk)
    if gather:
        assert not probs and units <= n_chunks, "more sample-attention units than MLP chunks"

        @pl.when(j == last)
        def _():
            for group in ahead:
                for cp in group:
                    cp.wait()


def _post_attn(x, o, mod, g, wo, wup, wdown, *, layer, tm, tiles_per_seq, sample=None):
    rows, d = x.shape
    n_tiles = rows // tm
    row_spec = pl.BlockSpec((tm, d), lambda i, *_: (i, 0))
    in_specs = [
        row_spec,
        pl.BlockSpec((tm, o.shape[1]), lambda i, *_: (i, 0)),
        pl.BlockSpec((1, mod.shape[1], mod.shape[2]), lambda i, *_: (i // tiles_per_seq, 0, 0)),
        _const_spec((1, d)),
        _const_spec(wo.shape), _layer_spec(wup.shape, layer), _layer_spec(wdown.shape, layer),
    ]
    args = [x, o, mod, g, wo, wup, wdown]
    if sample is None:
        return pl.pallas_call(
            functools.partial(_post_attn_kernel, gather=None),
            grid=(n_tiles,),
            in_specs=in_specs,
            out_specs=row_spec,
            out_shape=jax.ShapeDtypeStruct((rows, d), F32),
            compiler_params=_params("arbitrary", cast_inputs=(4, 5, 6), n_inputs=7),
            name="post_attn",
        )(*args)

    page_table, top_idx, q_s, k_s, v_s, cache_k, cache_v = sample
    nseq, t, attn_dim = q_s.shape
    heads = attn_dim // HEAD_DIM
    page = cache_v.shape[1]
    nsel = t * MOBA_TOP_K
    ppb = MOBA_BLOCK // page
    assert (nseq * heads) % n_tiles == 0
    units = nseq * heads // n_tiles
    assert heads % units == 0 and n_tiles >= GATHER_SLOTS
    groups = heads // units
    new_spec = pl.BlockSpec((1, t, units * HEAD_DIM), lambda i, *_: (i // groups, 0, i % groups))
    gather_buf = pltpu.VMEM((GATHER_SLOTS, units * nsel * ppb, page, HEAD_DIM), F32)
    grid_spec = pltpu.PrefetchScalarGridSpec(
        num_scalar_prefetch=2,
        grid=(n_tiles,),
        in_specs=in_specs + [new_spec, new_spec, new_spec,
                             pl.BlockSpec(memory_space=pl.ANY), pl.BlockSpec(memory_space=pl.ANY)],
        out_specs=[row_spec, new_spec],
        scratch_shapes=[gather_buf, gather_buf, pltpu.SemaphoreType.DMA((GATHER_SLOTS, units))],
    )
    return pl.pallas_call(
        functools.partial(_post_attn_kernel, gather=dict(units=units, heads=heads)),
        grid_spec=grid_spec,
        out_shape=[jax.ShapeDtypeStruct((rows, d), F32), jax.ShapeDtypeStruct((nseq, t, attn_dim), F32)],
        compiler_params=_params("arbitrary", cast_inputs=(6, 7, 8), n_inputs=14),
        name="post_attn_gather",
    )(page_table, top_idx, *args, q_s, k_s, v_s, cache_k, cache_v)


def _conv_layer_kernel(x_ref, mod_ref, g1_ref, g2_ref, gf_ref, wbcx_ref, wconv_ref, wout_ref, wup_ref, wdown_ref,
                       *rest, tiles_per_seq, period, tail):
    if period is None:
        y_ref, ut_ref, ext = rest
    else:
        e1_ref, e2_ref, y_ref, ut_ref, ext = rest
    i = pl.program_id(0)
    tm, d = x_ref.shape
    halo = 8
    sub = min(tm, CONV_SUB_ROWS)
    subs = [slice(r, r + sub) for r in range(0, tm, sub)]
    m_all = mod_ref[0]
    ms_ = [m_all if m_all.shape[0] == 1 else m_all[s] for s in subs]
    x = [x_ref[s, :] for s in subs]
    h = [_norm_mod(x[n], g1_ref[...], ms_[n][:, 0:d], ms_[n][:, d:2 * d]).astype(BF16) for n in range(len(subs))]
    bcx = [[jnp.dot(hn, wbcx_ref[:, c * d:(c + 1) * d], preferred_element_type=F32) for c in range(3)] for hn in h]
    u = [cg * xin for _, cg, xin in bcx]

    @pl.when(i % tiles_per_seq == 0)
    def _():
        ext[0:halo, :] = jnp.zeros((halo, d), F32)

    w = wconv_ref[...]
    gated = []
    for n, s in enumerate(subs):
        ext[halo + s.start:halo + s.stop, :] = u[n]
        prev1 = ext[halo - 1 + s.start:halo - 1 + s.stop, :]
        prev2 = ext[halo - 2 + s.start:halo - 2 + s.stop, :]
        if period is not None:
            tpos = (lax.broadcasted_iota(jnp.int32, (sub, d), 0) + s.start) % period
            prev1 = jnp.where(tpos < 1, e1_ref[s, :], prev1)
            prev2 = jnp.where(tpos < 2, e2_ref[s, :], prev2)
        conv = w[0:1, :] * prev2 + w[1:2, :] * prev1 + w[2:3, :] * u[n]
        gated.append((bcx[n][0] * conv).astype(BF16))
    ext[0:halo, :] = ext[tm:tm + halo, :]
    ut_ref[0] = ext[halo + tm - tail:halo + tm, :]

    y = [jnp.dot(gn, wout_ref[...], preferred_element_type=F32) for gn in gated]
    x1 = [x[n] + ms_[n][:, 2 * d:3 * d] * y[n] for n in range(len(subs))]
    h2 = [_norm_mod(x1[n], g2_ref[...], ms_[n][:, 3 * d:4 * d], ms_[n][:, 4 * d:5 * d]) for n in range(len(subs))]
    mlp = _mlp_tiles(h2, wup_ref, wdown_ref)
    for n, s in enumerate(subs):
        x2 = x1[n] + ms_[n][:, 5 * d:6 * d] * mlp[n]
        ms2 = jnp.mean(x2 * x2, axis=-1, keepdims=True)
        y_ref[s, :] = x2 * lax.rsqrt(ms2 + RMS_EPS) * gf_ref[...]


def _conv_layer(x, mod, g1, g2, gf, wbcx, wconv, wout, wup, wdown, state_rows,
                *, layer, tm, tiles_per_seq, period, tail):
    rows, d = x.shape
    n_tiles = rows // tm
    row_spec = pl.BlockSpec((tm, d), lambda i: (i, 0))
    in_specs = [
        row_spec,
        pl.BlockSpec((1, mod.shape[1], mod.shape[2]), lambda i: (i // tiles_per_seq, 0, 0)),
        _const_spec((1, d)), _const_spec((1, d)), _const_spec((1, d)),
        _const_spec(wbcx.shape), _const_spec(wconv.shape), _const_spec(wout.shape),
        _layer_spec(wup.shape, layer), _layer_spec(wdown.shape, layer),
    ]
    args = [x, mod, g1, g2, gf, wbcx, wconv, wout, wup, wdown]
    if period is not None:
        in_specs += [row_spec, row_spec]
        args += list(state_rows)
    return pl.pallas_call(
        functools.partial(_conv_layer_kernel, tiles_per_seq=tiles_per_seq, period=period, tail=tail),
        grid=(n_tiles,),
        in_specs=in_specs,
        out_specs=[row_spec, pl.BlockSpec((1, tail, d), lambda i: (i, 0, 0))],
        out_shape=[jax.ShapeDtypeStruct((rows, d), F32), jax.ShapeDtypeStruct((n_tiles, tail, d), F32)],
        scratch_shapes=[pltpu.VMEM((tm + 8, d), F32)],
        compiler_params=_params("arbitrary", cast_inputs=(5, 7, 8, 9), n_inputs=len(args)),
        name="conv_layer",
    )(*args)


def kernel(x_prompt, x_sample, cache_k, cache_v, state_conv, page_table, c_prompt, c_sample, w_ada, b_ada, g_norm,
           w_qkv, w_o, w_bcx, w_conv, w_out, w_up, w_down, g_final):
    batch, seq, d = x_prompt.shape
    nseq, t, _ = x_sample.shape
    depth = w_ada.shape[0]
    n_layers_attn, pool, page, heads, head_dim = cache_k.shape
    attn_dim = heads * head_dim
    n_pages = page_table.shape[1]
    past = n_pages * page
    kw = state_conv.shape[2] + 1
    assert depth == 2 and n_layers_attn == 1 and state_conv.shape[0] == 1, "layer 0 attention, layer 1 convolution"
    assert head_dim == HEAD_DIM and kw == 3 and t >= kw - 1
    assert seq % ROW_TILE == 0 and ROW_TILE % MOBA_BLOCK == 0 and past % MOBA_BLOCK == 0 and t <= MOBA_BLOCK
    assert seq % WIDE_ROW_TILE == 0 and WIDE_ROW_TILE % CONV_SUB_ROWS == 0 and WIDE_ROW_TILE % MOBA_BLOCK == 0
    rows_p, rows_s = batch * seq, nseq * t
    assert rows_s % 8 == 0

    pad = -(rows_s + batch) % 16
    c_all = jnp.concatenate([jnp.repeat(c_sample, t, axis=0), c_prompt, jnp.zeros((pad, d), F32)], axis=0)
    mod = _ada(c_all, w_ada, b_ada)
    mod_s = mod[:, None, :rows_s]
    mod_p = mod[:, rows_s:rows_s + batch, None, :]

    bf = lambda a: a.astype(BF16)
    wqkv, wo, wbcx, wout = bf(w_qkv[0]), bf(w_o[0]), bf(w_bcx[0]), bf(w_out[0])
    wup, wdown = bf(w_up), bf(w_down)
    g = g_norm.reshape(depth, 2, 1, d)
    gf = g_final.reshape(1, d)

    xp = x_prompt.reshape(rows_p, d)
    xs = x_sample.reshape(rows_s, d)
    tps = seq // ROW_TILE
    tab_p = _rope_tables(np.arange(seq))
    tab_s = _rope_tables(past + (np.arange(rows_s) % t))

    q_p, k_p, v_p, kb_p, vb_p, ksum = _qkv(xp, mod_p[0], g[0, 0], wqkv, tab_p, tm=WIDE_ROW_TILE,
                                           tiles_per_seq=seq // WIDE_ROW_TILE, with_ksum=True, q_dtype=BF16)
    q_s, k_s, v_s, kb_s, vb_s = _qkv(xs, mod_s[0], g[0, 0], wqkv, tab_s, tm=rows_s, tiles_per_seq=1,
                                     with_ksum=False, q_dtype=F32)
    q_s3, kn_s3, vn_s3 = (a.astype(F32).reshape(nseq, t, attn_dim) for a in (q_s, kb_s, vb_s))
    o_p, top = _attn_scan(page_table, q_p, kb_p, vb_p, ksum, q_s3,
                          cache_k.reshape(pool * page * heads, head_dim), batch=batch, seq=seq, page=page)
    top_idx = top[:, :, :t, :MOBA_TOP_K].reshape(nseq, heads * t * MOBA_TOP_K)
    sample = (page_table, top_idx, q_s3, kn_s3, vn_s3, cache_k.reshape(pool, page, heads, head_dim),
              cache_v.reshape(pool, page, heads, head_dim))
    x1_p, o_s = _post_attn(xp, o_p, mod_p[0], g[0, 1], wo, wup, wdown, layer=0, tm=ROW_TILE, tiles_per_seq=tps,
                           sample=sample)
    x1_s = _post_attn(xs, o_s.reshape(rows_s, attn_dim), mod_s[0], g[0, 1], wo, wup, wdown, layer=0,
                      tm=rows_s, tiles_per_seq=1)

    y_p, ut_p = _conv_layer(x1_p, mod_p[1], g[1, 0], g[1, 1], gf, wbcx, w_conv[0], wout, wup, wdown, None, layer=1,
                            tm=WIDE_ROW_TILE, tiles_per_seq=seq // WIDE_ROW_TILE, period=None, tail=8)
    st = state_conv[0]
    zeros = jnp.zeros((nseq, t - 1, d), F32)
    e1 = jnp.concatenate([st[:, 1:2], zeros], axis=1).reshape(rows_s, d)
    e2 = jnp.concatenate([st[:, 0:2], zeros[:, 1:]], axis=1).reshape(rows_s, d)
    y_s, ut_s = _conv_layer(x1_s, mod_s[1], g[1, 0], g[1, 1], gf, wbcx, w_conv[0], wout, wup, wdown, (e1, e2),
                            layer=1, tm=rows_s, tiles_per_seq=1, period=t, tail=rows_s)

    conv_p = ut_p.reshape(batch, seq // WIDE_ROW_TILE, 8, d)[:, -1, 8 - (kw - 1):, :]
    conv_s = ut_s.reshape(nseq, t, d)[:, t - (kw - 1):, :]
    return (
        y_p.reshape(batch, seq, d),
        y_s.reshape(nseq, t, d),
        k_p.reshape(1, batch, seq, heads, head_dim),
        v_p.reshape(1, batch, seq, heads, head_dim),
        k_s.reshape(1, nseq, t, heads, head_dim),
        v_s.reshape(1, nseq, t, heads, head_dim),
        conv_p[None],
        conv_s[None],
    )
```
